```python
import jax, jax.numpy as jnp
from jax import lax
import numpy as np

D_MODEL = 1024
BATCH = 8
SEQ = 8192
DEPTH = 2

CONV_WIDTH = D_MODEL // 2
POOL_WIDTH = D_MODEL - CONV_WIDTH
MIX_WIDTH = CONV_WIDTH + POOL_WIDTH
IN_WIDTH = 2 * CONV_WIDTH + POOL_WIDTH
CONV_KERNEL = 31
POOL_WINDOWS = (2, 4, 8, 16)
N_POOL_GROUPS = len(POOL_WINDOWS)
POOL_GROUP_DIM = POOL_WIDTH // N_POOL_GROUPS
N_EXPERTS = 32
N_EXPERT_GROUPS = 4
EXPERTS_PER_GROUP = N_EXPERTS // N_EXPERT_GROUPS
TOP_K = 2
D_EXPERT = D_MODEL // 2
EXPERT_BLOCK = 128
LN_EPS = 1e-5
DEEPNORM_ALPHA = (2 * DEPTH) ** 0.25
DEEPNORM_BETA = (8 * DEPTH) ** -0.25

kernel_name = 'hybrid_conv_pool_moe_deepnorm'


def _layer_norm(x, g, b):
    xf = x.astype(jnp.float32)
    mu = xf.mean(-1, keepdims=True)
    var = jnp.square(xf - mu).mean(-1, keepdims=True)
    y = (xf - mu) * lax.rsqrt(var + LN_EPS) * g.astype(jnp.float32) + b.astype(jnp.float32)
    return y.astype(x.dtype)


def _conformer_conv_group(a, gate, conv_w, conv_b, ln_g, ln_b):
    v = a * jax.nn.sigmoid(gate)
    y = lax.conv_general_dilated(
        v, conv_w[:, None, :].astype(v.dtype), window_strides=(1,),
        padding=[(CONV_KERNEL - 1, 0)], dimension_numbers=('NWC', 'WIO', 'NWC'),
        feature_group_count=CONV_WIDTH) + conv_b.astype(v.dtype)
    y = _layer_norm(y, ln_g, ln_b)
    return jax.nn.silu(y)


def _multiscale_pool_group(p, pool_w, pool_scale):
    B, S, _ = p.shape
    pf = p.astype(jnp.float32).reshape(B, S, N_POOL_GROUPS, POOL_GROUP_DIM)
    csum = jnp.cumsum(pf, axis=1)
    csum = jnp.concatenate([jnp.zeros_like(csum[:, :1]), csum], axis=1)
    pos = jnp.arange(S) + 1
    outs = []
    for gi, w in enumerate(POOL_WINDOWS):
        cg = csum[:, :, gi]
        upper = cg[:, 1:]
        lower = jnp.concatenate([jnp.zeros_like(cg[:, :w - 1]), cg[:, :S + 1 - w]], axis=1)
        cnt = jnp.minimum(pos, w).astype(jnp.float32)[None, :, None]
        outs.append((upper - lower) / cnt - pf[:, :, gi])
    d = jnp.stack(outs, axis=2)
    y = jnp.einsum('bsgc,gcd->bsgd', d, pool_w.astype(jnp.float32)).reshape(B, S, POOL_WIDTH)
    return (y * pool_scale.astype(jnp.float32)).astype(p.dtype)


def _grouped_moe(u, w_router, router_bias, wg, wu, wd):
    B, S, D = u.shape
    T = B * S
    uf = u.reshape(T, D)
    scores = jax.nn.sigmoid((uf @ w_router.astype(uf.dtype)).astype(jnp.float32))
    sel = scores + router_bias.astype(jnp.float32)
    sel_g = sel.reshape(T, N_EXPERT_GROUPS, EXPERTS_PER_GROUP)
    grp_score = lax.top_k(sel_g, TOP_K)[0].sum(-1)
    g_star = jnp.argmax(grp_score, axis=-1)
    in_grp = jnp.take_along_axis(sel_g, g_star[:, None, None], axis=1)[:, 0]
    _, loc = lax.top_k(in_grp, TOP_K)
    idx = g_star[:, None] * EXPERTS_PER_GROUP + loc
    wts = jnp.take_along_axis(scores, idx, axis=1)
    wts = wts / wts.sum(-1, keepdims=True)

    A = T * TOP_K
    P = A + N_EXPERTS * EXPERT_BLOCK
    NB = P // EXPERT_BLOCK
    flat_e = idx.reshape(A).astype(jnp.int32)
    flat_tok = jnp.repeat(jnp.arange(T, dtype=jnp.int32), TOP_K)
    flat_w = wts.reshape(A)
    order = jnp.argsort(flat_e, stable=True)
    se = flat_e[order]
    counts = jnp.bincount(flat_e, length=N_EXPERTS)
    starts = jnp.cumsum(counts) - counts
    pcounts = (counts + EXPERT_BLOCK - 1) // EXPERT_BLOCK * EXPERT_BLOCK
    pends = jnp.cumsum(pcounts)
    pstarts = pends - pcounts
    dest = pstarts[se] + jnp.arange(A, dtype=jnp.int32) - starts[se]
    buf_tok = jnp.full((P,), T, jnp.int32).at[dest].set(flat_tok[order])
    buf_w = jnp.zeros((P,), jnp.float32).at[dest].set(flat_w[order])
    blk_e = jnp.clip(jnp.searchsorted(pends, jnp.arange(NB) * EXPERT_BLOCK, side='right'), 0, N_EXPERTS - 1)
    x_pad = jnp.concatenate([uf, jnp.zeros((1, D), uf.dtype)], axis=0)

    def expert_block(args):
        tok, e = args
        xb = x_pad[tok]
        h = jax.nn.silu(xb @ wg[e]) * (xb @ wu[e])
        return h @ wd[e]

    out = lax.map(expert_block, (buf_tok.reshape(NB, EXPERT_BLOCK), blk_e))
    out = out.reshape(P, D) * buf_w[:, None].astype(out.dtype)
    y = jnp.zeros((T + 1, D), out.dtype).at[buf_tok].add(out)[:T]
    return y.reshape(B, S, D)


def setup_inputs(seed: int = 0) -> dict:
    key = jax.random.key(seed)
    ks = jax.random.split(key, 23)
    f32 = jnp.float32

    def nrm(k, shape, std):
        return jax.random.normal(k, shape, f32) * std

    D, L = D_MODEL, DEPTH
    return {
        'x': nrm(ks[0], (BATCH, SEQ, D), 1.0),
        'c': nrm(ks[1], (BATCH, D), 1.0),
        'w_ada': nrm(ks[2], (L, D, 6 * D), 0.2 * D ** -0.5),
        'b_ada': nrm(ks[3], (L, 6 * D), 0.02),
        'w_in': nrm(ks[4], (L, D, IN_WIDTH), D ** -0.5),
        'b_in': nrm(ks[5], (L, IN_WIDTH), 0.02),
        'conv_w': nrm(ks[6], (L, CONV_KERNEL, CONV_WIDTH), CONV_KERNEL ** -0.5),
        'conv_b': nrm(ks[7], (L, CONV_WIDTH), 0.02),
        'conv_ln_g': 1.0 + nrm(ks[8], (L, CONV_WIDTH), 0.02),
        'conv_ln_b': nrm(ks[9], (L, CONV_WIDTH), 0.02),
        'pool_w': nrm(ks[10], (L, N_POOL_GROUPS, POOL_GROUP_DIM, POOL_GROUP_DIM), POOL_GROUP_DIM ** -0.5),
        'pool_scale': 1.0 + nrm(ks[11], (L, POOL_WIDTH), 0.02),
        'w_out': nrm(ks[12], (L, MIX_WIDTH, D), MIX_WIDTH ** -0.5 * DEEPNORM_BETA),
        'b_out': nrm(ks[13], (L, D), 0.02),
        'ln1_g': 1.0 + nrm(ks[14], (L, D), 0.02),
        'ln1_b': nrm(ks[15], (L, D), 0.02),
        'w_router': nrm(ks[16], (D, N_EXPERTS), D ** -0.5),
        'router_bias': nrm(ks[17], (N_EXPERTS,), 0.01),
        'w_gate': nrm(ks[18], (L, N_EXPERTS, D, D_EXPERT), D ** -0.5),
        'w_up': nrm(ks[19], (L, N_EXPERTS, D, D_EXPERT), D ** -0.5),
        'w_down': nrm(ks[20], (L, N_EXPERTS, D_EXPERT, D), D_EXPERT ** -0.5 * DEEPNORM_BETA),
        'ln2_g': 1.0 + nrm(ks[21], (L, D), 0.02),
        'ln2_b': nrm(ks[22], (L, D), 0.02),
    }


def reference(x, c, w_ada, b_ada, w_in, b_in, conv_w, conv_b, conv_ln_g, conv_ln_b,
              pool_w, pool_scale, w_out, b_out, ln1_g, ln1_b, w_router, router_bias,
              w_gate, w_up, w_down, ln2_g, ln2_b):
    c_act = jax.nn.silu(c)
    for l in range(DEPTH):
        mod = c_act @ w_ada[l] + b_ada[l]
        sh1, sc1, g1, sh2, sc2, g2 = [m[:, None, :] for m in jnp.split(mod, 6, axis=-1)]

        u = x * (1.0 + sc1) + sh1
        h = u @ w_in[l] + b_in[l]
        a = h[..., :CONV_WIDTH]
        gate = h[..., CONV_WIDTH:2 * CONV_WIDTH]
        p = h[..., 2 * CONV_WIDTH:]
        ya = _conformer_conv_group(a, gate, conv_w[l], conv_b[l], conv_ln_g[l], conv_ln_b[l])
        yb = _multiscale_pool_group(p, pool_w[l], pool_scale[l])
        mix = jnp.concatenate([ya, yb], axis=-1) @ w_out[l] + b_out[l]
        x = _layer_norm(DEEPNORM_ALPHA * x + (1.0 + g1) * mix, ln1_g[l], ln1_b[l])

        u2 = x * (1.0 + sc2) + sh2
        f = _grouped_moe(u2, w_router, router_bias, w_gate[l], w_up[l], w_down[l])
        x = _layer_norm(DEEPNORM_ALPHA * x + (1.0 + g2) * f, ln2_g[l], ln2_b[l])
    return x
```

```python
import functools

import jax
import jax.numpy as jnp
from jax import lax
from jax.experimental import pallas as pl
from jax.experimental.pallas import tpu as pltpu

F32 = jnp.float32
BF16 = jnp.bfloat16
I32 = jnp.int32

CONV_KERNEL = 31
POOL_WINDOWS = (2, 4, 8, 16)
N_EXPERT_GROUPS = 4
LN_EPS = 1e-5

SUBLANES = 8
LANES = 128

SEQ_TILE = 512
DISPATCH_TILE = 512
EXPERT_TILE = 512
COMBINE_TILE = 256
CONV_HALO_VREGS = 32
POOL_HALO = 16
VMEM_LIMIT = 56 * 1024 * 1024


def _split_bf16(a):
    hi = a.astype(BF16)
    lo = (a - hi.astype(F32)).astype(BF16)
    return hi, lo


def _dot(a, b):
    return jnp.dot(a, b, preferred_element_type=F32)


def _ada_body(c_ref, w_ref, b_ref, o_ref):
    c = c_ref[...]
    ca = c * jax.nn.sigmoid(c)
    chi, clo = _split_bf16(ca)
    whi, wlo = _split_bf16(w_ref[0])
    o_ref[0] = _dot(chi, whi) + _dot(chi, wlo) + _dot(clo, whi) + b_ref[0]


def _ada_mod(c, w_ada, b_ada):
    n_layers, d, n6 = w_ada.shape
    bsz = c.shape[0]
    tn = n6 // 6
    return pl.pallas_call(
        _ada_body,
        grid=(n_layers, n6 // tn),
        in_specs=[
            pl.BlockSpec((bsz, d), lambda l, j: (0, 0)),
            pl.BlockSpec((1, d, tn), lambda l, j: (l, 0, j)),
            pl.BlockSpec((1, 1, tn), lambda l, j: (l, 0, j)),
        ],
        out_specs=pl.BlockSpec((1, bsz, tn), lambda l, j: (l, 0, j)),
        out_shape=jax.ShapeDtypeStruct((n_layers, bsz, n6), F32),
        compiler_params=pltpu.CompilerParams(dimension_semantics=("arbitrary", "arbitrary"), vmem_limit_bytes=VMEM_LIMIT),
        name="ada_mod",
    )(c, w_ada, b_ada.reshape(n_layers, 1, n6))


def _layer_norm_rows(z, g, b):
    mu = jnp.mean(z, axis=-1, keepdims=True)
    zc = z - mu
    var = jnp.mean(zc * zc, axis=-1, keepdims=True)
    return zc * lax.rsqrt(var + LN_EPS) * g + b


def _mixer_body(alpha, n_experts,
                x_ref, mod_ref, win_ref, bin_ref, cw_ref, cb_ref, cg_ref, cbeta_ref, pw_ref, ps_ref,
                wout_ref, bout_ref, l1g_ref, l1b_ref, wr1_ref, wr2_ref, rb_ref, perm_ref, permt_ref, tri_ref,
                x1_ref, u2_ref, e_ref, rank_ref, wm_ref, cnt_ref,
                ebuf, vprev, cbuf, pbuf, carry):
    ts = x_ref.shape[1]
    cw = cw_ref.shape[1]
    pw = ps_ref.shape[1]
    gw = pw // len(POOL_WINDOWS)
    fine = ts // SUBLANES
    halo = CONV_HALO_VREGS
    b = pl.program_id(0)
    i = pl.program_id(1)

    @pl.when(i == 0)
    def _():
        vprev[...] = jnp.zeros(vprev.shape, F32)
        pbuf[0:POOL_HALO, :] = jnp.zeros((POOL_HALO, pw), F32)

    @pl.when(jnp.logical_and(b == 0, i == 0))
    def _():
        carry[...] = jnp.zeros(carry.shape, F32)

    x = x_ref[0]
    mod = mod_ref[0]
    sh1, sc1, g1 = mod[0:1], mod[1:2], mod[2:3]
    sh2, sc2 = mod[3:4], mod[4:5]

    u = (x * (1.0 + sc1) + sh1).astype(BF16)
    h = _dot(u, win_ref[...]) + bin_ref[...]
    a = h[:, :cw]
    gate = h[:, cw:2 * cw]
    p = h[:, 2 * cw:]

    v = (a * jax.nn.sigmoid(gate)).astype(BF16)
    vb = _dot(perm_ref[...], v)
    ebuf[halo * SUBLANES:, :] = vb
    sub = lax.broadcasted_iota(I32, (SUBLANES, cw), 0)
    for mm in range(halo):
        src = (fine - halo + mm) * SUBLANES
        cur = vb[src:src + SUBLANES, :]
        prev = vprev[mm * SUBLANES:(mm + 1) * SUBLANES, :]
        ebuf[mm * SUBLANES:(mm + 1) * SUBLANES, :] = pltpu.roll(
            jnp.where(sub == SUBLANES - 1, prev, cur), 1, 0)
    vprev[...] = vb[(fine - halo) * SUBLANES:, :]

    chunk = 8 * SUBLANES
    first = halo - (CONV_KERNEL - 1)
    for r0 in range(0, ts, chunk):
        for c0 in range(0, cw, LANES):
            acc = jnp.broadcast_to(cb_ref[:, c0:c0 + LANES], (chunk, LANES))
            for k in range(CONV_KERNEL):
                off = r0 + (first + k) * SUBLANES
                acc = acc + cw_ref[k:k + 1, c0:c0 + LANES] * ebuf[off:off + chunk, c0:c0 + LANES]
            cbuf[r0:r0 + chunk, c0:c0 + LANES] = acc
    yc = _layer_norm_rows(cbuf[...], cg_ref[...], cbeta_ref[...])
    ya_b = (yc * jax.nn.sigmoid(yc)).astype(BF16)
    ya = _dot(permt_ref[...], ya_b).astype(BF16)

    pbuf[POOL_HALO:, :] = p
    ext = pbuf[...]
    s = ext + pltpu.roll(ext, 1, 0)
    parts = [s[:, :gw]]
    shift = 2
    for _ in range(len(POOL_WINDOWS) - 1):
        s = s[:, gw:]
        s = s + pltpu.roll(s, shift, 0)
        parts.append(s[:, :gw])
        shift *= 2
    wsum = jnp.concatenate(parts, axis=1)[POOL_HALO:, :]
    pbuf[0:POOL_HALO, :] = p[ts - POOL_HALO:, :]
    pos = (lax.broadcasted_iota(I32, (ts, gw), 0) + (i * ts + 1)).astype(F32)
    cnt = jnp.concatenate([jnp.minimum(pos, float(wlen)) for wlen in POOL_WINDOWS], axis=1)
    dpool = (wsum / cnt - p).astype(BF16)
    yb_parts = [_dot(dpool[:, gi * gw:(gi + 1) * gw], pw_ref[gi]) for gi in range(len(POOL_WINDOWS))]
    yb = (jnp.concatenate(yb_parts, axis=1) * ps_ref[...]).astype(BF16)

    mix = _dot(ya, wout_ref[0:cw, :]) + _dot(yb, wout_ref[cw:, :]) + bout_ref[...]
    x1 = _layer_norm_rows(alpha * x + (1.0 + g1) * mix, l1g_ref[...], l1b_ref[...])
    x1_ref[0] = x1
    u2 = x1 * (1.0 + sc2) + sh2
    u2_ref[0] = u2

    u2h, u2l = _split_bf16(u2)
    lg = _dot(u2h, wr1_ref[...]) + _dot(u2l, wr2_ref[...])
    lgt = lg.T
    logits = lgt[0:n_experts, :] + lgt[n_experts:2 * n_experts, :]
    scores = jax.nn.sigmoid(logits)
    sel = scores + rb_ref[...]
    epg = n_experts // N_EXPERT_GROUPS
    io = lax.broadcasted_iota(I32, (epg, ts), 0).astype(F32)
    neg = jnp.float32(-jnp.inf)
    best = None
    for g in range(N_EXPERT_GROUPS):
        sg = sel[g * epg:(g + 1) * epg, :]
        scg = scores[g * epg:(g + 1) * epg, :]
        m1 = jnp.max(sg, axis=0, keepdims=True)
        i1 = jnp.min(jnp.where(sg == m1, io, float(epg)), axis=0, keepdims=True)
        rest = jnp.where(io == i1, neg, sg)
        m2 = jnp.max(rest, axis=0, keepdims=True)
        i2 = jnp.min(jnp.where(jnp.logical_and(rest == m2, io != i1), io, float(epg)), axis=0, keepdims=True)
        s1 = jnp.sum(jnp.where(io == i1, scg, 0.0), axis=0, keepdims=True)
        s2 = jnp.sum(jnp.where(io == i2, scg, 0.0), axis=0, keepdims=True)
        gs = m1 + m2
        cand = (gs, i1 + float(g * epg), i2 + float(g * epg), s1, s2)
        if best is None:
            best = cand
        else:
            upd = gs > best[0]
            best = tuple(jnp.where(upd, cn, bs) for cn, bs in zip(cand, best))
    _, e1, e2, s1, s2 = best
    denom = s1 + s2
    w1 = s1 / denom
    w2 = s2 / denom

    ioe = lax.broadcasted_iota(I32, (n_experts, ts), 0).astype(F32)
    hit1 = ioe == e1
    hit2 = ioe == e2
    onehot = jnp.where(jnp.logical_or(hit1, hit2), 1.0, 0.0)
    before = _dot(onehot.astype(BF16), tri_ref[...]) + carry[:, 0:1]
    r1 = jnp.sum(jnp.where(hit1, before, 0.0), axis=0, keepdims=True)
    r2 = jnp.sum(jnp.where(hit2, before, 0.0), axis=0, keepdims=True)
    carry[...] = carry[...] + jnp.sum(onehot, axis=1, keepdims=True)
    cnt_ref[...] = carry[...]

    e_ref[0:1, :] = e1.astype(I32)
    e_ref[1:2, :] = e2.astype(I32)
    rank_ref[0:1, :] = r1.astype(I32)
    rank_ref[1:2, :] = r2.astype(I32)
    rowi = lax.broadcasted_iota(I32, (LANES, ts), 0)
    wpad = jnp.where(rowi == 0, w1, jnp.where(rowi == 1, w2, 0.0))
    wm_ref[...] = wpad.T


def _row_perm(ts):
    r = jnp.arange(ts)
    t = (ts // SUBLANES) * (r % SUBLANES) + r // SUBLANES
    return (t[:, None] == jnp.arange(ts)[None, :]).astype(BF16)


def _mixer(x, mod, lw, wr1, wr2, rbias, alpha):
    bsz, seq, d = x.shape
    ts = min(SEQ_TILE, seq)
    ns = seq // ts
    n_tok = bsz * seq
    n_experts = rbias.shape[0]
    cw = lw["conv_w"].shape[1]
    pw = lw["pool_scale"].shape[0]
    perm = _row_perm(ts)
    tri = (jnp.arange(ts)[:, None] < jnp.arange(ts)[None, :]).astype(BF16)
    row = lambda a: a.reshape(1, -1)
    full = lambda shape: pl.BlockSpec(shape, lambda b, i: (0,) * len(shape))
    ins = [
        (x, pl.BlockSpec((1, ts, d), lambda b, i: (b, i, 0))),
        (mod, pl.BlockSpec((1, 6, d), lambda b, i: (b, 0, 0))),
        (lw["w_in"], None), (row(lw["b_in"]), None),
        (jnp.pad(lw["conv_w"], ((0, 1), (0, 0))), None), (row(lw["conv_b"]), None),
        (row(lw["conv_ln_g"]), None), (row(lw["conv_ln_b"]), None),
        (lw["pool_w"], None), (row(lw["pool_scale"]), None),
        (lw["w_out"], None), (row(lw["b_out"]), None),
        (row(lw["ln1_g"]), None), (row(lw["ln1_b"]), None),
        (wr1, None), (wr2, None), (rbias.reshape(-1, 1), None),
        (perm, None), (perm.T, None), (tri, None),
    ]
    args = [a for a, _ in ins]
    specs = [s if s is not None else full(a.shape) for a, s in ins]
    out_shape = (
        jax.ShapeDtypeStruct((bsz, seq, d), F32),
        jax.ShapeDtypeStruct((bsz, seq, d), F32),
        jax.ShapeDtypeStruct((2, n_tok), I32),
        jax.ShapeDtypeStruct((2, n_tok), I32),
        jax.ShapeDtypeStruct((n_tok, LANES), F32),
        jax.ShapeDtypeStruct((n_experts, LANES), F32),
    )
    out_specs = (
        pl.BlockSpec((1, ts, d), lambda b, i: (b, i, 0)),
        pl.BlockSpec((1, ts, d), lambda b, i: (b, i, 0)),
        pl.BlockSpec((2, ts), lambda b, i: (0, b * ns + i)),
        pl.BlockSpec((2, ts), lambda b, i: (0, b * ns + i)),
        pl.BlockSpec((ts, LANES), lambda b, i: (b * ns + i, 0)),
        pl.BlockSpec((n_experts, LANES), lambda b, i: (0, 0)),
    )
    return pl.pallas_call(
        functools.partial(_mixer_body, alpha, n_experts),
        grid=(bsz, ns),
        in_specs=specs,
        out_specs=out_specs,
        out_shape=out_shape,
        scratch_shapes=[
            pltpu.VMEM((CONV_HALO_VREGS * SUBLANES + ts, cw), F32),
            pltpu.VMEM((CONV_HALO_VREGS * SUBLANES, cw), F32),
            pltpu.VMEM((ts, cw), F32),
            pltpu.VMEM((POOL_HALO + ts, pw), F32),
            pltpu.VMEM((n_experts, LANES), F32),
        ],
        compiler_params=pltpu.CompilerParams(dimension_semantics=("arbitrary", "arbitrary"), vmem_limit_bytes=VMEM_LIMIT),
        name="mixer_router",
    )(*args)


def _dispatch_body(pstart_ref, e_ref, rank_ref, u_ref, xs_in_ref, xs_ref, sem):
    del xs_in_ref
    tn = u_ref.shape[0]

    def issue(t, carry):
        for k in range(2):
            dst = pstart_ref[e_ref[k, t]] + rank_ref[k, t]
            pltpu.make_async_copy(u_ref.at[pl.ds(t, 1)], xs_ref.at[pl.ds(dst, 1)], sem).start()
        return carry

    lax.fori_loop(0, tn, issue, 0)

    def drain(t, carry):
        for k in range(2):
            pltpu.make_async_copy(u_ref.at[pl.ds(0, 1)], xs_ref.at[pl.ds(0, 1)], sem).wait()
        return carry

    lax.fori_loop(0, tn, drain, 0)


def _dispatch(pstart, eidx, rank, u2, n_rows):
    n_tok, d = u2.shape
    tn = min(DISPATCH_TILE, n_tok)
    smem = lambda: pl.BlockSpec((2, tn), lambda i, ps: (0, i), memory_space=pltpu.SMEM)
    grid_spec = pltpu.PrefetchScalarGridSpec(
        num_scalar_prefetch=1,
        grid=(n_tok // tn,),
        in_specs=[smem(), smem(),
                  pl.BlockSpec((tn, d), lambda i, ps: (i, 0)),
                  pl.BlockSpec(memory_space=pl.ANY)],
        out_specs=pl.BlockSpec(memory_space=pl.ANY),
        scratch_shapes=[pltpu.SemaphoreType.DMA(())],
    )
    return pl.pallas_call(
        _dispatch_body,
        grid_spec=grid_spec,
        out_shape=jax.ShapeDtypeStruct((n_rows, d), F32),
        input_output_aliases={4: 0},
        compiler_params=pltpu.CompilerParams(dimension_semantics=("arbitrary",), vmem_limit_bytes=VMEM_LIMIT),
        name="dispatch_rows",
    )(pstart, eidx, rank, u2, jnp.zeros((n_rows, d), F32))


def _expert_body(blk_ref, nused_ref, xs_ref, wg_ref, wu_ref, wd_ref, ys_ref):
    j = pl.program_id(0)

    @pl.when(j < nused_ref[0])
    def _():
        xb = xs_ref[...].astype(BF16)
        g = _dot(xb, wg_ref[0])
        up = _dot(xb, wu_ref[0])
        hid = (g * jax.nn.sigmoid(g) * up).astype(BF16)
        ys_ref[...] = _dot(hid, wd_ref[0])

    @pl.when(j >= nused_ref[0])
    def _():
        ys_ref[...] = jnp.zeros(ys_ref.shape, F32)


def _experts(blk_e, nused, xs, wg, wu, wd):
    n_rows, d = xs.shape
    f = wg.shape[2]
    tm = EXPERT_TILE
    grid_spec = pltpu.PrefetchScalarGridSpec(
        num_scalar_prefetch=2,
        grid=(n_rows // tm,),
        in_specs=[
            pl.BlockSpec((tm, d), lambda j, be, nu: (jnp.minimum(j, nu[0] - 1), 0)),
            pl.BlockSpec((1, d, f), lambda j, be, nu: (be[j], 0, 0)),
            pl.BlockSpec((1, d, f), lambda j, be, nu: (be[j], 0, 0)),
            pl.BlockSpec((1, f, d), lambda j, be, nu: (be[j], 0, 0)),
        ],
        out_specs=pl.BlockSpec((tm, d), lambda j, be, nu: (j, 0)),
    )
    return pl.pallas_call(
        _expert_body,
        grid_spec=grid_spec,
        out_shape=jax.ShapeDtypeStruct((n_rows, d), F32),
        compiler_params=pltpu.CompilerParams(dimension_semantics=("arbitrary",), vmem_limit_bytes=VMEM_LIMIT),
        name="expert_ffn",
    )(blk_e, nused, xs, wg, wu, wd)


def _combine_body(alpha, pstart_ref, e_ref, rank_ref, x1_ref, mod_ref, wm_ref, g_ref, b_ref, ys_ref, o_ref, ybuf, sem):
    tn = x1_ref.shape[0]

    def issue(t, carry):
        for k in range(2):
            src = pstart_ref[e_ref[k, t]] + rank_ref[k, t]
            pltpu.make_async_copy(ys_ref.at[pl.ds(src, 1)], ybuf.at[k, pl.ds(t, 1)], sem).start()
        return carry

    lax.fori_loop(0, tn, issue, 0)

    def drain(t, carry):
        for k in range(2):
            pltpu.make_async_copy(ys_ref.at[pl.ds(0, 1)], ybuf.at[k, pl.ds(0, 1)], sem).wait()
        return carry

    lax.fori_loop(0, tn, drain, 0)

    wm = wm_ref[...]
    f = wm[:, 0:1] * ybuf[0] + wm[:, 1:2] * ybuf[1]
    g2 = mod_ref[0][5:6]
    o_ref[...] = _layer_norm_rows(alpha * x1_ref[...] + (1.0 + g2) * f, g_ref[...], b_ref[...])


def _combine(pstart, eidx, rank, x1, mod, wm, ln_g, ln_b, ys, seq, alpha):
    n_tok, d = x1.shape
    tn = min(COMBINE_TILE, seq)
    per_seq = seq // tn
    smem = lambda: pl.BlockSpec((2, tn), lambda i, ps: (0, i), memory_space=pltpu.SMEM)
    grid_spec = pltpu.PrefetchScalarGridSpec(
        num_scalar_prefetch=1,
        grid=(n_tok // tn,),
        in_specs=[smem(), smem(),
                  pl.BlockSpec((tn, d), lambda i, ps: (i, 0)),
                  pl.BlockSpec((1, 6, d), lambda i, ps: (i // per_seq, 0, 0)),
                  pl.BlockSpec((tn, LANES), lambda i, ps: (i, 0)),
                  pl.BlockSpec((1, d), lambda i, ps: (0, 0)),
                  pl.BlockSpec((1, d), lambda i, ps: (0, 0)),
                  pl.BlockSpec(memory_space=pl.ANY)],
        out_specs=pl.BlockSpec((tn, d), lambda i, ps: (i, 0)),
        scratch_shapes=[pltpu.VMEM((2, tn, d), F32), pltpu.SemaphoreType.DMA(())],
    )
    return pl.pallas_call(
        functools.partial(_combine_body, alpha),
        grid_spec=grid_spec,
        out_shape=jax.ShapeDtypeStruct((n_tok, d), F32),
        compiler_params=pltpu.CompilerParams(dimension_semantics=("arbitrary",), vmem_limit_bytes=VMEM_LIMIT),
        name="combine_ln",
    )(pstart, eidx, rank, x1, mod, wm, ln_g.reshape(1, d), ln_b.reshape(1, d), ys)


def kernel(x, c, w_ada, b_ada, w_in, b_in, conv_w, conv_b, conv_ln_g, conv_ln_b, pool_w, pool_scale, w_out, b_out, ln1_g, ln1_b, w_router, router_bias, w_gate, w_up, w_down, ln2_g, ln2_b):
    bsz, seq, d = x.shape
    depth = w_ada.shape[0]
    n_experts = w_router.shape[1]
    n_tok = bsz * seq
    alpha = float((2 * depth) ** 0.25)
    tm = EXPERT_TILE
    n_rows = 2 * n_tok + n_experts * tm

    mod_all = _ada_mod(c, w_ada, b_ada).reshape(depth, bsz, 6, d)

    wr_hi, wr_lo = _split_bf16(w_router)
    zeros = jnp.zeros((d, LANES - 2 * n_experts), BF16)
    wr1 = jnp.concatenate([wr_hi, wr_lo, zeros], axis=1)
    wr2 = jnp.concatenate([wr_hi, jnp.zeros((d, LANES - n_experts), BF16)], axis=1)

    for l in range(depth):
        lw = dict(w_in=w_in[l].astype(BF16), b_in=b_in[l], conv_w=conv_w[l], conv_b=conv_b[l],
                  conv_ln_g=conv_ln_g[l], conv_ln_b=conv_ln_b[l], pool_w=pool_w[l].astype(BF16),
                  pool_scale=pool_scale[l], w_out=w_out[l].astype(BF16), b_out=b_out[l],
                  ln1_g=ln1_g[l], ln1_b=ln1_b[l])
        mod = mod_all[l]
        x1, u2, eidx, rank, wm, cnt = _mixer(x, mod, lw, wr1, wr2, router_bias, alpha)

        counts = cnt[:, 0].astype(I32)
        tiles = (counts + tm - 1) // tm
        tile_end = jnp.cumsum(tiles)
        pstart = ((tile_end - tiles) * tm).astype(I32)
        nused = tile_end[-1:].astype(I32)
        blk_e = jnp.minimum(
            jnp.sum(tile_end[None, :] <= jnp.arange(n_rows // tm, dtype=I32)[:, None], axis=1), n_experts - 1
        ).astype(I32)

        xs = _dispatch(pstart, eidx, rank, u2.reshape(n_tok, d), n_rows)
        ys = _experts(blk_e, nused, xs, w_gate[l].astype(BF16), w_up[l].astype(BF16), w_down[l].astype(BF16))
        x = _combine(pstart, eidx, rank, x1.reshape(n_tok, d), mod, wm, ln2_g[l], ln2_b[l], ys, seq, alpha)
        x = x.reshape(bsz, seq, d)
    return x
```

```python
import functools

import jax
import jax.numpy as jnp
from jax import lax
from jax.experimental import pallas as pl
from jax.experimental.pallas import tpu as pltpu
from jax.experimental.pallas import tpu_sc as plsc

F32 = jnp.float32
BF16 = jnp.bfloat16
I32 = jnp.int32

CONV_KERNEL = 31
POOL_WINDOWS = (2, 4, 8, 16)
N_EXPERT_GROUPS = 4
LN_EPS = 1e-5

SUBLANES = 8
LANES = 128

SEQ_TILE = 512
EXPERT_TILE = 512
COMBINE_TILE = 512
SC_CORES = 2
SC_SUBCORES = 16
SC_ROWS_PER_STEP = 32
CONV_HALO_VREGS = 32
POOL_HALO = 16
VMEM_LIMIT = 56 * 1024 * 1024


def _split_bf16(a):
    hi = a.astype(BF16)
    lo = (a - hi.astype(F32)).astype(BF16)
    return hi, lo


def _dot(a, b):
    return jnp.dot(a, b, preferred_element_type=F32)


def _ada_body(c_ref, w_ref, b_ref, o_ref):
    c = c_ref[...]
    ca = c * jax.nn.sigmoid(c)
    chi, clo = _split_bf16(ca)
    whi, wlo = _split_bf16(w_ref[0])
    o_ref[0] = _dot(chi, whi) + _dot(chi, wlo) + _dot(clo, whi) + b_ref[0]


def _ada_mod(c, w_ada, b_ada):
    n_layers, d, n6 = w_ada.shape
    bsz = c.shape[0]
    tn = n6 // 6
    return pl.pallas_call(
        _ada_body,
        grid=(n_layers, n6 // tn),
        in_specs=[
            pl.BlockSpec((bsz, d), lambda l, j: (0, 0)),
            pl.BlockSpec((1, d, tn), lambda l, j: (l, 0, j)),
            pl.BlockSpec((1, 1, tn), lambda l, j: (l, 0, j)),
        ],
        out_specs=pl.BlockSpec((1, bsz, tn), lambda l, j: (l, 0, j)),
        out_shape=jax.ShapeDtypeStruct((n_layers, bsz, n6), F32),
        compiler_params=pltpu.CompilerParams(dimension_semantics=("arbitrary", "arbitrary"), vmem_limit_bytes=VMEM_LIMIT),
        name="ada_mod",
    )(c, w_ada, b_ada.reshape(n_layers, 1, n6))


def _layer_norm_rows(z, g, b):
    mu = jnp.mean(z, axis=-1, keepdims=True)
    zc = z - mu
    var = jnp.mean(zc * zc, axis=-1, keepdims=True)
    return zc * lax.rsqrt(var + LN_EPS) * g + b


def _mixer_body(alpha, n_experts,
                x_ref, mod_ref, win_ref, bin_ref, cw_ref, cb_ref, cg_ref, cbeta_ref, pw_ref, ps_ref,
                wout_ref, bout_ref, l1g_ref, l1b_ref, wr1_ref, wr2_ref, rb_ref, perm_ref, permt_ref, tri_ref,
                x1_ref, u2_ref, e_ref, rank_ref, wm_ref, cnt_ref,
                ebuf, vprev, cbuf, pbuf, carry):
    ts = x_ref.shape[1]
    cw = cw_ref.shape[1]
    pw = ps_ref.shape[1]
    gw = pw // len(POOL_WINDOWS)
    fine = ts // SUBLANES
    halo = CONV_HALO_VREGS
    b = pl.program_id(0)
    i = pl.program_id(1)

    @pl.when(i == 0)
    def _():
        vprev[...] = jnp.zeros(vprev.shape, F32)
        pbuf[0:POOL_HALO, :] = jnp.zeros((POOL_HALO, pw), F32)

    @pl.when(jnp.logical_and(b == 0, i == 0))
    def _():
        carry[...] = jnp.zeros(carry.shape, F32)

    x = x_ref[0]
    mod = mod_ref[0]
    sh1, sc1, g1 = mod[0:1], mod[1:2], mod[2:3]
    sh2, sc2 = mod[3:4], mod[4:5]

    u = (x * (1.0 + sc1) + sh1).astype(BF16)
    h = _dot(u, win_ref[...]) + bin_ref[...]
    a = h[:, :cw]
    gate = h[:, cw:2 * cw]
    p = h[:, 2 * cw:]

    v = (a * jax.nn.sigmoid(gate)).astype(BF16)
    vb = _dot(perm_ref[...], v)
    ebuf[halo * SUBLANES:, :] = vb
    sub = lax.broadcasted_iota(I32, (SUBLANES, cw), 0)
    for mm in range(halo):
        src = (fine - halo + mm) * SUBLANES
        cur = vb[src:src + SUBLANES, :]
        prev = vprev[mm * SUBLANES:(mm + 1) * SUBLANES, :]
        ebuf[mm * SUBLANES:(mm + 1) * SUBLANES, :] = pltpu.roll(
            jnp.where(sub == SUBLANES - 1, prev, cur), 1, 0)
    vprev[...] = vb[(fine - halo) * SUBLANES:, :]

    chunk = 8 * SUBLANES
    first = halo - (CONV_KERNEL - 1)
    for r0 in range(0, ts, chunk):
        for c0 in range(0, cw, LANES):
            acc = jnp.broadcast_to(cb_ref[:, c0:c0 + LANES], (chunk, LANES))
            for k in range(CONV_KERNEL):
                off = r0 + (first + k) * SUBLANES
                acc = acc + cw_ref[k:k + 1, c0:c0 + LANES] * ebuf[off:off + chunk, c0:c0 + LANES]
            cbuf[r0:r0 + chunk, c0:c0 + LANES] = acc
    yc = _layer_norm_rows(cbuf[...], cg_ref[...], cbeta_ref[...])
    ya_b = (yc * jax.nn.sigmoid(yc)).astype(BF16)
    ya = _dot(permt_ref[...], ya_b).astype(BF16)

    pbuf[POOL_HALO:, :] = p
    ext = pbuf[...]
    s = ext + pltpu.roll(ext, 1, 0)
    parts = [s[:, :gw]]
    shift = 2
    for _ in range(len(POOL_WINDOWS) - 1):
        s = s[:, gw:]
        s = s + pltpu.roll(s, shift, 0)
        parts.append(s[:, :gw])
        shift *= 2
    wsum = jnp.concatenate(parts, axis=1)[POOL_HALO:, :]
    pbuf[0:POOL_HALO, :] = p[ts - POOL_HALO:, :]
    pos = (lax.broadcasted_iota(I32, (ts, gw), 0) + (i * ts + 1)).astype(F32)
    cnt = jnp.concatenate([jnp.minimum(pos, float(wlen)) for wlen in POOL_WINDOWS], axis=1)
    dpool = (wsum / cnt - p).astype(BF16)
    yb_parts = [_dot(dpool[:, gi * gw:(gi + 1) * gw], pw_ref[gi]) for gi in range(len(POOL_WINDOWS))]
    yb = (jnp.concatenate(yb_parts, axis=1) * ps_ref[...]).astype(BF16)

    mix = _dot(ya, wout_ref[0:cw, :]) + _dot(yb, wout_ref[cw:, :]) + bout_ref[...]
    x1 = _layer_norm_rows(alpha * x + (1.0 + g1) * mix, l1g_ref[...], l1b_ref[...])
    x1_ref[0] = x1
    u2 = x1 * (1.0 + sc2) + sh2
    u2_ref[0] = u2

    u2h, u2l = _split_bf16(u2)
    lg = _dot(u2h, wr1_ref[...]) + _dot(u2l, wr2_ref[...])
    lgt = lg.T
    logits = lgt[0:n_experts, :] + lgt[n_experts:2 * n_experts, :]
    scores = jax.nn.sigmoid(logits)
    sel = scores + rb_ref[...]
    epg = n_experts // N_EXPERT_GROUPS
    io = lax.broadcasted_iota(I32, (epg, ts), 0).astype(F32)
    neg = jnp.float32(-jnp.inf)
    best = None
    for g in range(N_EXPERT_GROUPS):
        sg = sel[g * epg:(g + 1) * epg, :]
        scg = scores[g * epg:(g + 1) * epg, :]
        m1 = jnp.max(sg, axis=0, keepdims=True)
        i1 = jnp.min(jnp.where(sg == m1, io, float(epg)), axis=0, keepdims=True)
        rest = jnp.where(io == i1, neg, sg)
        m2 = jnp.max(rest, axis=0, keepdims=True)
        i2 = jnp.min(jnp.where(jnp.logical_and(rest == m2, io != i1), io, float(epg)), axis=0, keepdims=True)
        s1 = jnp.sum(jnp.where(io == i1, scg, 0.0), axis=0, keepdims=True)
        s2 = jnp.sum(jnp.where(io == i2, scg, 0.0), axis=0, keepdims=True)
        gs = m1 + m2
        cand = (gs, i1 + float(g * epg), i2 + float(g * epg), s1, s2)
        if best is None:
            best = cand
        else:
            upd = gs > best[0]
            best = tuple(jnp.where(upd, cn, bs) for cn, bs in zip(cand, best))
    _, e1, e2, s1, s2 = best
    denom = s1 + s2
    w1 = s1 / denom
    w2 = s2 / denom

    ioe = lax.broadcasted_iota(I32, (n_experts, ts), 0).astype(F32)
    hit1 = ioe == e1
    hit2 = ioe == e2
    onehot = jnp.where(jnp.logical_or(hit1, hit2), 1.0, 0.0)
    before = _dot(onehot.astype(BF16), tri_ref[...]) + carry[:, 0:1]
    r1 = jnp.sum(jnp.where(hit1, before, 0.0), axis=0, keepdims=True)
    r2 = jnp.sum(jnp.where(hit2, before, 0.0), axis=0, keepdims=True)
    carry[...] = carry[...] + jnp.sum(onehot, axis=1, keepdims=True)
    cnt_ref[...] = carry[...]

    e_ref[0:1, :] = e1.astype(I32)
    e_ref[1:2, :] = e2.astype(I32)
    rank_ref[0:1, :] = r1.astype(I32)
    rank_ref[1:2, :] = r2.astype(I32)
    rowi = lax.broadcasted_iota(I32, (LANES, ts), 0)
    wpad = jnp.where(rowi == 0, w1, jnp.where(rowi == 1, w2, 0.0))
    wm_ref[...] = wpad.T


def _row_perm(ts):
    r = jnp.arange(ts)
    t = (ts // SUBLANES) * (r % SUBLANES) + r // SUBLANES
    return (t[:, None] == jnp.arange(ts)[None, :]).astype(BF16)


def _mixer(x, mod, lw, wr1, wr2, rbias, alpha):
    bsz, seq, d = x.shape
    ts = min(SEQ_TILE, seq)
    ns = seq // ts
    n_tok = bsz * seq
    n_experts = rbias.shape[0]
    cw = lw["conv_w"].shape[1]
    pw = lw["pool_scale"].shape[0]
    perm = _row_perm(ts)
    tri = (jnp.arange(ts)[:, None] < jnp.arange(ts)[None, :]).astype(BF16)
    row = lambda a: a.reshape(1, -1)
    full = lambda shape: pl.BlockSpec(shape, lambda b, i: (0,) * len(shape))
    ins = [
        (x, pl.BlockSpec((1, ts, d), lambda b, i: (b, i, 0))),
        (mod, pl.BlockSpec((1, 6, d), lambda b, i: (b, 0, 0))),
        (lw["w_in"], None), (row(lw["b_in"]), None),
        (jnp.pad(lw["conv_w"], ((0, 1), (0, 0))), None), (row(lw["conv_b"]), None),
        (row(lw["conv_ln_g"]), None), (row(lw["conv_ln_b"]), None),
        (lw["pool_w"], None), (row(lw["pool_scale"]), None),
        (lw["w_out"], None), (row(lw["b_out"]), None),
        (row(lw["ln1_g"]), None), (row(lw["ln1_b"]), None),
        (wr1, None), (wr2, None), (rbias.reshape(-1, 1), None),
        (perm, None), (perm.T, None), (tri, None),
    ]
    args = [a for a, _ in ins]
    specs = [s if s is not None else full(a.shape) for a, s in ins]
    out_shape = (
        jax.ShapeDtypeStruct((bsz, seq, d), F32),
        jax.ShapeDtypeStruct((bsz, seq, d), F32),
        jax.ShapeDtypeStruct((2, n_tok), I32),
        jax.ShapeDtypeStruct((2, n_tok), I32),
        jax.ShapeDtypeStruct((n_tok, LANES), F32),
        jax.ShapeDtypeStruct((n_experts, LANES), F32),
    )
    out_specs = (
        pl.BlockSpec((1, ts, d), lambda b, i: (b, i, 0)),
        pl.BlockSpec((1, ts, d), lambda b, i: (b, i, 0)),
        pl.BlockSpec((2, ts), lambda b, i: (0, b * ns + i)),
        pl.BlockSpec((2, ts), lambda b, i: (0, b * ns + i)),
        pl.BlockSpec((ts, LANES), lambda b, i: (b * ns + i, 0)),
        pl.BlockSpec((n_experts, LANES), lambda b, i: (0, 0)),
    )
    return pl.pallas_call(
        functools.partial(_mixer_body, alpha, n_experts),
        grid=(bsz, ns),
        in_specs=specs,
        out_specs=out_specs,
        out_shape=out_shape,
        scratch_shapes=[
            pltpu.VMEM((CONV_HALO_VREGS * SUBLANES + ts, cw), F32),
            pltpu.VMEM((CONV_HALO_VREGS * SUBLANES, cw), F32),
            pltpu.VMEM((ts, cw), F32),
            pltpu.VMEM((POOL_HALO + ts, pw), F32),
            pltpu.VMEM((n_experts, LANES), F32),
        ],
        compiler_params=pltpu.CompilerParams(dimension_semantics=("arbitrary", "arbitrary"), vmem_limit_bytes=VMEM_LIMIT),
        name="mixer_router",
    )(*args)


def _dest_body(n_experts, pstart_ref, e_ref, rank_ref, d0_ref, d1_ref):
    e = e_ref[...]
    dest = rank_ref[...]
    for ex in range(n_experts):
        dest = dest + jnp.where(e == ex, pstart_ref[ex], 0)
    d0_ref[...] = dest[0:1, :]
    d1_ref[...] = dest[1:2, :]


def _dest_rows(pstart, eidx, rank):
    n_tok = eidx.shape[1]
    tn = min(8192, n_tok)
    grid_spec = pltpu.PrefetchScalarGridSpec(
        num_scalar_prefetch=1,
        grid=(n_tok // tn,),
        in_specs=[pl.BlockSpec((2, tn), lambda i, ps: (0, i)), pl.BlockSpec((2, tn), lambda i, ps: (0, i))],
        out_specs=(pl.BlockSpec((1, tn), lambda i, ps: (0, i)), pl.BlockSpec((1, tn), lambda i, ps: (0, i))),
    )
    d0, d1 = pl.pallas_call(
        functools.partial(_dest_body, pstart.shape[0]),
        grid_spec=grid_spec,
        out_shape=(jax.ShapeDtypeStruct((1, n_tok), I32), jax.ShapeDtypeStruct((1, n_tok), I32)),
        compiler_params=pltpu.CompilerParams(dimension_semantics=("arbitrary",)),
        name="dest_rows",
    )(pstart, eidx, rank)
    return d0.reshape(n_tok), d1.reshape(n_tok)


def _sc_mesh():
    return plsc.VectorSubcoreMesh(core_axis_name="c", subcore_axis_name="s",
                                  num_cores=SC_CORES, num_subcores=SC_SUBCORES)


def _dispatch(u2, d0, d1, n_rows):
    n_tok, d = u2.shape
    workers = SC_CORES * SC_SUBCORES
    per_w = n_tok // workers
    ch = min(SC_ROWS_PER_STEP, per_w)

    @functools.partial(
        pl.kernel, mesh=_sc_mesh(),
        out_type=jax.ShapeDtypeStruct((n_rows, d), F32),
        scratch_types=[pltpu.VMEM((ch,), I32), pltpu.VMEM((ch,), I32), pltpu.VMEM((ch, d), F32)],
        name="sc_dispatch",
    )
    def run(u_hbm, d0_hbm, d1_hbm, xs_hbm, i0, i1, rows):
        wid = lax.axis_index("s") * SC_CORES + lax.axis_index("c")
        base = wid * per_w

        @pl.loop(0, per_w // ch)
        def _(j):
            off = base + j * ch
            pltpu.sync_copy(d0_hbm.at[pl.ds(off, ch)], i0)
            pltpu.sync_copy(d1_hbm.at[pl.ds(off, ch)], i1)
            pltpu.sync_copy(u_hbm.at[pl.ds(off, ch)], rows)
            pltpu.sync_copy(rows, xs_hbm.at[i0])
            pltpu.sync_copy(rows, xs_hbm.at[i1])

    return run(u2, d0, d1)


def _gather_pairs(ys, d0, d1):
    n_tok = d0.shape[0]
    d = ys.shape[1]
    workers = SC_CORES * SC_SUBCORES
    per_w = n_tok // workers
    ch = min(SC_ROWS_PER_STEP, per_w)

    @functools.partial(
        pl.kernel, mesh=_sc_mesh(),
        out_type=(jax.ShapeDtypeStruct((n_tok, d), F32), jax.ShapeDtypeStruct((n_tok, d), F32)),
        scratch_types=[pltpu.VMEM((ch,), I32), pltpu.VMEM((ch,), I32),
                       pltpu.VMEM((ch, d), F32), pltpu.VMEM((ch, d), F32)],
        name="sc_gather",
    )
    def run(ys_hbm, d0_hbm, d1_hbm, y0_hbm, y1_hbm, i0, i1, r0, r1):
        wid = lax.axis_index("s") * SC_CORES + lax.axis_index("c")
        base = wid * per_w

        @pl.loop(0, per_w // ch)
        def _(j):
            off = base + j * ch
            pltpu.sync_copy(d0_hbm.at[pl.ds(off, ch)], i0)
            pltpu.sync_copy(d1_hbm.at[pl.ds(off, ch)], i1)
            pltpu.sync_copy(ys_hbm.at[i0], r0)
            pltpu.sync_copy(ys_hbm.at[i1], r1)
            pltpu.sync_copy(r0, y0_hbm.at[pl.ds(off, ch)])
            pltpu.sync_copy(r1, y1_hbm.at[pl.ds(off, ch)])

    return run(ys, d0, d1)


def _expert_body(blk_ref, nused_ref, xs_ref, wg_ref, wu_ref, wd_ref, ys_ref):
    j = pl.program_id(0)

    @pl.when(j < nused_ref[0])
    def _():
        xb = xs_ref[...].astype(BF16)
        g = _dot(xb, wg_ref[0])
        up = _dot(xb, wu_ref[0])
        hid = (g * jax.nn.sigmoid(g) * up).astype(BF16)
        ys_ref[...] = _dot(hid, wd_ref[0])

    @pl.when(j >= nused_ref[0])
    def _():
        ys_ref[...] = jnp.zeros(ys_ref.shape, F32)


def _experts(blk_e, nused, xs, wg, wu, wd):
    n_rows, d = xs.shape
    f = wg.shape[2]
    tm = EXPERT_TILE
    grid_spec = pltpu.PrefetchScalarGridSpec(
        num_scalar_prefetch=2,
        grid=(n_rows // tm,),
        in_specs=[
            pl.BlockSpec((tm, d), lambda j, be, nu: (jnp.minimum(j, nu[0] - 1), 0)),
            pl.BlockSpec((1, d, f), lambda j, be, nu: (be[j], 0, 0)),
            pl.BlockSpec((1, d, f), lambda j, be, nu: (be[j], 0, 0)),
            pl.BlockSpec((1, f, d), lambda j, be, nu: (be[j], 0, 0)),
        ],
        out_specs=pl.BlockSpec((tm, d), lambda j, be, nu: (j, 0)),
    )
    return pl.pallas_call(
        _expert_body,
        grid_spec=grid_spec,
        out_shape=jax.ShapeDtypeStruct((n_rows, d), F32),
        compiler_params=pltpu.CompilerParams(dimension_semantics=("arbitrary",), vmem_limit_bytes=VMEM_LIMIT),
        name="expert_ffn",
    )(blk_e, nused, xs, wg, wu, wd)


def _combine_body(alpha, x1_ref, y0_ref, y1_ref, mod_ref, wm_ref, g_ref, b_ref, o_ref):
    wm = wm_ref[...]
    f = wm[:, 0:1] * y0_ref[...] + wm[:, 1:2] * y1_ref[...]
    g2 = mod_ref[0][5:6]
    o_ref[...] = _layer_norm_rows(alpha * x1_ref[...] + (1.0 + g2) * f, g_ref[...], b_ref[...])


def _combine(x1, y0, y1, mod, wm, ln_g, ln_b, seq, alpha):
    n_tok, d = x1.shape
    tn = min(COMBINE_TILE, seq)
    per_seq = seq // tn
    rows = lambda: pl.BlockSpec((tn, d), lambda i: (i, 0))
    return pl.pallas_call(
        functools.partial(_combine_body, alpha),
        grid=(n_tok // tn,),
        in_specs=[rows(), rows(), rows(),
                  pl.BlockSpec((1, 6, d), lambda i: (i // per_seq, 0, 0)),
                  pl.BlockSpec((tn, LANES), lambda i: (i, 0)),
                  pl.BlockSpec((1, d), lambda i: (0, 0)),
                  pl.BlockSpec((1, d), lambda i: (0, 0))],
        out_specs=rows(),
        out_shape=jax.ShapeDtypeStruct((n_tok, d), F32),
        compiler_params=pltpu.CompilerParams(dimension_semantics=("arbitrary",), vmem_limit_bytes=VMEM_LIMIT),
        name="combine_ln",
    )(x1, y0, y1, mod, wm, ln_g.reshape(1, d), ln_b.reshape(1, d))


def kernel(x, c, w_ada, b_ada, w_in, b_in, conv_w, conv_b, conv_ln_g, conv_ln_b, pool_w, pool_scale, w_out, b_out, ln1_g, ln1_b, w_router, router_bias, w_gate, w_up, w_down, ln2_g, ln2_b):
    bsz, seq, d = x.shape
    depth = w_ada.shape[0]
    n_experts = w_router.shape[1]
    n_tok = bsz * seq
    alpha = float((2 * depth) ** 0.25)
    tm = EXPERT_TILE
    n_rows = 2 * n_tok + n_experts * tm

    mod_all = _ada_mod(c, w_ada, b_ada).reshape(depth, bsz, 6, d)

    wr_hi, wr_lo = _split_bf16(w_router)
    zeros = jnp.zeros((d, LANES - 2 * n_experts), BF16)
    wr1 = jnp.concatenate([wr_hi, wr_lo, zeros], axis=1)
    wr2 = jnp.concatenate([wr_hi, jnp.zeros((d, LANES - n_experts), BF16)], axis=1)

    for l in range(depth):
        lw = dict(w_in=w_in[l].astype(BF16), b_in=b_in[l], conv_w=conv_w[l], conv_b=conv_b[l],
                  conv_ln_g=conv_ln_g[l], conv_ln_b=conv_ln_b[l], pool_w=pool_w[l].astype(BF16),
                  pool_scale=pool_scale[l], w_out=w_out[l].astype(BF16), b_out=b_out[l],
                  ln1_g=ln1_g[l], ln1_b=ln1_b[l])
        mod = mod_all[l]
        x1, u2, eidx, rank, wm, cnt = _mixer(x, mod, lw, wr1, wr2, router_bias, alpha)

        counts = cnt[:, 0].astype(I32)
        tiles = (counts + tm - 1) // tm
        tile_end = jnp.cumsum(tiles)
        pstart = ((tile_end - tiles) * tm).astype(I32)
        nused = tile_end[-1:].astype(I32)
        blk_e = jnp.minimum(
            jnp.sum(tile_end[None, :] <= jnp.arange(n_rows // tm, dtype=I32)[:, None], axis=1), n_experts - 1
        ).astype(I32)

        d0, d1 = _dest_rows(pstart, eidx, rank)
        xs = _dispatch(u2.reshape(n_tok, d), d0, d1, n_rows)
        ys = _experts(blk_e, nused, xs, w_gate[l].astype(BF16), w_up[l].astype(BF16), w_down[l].astype(BF16))
        y0, y1 = _gather_pairs(ys, d0, d1)
        x = _combine(x1.reshape(n_tok, d), y0, y1, mod, wm, ln2_g[l], ln2_b[l], seq, alpha)
        x = x.reshape(bsz, seq, d)
    return x
```

```python
import functools

import jax
import jax.numpy as jnp
from jax import lax
from jax.experimental import pallas as pl
from jax.experimental.pallas import tpu as pltpu
from jax.experimental.pallas import tpu_sc as plsc

F32 = jnp.float32
BF16 = jnp.bfloat16
I32 = jnp.int32

CONV_KERNEL = 31
POOL_WINDOWS = (2, 4, 8, 16)
N_EXPERT_GROUPS = 4
LN_EPS = 1e-5

SUBLANES = 8
LANES = 128

SEQ_TILE = 512
EXPERT_TILE = 512
COMBINE_TILE = 512
SC_CORES = 2
SC_SUBCORES = 16
SC_ROWS_PER_STEP = 32
CONV_HALO_VREGS = 32
POOL_HALO = 16
VMEM_LIMIT = 56 * 1024 * 1024


def _split_bf16(a):
    hi = a.astype(BF16)
    lo = (a - hi.astype(F32)).astype(BF16)
    return hi, lo


def _dot(a, b):
    return jnp.dot(a, b, preferred_element_type=F32)


def _ada_body(c_ref, w_ref, b_ref, o_ref):
    c = c_ref[...]
    ca = c * jax.nn.sigmoid(c)
    chi, clo = _split_bf16(ca)
    whi, wlo = _split_bf16(w_ref[0])
    o_ref[0] = _dot(chi, whi) + _dot(chi, wlo) + _dot(clo, whi) + b_ref[0]


def _ada_mod(c, w_ada, b_ada):
    n_layers, d, n6 = w_ada.shape
    bsz = c.shape[0]
    tn = n6 // 6
    return pl.pallas_call(
        _ada_body,
        grid=(n_layers, n6 // tn),
        in_specs=[
            pl.BlockSpec((bsz, d), lambda l, j: (0, 0)),
            pl.BlockSpec((1, d, tn), lambda l, j: (l, 0, j)),
            pl.BlockSpec((1, 1, tn), lambda l, j: (l, 0, j)),
        ],
        out_specs=pl.BlockSpec((1, bsz, tn), lambda l, j: (l, 0, j)),
        out_shape=jax.ShapeDtypeStruct((n_layers, bsz, n6), F32),
        compiler_params=pltpu.CompilerParams(dimension_semantics=("arbitrary", "arbitrary"), vmem_limit_bytes=VMEM_LIMIT),
        name="ada_mod",
    )(c, w_ada, b_ada.reshape(n_layers, 1, n6))


def _layer_norm_rows(z, g, b):
    mu = jnp.mean(z, axis=-1, keepdims=True)
    zc = z - mu
    var = jnp.mean(zc * zc, axis=-1, keepdims=True)
    return zc * lax.rsqrt(var + LN_EPS) * g + b


def _mixer_body(alpha, n_experts,
                x_ref, mod_ref, win_ref, bin_ref, cw_ref, cb_ref, cg_ref, cbeta_ref, pw_ref, ps_ref,
                wout_ref, bout_ref, l1g_ref, l1b_ref, wr1_ref, wr2_ref, rb_ref, perm_ref, permt_ref, tri_ref,
                x1_ref, u2_ref, e_ref, rank_ref, wm_ref, cnt_ref,
                ebuf, vprev, cbuf, pbuf, carry):
    ts = x_ref.shape[1]
    cw = cw_ref.shape[1]
    pw = ps_ref.shape[1]
    gw = pw // len(POOL_WINDOWS)
    fine = ts // SUBLANES
    halo = CONV_HALO_VREGS
    b = pl.program_id(0)
    i = pl.program_id(1)

    @pl.when(i == 0)
    def _():
        vprev[...] = jnp.zeros(vprev.shape, F32)
        pbuf[0:POOL_HALO, :] = jnp.zeros((POOL_HALO, pw), F32)

    @pl.when(jnp.logical_and(b == 0, i == 0))
    def _():
        carry[...] = jnp.zeros(carry.shape, F32)

    x = x_ref[0]
    mod = mod_ref[0]
    sh1, sc1, g1 = mod[0:1], mod[1:2], mod[2:3]
    sh2, sc2 = mod[3:4], mod[4:5]

    u = (x * (1.0 + sc1) + sh1).astype(BF16)
    h = _dot(u, win_ref[...]) + bin_ref[...]
    a = h[:, :cw]
    gate = h[:, cw:2 * cw]
    p = h[:, 2 * cw:]

    v = (a * jax.nn.sigmoid(gate)).astype(BF16)
    vb = _dot(perm_ref[...], v)
    ebuf[halo * SUBLANES:, :] = vb
    sub = lax.broadcasted_iota(I32, (SUBLANES, cw), 0)
    for mm in range(halo):
        src = (fine - halo + mm) * SUBLANES
        cur = vb[src:src + SUBLANES, :]
        prev = vprev[mm * SUBLANES:(mm + 1) * SUBLANES, :]
        ebuf[mm * SUBLANES:(mm + 1) * SUBLANES, :] = pltpu.roll(
            jnp.where(sub == SUBLANES - 1, prev, cur), 1, 0)
    vprev[...] = vb[(fine - halo) * SUBLANES:, :]

    chunk = 8 * SUBLANES
    first = halo - (CONV_KERNEL - 1)
    for r0 in range(0, ts, chunk):
        for c0 in range(0, cw, LANES):
            acc = jnp.broadcast_to(cb_ref[:, c0:c0 + LANES], (chunk, LANES))
            for k in range(CONV_KERNEL):
                off = r0 + (first + k) * SUBLANES
                acc = acc + cw_ref[k:k + 1, c0:c0 + LANES] * ebuf[off:off + chunk, c0:c0 + LANES]
            cbuf[r0:r0 + chunk, c0:c0 + LANES] = acc
    yc = _layer_norm_rows(cbuf[...], cg_ref[...], cbeta_ref[...])
    ya_b = (yc * jax.nn.sigmoid(yc)).astype(BF16)
    ya = _dot(permt_ref[...], ya_b).astype(BF16)

    pbuf[POOL_HALO:, :] = p
    ext = pbuf[...]
    s = ext + pltpu.roll(ext, 1, 0)
    parts = [s[:, :gw]]
    shift = 2
    for _ in range(len(POOL_WINDOWS) - 1):
        s = s[:, gw:]
        s = s + pltpu.roll(s, shift, 0)
        parts.append(s[:, :gw])
        shift *= 2
    wsum = jnp.concatenate(parts, axis=1)[POOL_HALO:, :]
    pbuf[0:POOL_HALO, :] = p[ts - POOL_HALO:, :]
    pos = (lax.broadcasted_iota(I32, (ts, gw), 0) + (i * ts + 1)).astype(F32)
    cnt = jnp.concatenate([jnp.minimum(pos, float(wlen)) for wlen in POOL_WINDOWS], axis=1)
    dpool = (wsum / cnt - p).astype(BF16)
    yb_parts = [_dot(dpool[:, gi * gw:(gi + 1) * gw], pw_ref[gi]) for gi in range(len(POOL_WINDOWS))]
    yb = (jnp.concatenate(yb_parts, axis=1) * ps_ref[...]).astype(BF16)

    mix = _dot(ya, wout_ref[0:cw, :]) + _dot(yb, wout_ref[cw:, :]) + bout_ref[...]
    x1 = _layer_norm_rows(alpha * x + (1.0 + g1) * mix, l1g_ref[...], l1b_ref[...])
    x1_ref[0] = x1
    u2 = x1 * (1.0 + sc2) + sh2
    u2_ref[0] = u2

    u2h, u2l = _split_bf16(u2)
    lg = _dot(u2h, wr1_ref[...]) + _dot(u2l, wr2_ref[...])
    lgt = lg.T
    logits = lgt[0:n_experts, :] + lgt[n_experts:2 * n_experts, :]
    scores = jax.nn.sigmoid(logits)
    sel = scores + rb_ref[...]
    epg = n_experts // N_EXPERT_GROUPS
    io = lax.broadcasted_iota(I32, (epg, ts), 0).astype(F32)
    neg = jnp.float32(-jnp.inf)
    best = None
    for g in range(N_EXPERT_GROUPS):
        sg = sel[g * epg:(g + 1) * epg, :]
        scg = scores[g * epg:(g + 1) * epg, :]
        m1 = jnp.max(sg, axis=0, keepdims=True)
        i1 = jnp.min(jnp.where(sg == m1, io, float(epg)), axis=0, keepdims=True)
        rest = jnp.where(io == i1, neg, sg)
        m2 = jnp.max(rest, axis=0, keepdims=True)
        i2 = jnp.min(jnp.where(jnp.logical_and(rest == m2, io != i1), io, float(epg)), axis=0, keepdims=True)
        s1 = jnp.sum(jnp.where(io == i1, scg, 0.0), axis=0, keepdims=True)
        s2 = jnp.sum(jnp.where(io == i2, scg, 0.0), axis=0, keepdims=True)
        gs = m1 + m2
        cand = (gs, i1 + float(g * epg), i2 + float(g * epg), s1, s2)
        if best is None:
            best = cand
        else:
            upd = gs > best[0]
            best = tuple(jnp.where(upd, cn, bs) for cn, bs in zip(cand, best))
    _, e1, e2, s1, s2 = best
    denom = s1 + s2
    w1 = s1 / denom
    w2 = s2 / denom

    ioe = lax.broadcasted_iota(I32, (n_experts, ts), 0).astype(F32)
    hit1 = ioe == e1
    hit2 = ioe == e2
    onehot = jnp.where(jnp.logical_or(hit1, hit2), 1.0, 0.0)
    before = _dot(onehot.astype(BF16), tri_ref[...]) + carry[:, 0:1]
    r1 = jnp.sum(jnp.where(hit1, before, 0.0), axis=0, keepdims=True)
    r2 = jnp.sum(jnp.where(hit2, before, 0.0), axis=0, keepdims=True)
    carry[...] = carry[...] + jnp.sum(onehot, axis=1, keepdims=True)
    cnt_ref[...] = carry[...]

    e_ref[0:1, :] = e1.astype(I32)
    e_ref[1:2, :] = e2.astype(I32)
    rank_ref[0:1, :] = r1.astype(I32)
    rank_ref[1:2, :] = r2.astype(I32)
    rowi = lax.broadcasted_iota(I32, (LANES, ts), 0)
    wpad = jnp.where(rowi == 0, w1, jnp.where(rowi == 1, w2, 0.0))
    wm_ref[...] = wpad.T


def _row_perm(ts):
    r = jnp.arange(ts)
    t = (ts // SUBLANES) * (r % SUBLANES) + r // SUBLANES
    return (t[:, None] == jnp.arange(ts)[None, :]).astype(BF16)


def _mixer(x, mod, lw, wr1, wr2, rbias, alpha):
    bsz, seq, d = x.shape
    ts = min(SEQ_TILE, seq)
    ns = seq // ts
    n_tok = bsz * seq
    n_experts = rbias.shape[0]
    cw = lw["conv_w"].shape[1]
    pw = lw["pool_scale"].shape[0]
    perm = _row_perm(ts)
    tri = (jnp.arange(ts)[:, None] < jnp.arange(ts)[None, :]).astype(BF16)
    row = lambda a: a.reshape(1, -1)
    full = lambda shape: pl.BlockSpec(shape, lambda b, i: (0,) * len(shape))
    ins = [
        (x, pl.BlockSpec((1, ts, d), lambda b, i: (b, i, 0))),
        (mod, pl.BlockSpec((1, 6, d), lambda b, i: (b, 0, 0))),
        (lw["w_in"], None), (row(lw["b_in"]), None),
        (jnp.pad(lw["conv_w"], ((0, 1), (0, 0))), None), (row(lw["conv_b"]), None),
        (row(lw["conv_ln_g"]), None), (row(lw["conv_ln_b"]), None),
        (lw["pool_w"], None), (row(lw["pool_scale"]), None),
        (lw["w_out"], None), (row(lw["b_out"]), None),
        (row(lw["ln1_g"]), None), (row(lw["ln1_b"]), None),
        (wr1, None), (wr2, None), (rbias.reshape(-1, 1), None),
        (perm, None), (perm.T, None), (tri, None),
    ]
    args = [a for a, _ in ins]
    specs = [s if s is not None else full(a.shape) for a, s in ins]
    out_shape = (
        jax.ShapeDtypeStruct((bsz, seq, d), F32),
        jax.ShapeDtypeStruct((bsz, seq, d), F32),
        jax.ShapeDtypeStruct((2, n_tok), I32),
        jax.ShapeDtypeStruct((2, n_tok), I32),
        jax.ShapeDtypeStruct((n_tok, LANES), F32),
        jax.ShapeDtypeStruct((n_experts, LANES), F32),
    )
    out_specs = (
        pl.BlockSpec((1, ts, d), lambda b, i: (b, i, 0)),
        pl.BlockSpec((1, ts, d), lambda b, i: (b, i, 0)),
        pl.BlockSpec((2, ts), lambda b, i: (0, b * ns + i)),
        pl.BlockSpec((2, ts), lambda b, i: (0, b * ns + i)),
        pl.BlockSpec((ts, LANES), lambda b, i: (b * ns + i, 0)),
        pl.BlockSpec((n_experts, LANES), lambda b, i: (0, 0)),
    )
    return pl.pallas_call(
        functools.partial(_mixer_body, alpha, n_experts),
        grid=(bsz, ns),
        in_specs=specs,
        out_specs=out_specs,
        out_shape=out_shape,
        scratch_shapes=[
            pltpu.VMEM((CONV_HALO_VREGS * SUBLANES + ts, cw), F32),
            pltpu.VMEM((CONV_HALO_VREGS * SUBLANES, cw), F32),
            pltpu.VMEM((ts, cw), F32),
            pltpu.VMEM((POOL_HALO + ts, pw), F32),
            pltpu.VMEM((n_experts, LANES), F32),
        ],
        compiler_params=pltpu.CompilerParams(dimension_semantics=("arbitrary", "arbitrary"), vmem_limit_bytes=VMEM_LIMIT),
        name="mixer_router",
    )(*args)


def _dest_body(n_experts, pstart_ref, e_ref, rank_ref, dest_ref):
    e = e_ref[...]
    dest = rank_ref[...]
    for ex in range(n_experts):
        dest = dest + jnp.where(e == ex, pstart_ref[ex], 0)
    dest_ref[...] = dest


def _dest_rows(pstart, eidx, rank):
    n_tok = eidx.shape[1]
    tn = min(8192, n_tok)
    grid_spec = pltpu.PrefetchScalarGridSpec(
        num_scalar_prefetch=1,
        grid=(n_tok // tn,),
        in_specs=[pl.BlockSpec((2, tn), lambda i, ps: (0, i)), pl.BlockSpec((2, tn), lambda i, ps: (0, i))],
        out_specs=pl.BlockSpec((2, tn), lambda i, ps: (0, i)),
    )
    return pl.pallas_call(
        functools.partial(_dest_body, pstart.shape[0]),
        grid_spec=grid_spec,
        out_shape=jax.ShapeDtypeStruct((2, n_tok), I32),
        compiler_params=pltpu.CompilerParams(dimension_semantics=("arbitrary",)),
        name="dest_rows",
    )(pstart, eidx, rank)


def _sc_mesh():
    return plsc.VectorSubcoreMesh(core_axis_name="c", subcore_axis_name="s",
                                  num_cores=SC_CORES, num_subcores=SC_SUBCORES)


def _dispatch(u2, d0, d1, n_rows):
    n_tok, d = u2.shape
    workers = SC_CORES * SC_SUBCORES
    per_w = n_tok // workers
    ch = SC_ROWS_PER_STEP
    nch = per_w // ch

    @functools.partial(
        pl.kernel, mesh=_sc_mesh(),
        out_type=jax.ShapeDtypeStruct((n_rows, d), F32),
        scratch_types=[pltpu.VMEM((nch, ch), I32), pltpu.VMEM((nch, ch), I32), pltpu.VMEM((2, ch, d), F32),
                       pltpu.SemaphoreType.DMA((2,)), pltpu.SemaphoreType.DMA((2,))],
        name="sc_dispatch",
    )
    def run(u_hbm, d0_hbm, d1_hbm, xs_hbm, i0, i1, rows, rsem, ssem):
        wid = lax.axis_index("s") * SC_CORES + lax.axis_index("c")
        base = wid * per_w
        pltpu.sync_copy(d0_hbm.at[pl.ds(wid * nch, nch)], i0)
        pltpu.sync_copy(d1_hbm.at[pl.ds(wid * nch, nch)], i1)

        def read(j, b):
            return pltpu.make_async_copy(u_hbm.at[pl.ds(base + j * ch, ch)], rows.at[b], rsem.at[b])

        def scatter(idx, j, b):
            return pltpu.make_async_copy(rows.at[b], xs_hbm.at[idx.at[j]], ssem.at[b])

        read(0, 0).start()

        @pl.loop(0, nch, step=2)
        def _(j0):
            for b in range(2):
                j = j0 + b
                read(j, b).wait()
                scatter(i0, j, b).start()
                scatter(i1, j, b).start()

                @pl.when(j >= 1)
                def _():
                    scatter(i0, j - 1, 1 - b).wait()
                    scatter(i1, j - 1, 1 - b).wait()

                @pl.when(j + 1 < nch)
                def _():
                    read(j + 1, 1 - b).start()

        scatter(i0, nch - 1, 1).wait()
        scatter(i1, nch - 1, 1).wait()

    return run(u2, d0.reshape(n_tok // ch, ch), d1.reshape(n_tok // ch, ch))


def _gather_rows(ys, dd):
    n_out = dd.shape[0]
    d = ys.shape[1]
    workers = SC_CORES * SC_SUBCORES
    per_w = n_out // workers
    ch = SC_ROWS_PER_STEP
    nch = per_w // ch

    @functools.partial(
        pl.kernel, mesh=_sc_mesh(),
        out_type=jax.ShapeDtypeStruct((n_out, d), F32),
        scratch_types=[pltpu.VMEM((nch, ch), I32), pltpu.VMEM((2, ch, d), F32),
                       pltpu.SemaphoreType.DMA((2,)), pltpu.SemaphoreType.DMA((2,))],
        name="sc_gather",
    )
    def run(ys_hbm, dd_hbm, out_hbm, idx, rows, gsem, wsem):
        wid = lax.axis_index("s") * SC_CORES + lax.axis_index("c")
        base = wid * per_w
        pltpu.sync_copy(dd_hbm.at[pl.ds(wid * nch, nch)], idx)

        def gather(j, b):
            return pltpu.make_async_copy(ys_hbm.at[idx.at[j]], rows.at[b], gsem.at[b])

        def write(j, b):
            return pltpu.make_async_copy(rows.at[b], out_hbm.at[pl.ds(base + j * ch, ch)], wsem.at[b])

        gather(0, 0).start()

        @pl.loop(0, nch, step=2)
        def _(j0):
            for b in range(2):
                j = j0 + b
                gather(j, b).wait()
                write(j, b).start()

                @pl.when(j >= 1)
                def _():
                    write(j - 1, 1 - b).wait()

                @pl.when(j + 1 < nch)
                def _():
                    gather(j + 1, 1 - b).start()

        write(nch - 1, 1).wait()

    return run(ys, dd.reshape(n_out // ch, ch))


def _expert_body(blk_ref, nused_ref, xs_ref, wg_ref, wu_ref, wd_ref, ys_ref, wgb, wub, wdb):
    j = pl.program_id(0)
    used = j < nused_ref[0]
    new_expert = jnp.logical_or(j == 0, blk_ref[j] != blk_ref[jnp.maximum(j - 1, 0)])

    @pl.when(jnp.logical_and(used, new_expert))
    def _():
        wgb[...] = wg_ref[0].astype(BF16)
        wub[...] = wu_ref[0].astype(BF16)
        wdb[...] = wd_ref[0].astype(BF16)

    @pl.when(used)
    def _():
        xb = xs_ref[...].astype(BF16)
        g = _dot(xb, wgb[...])
        up = _dot(xb, wub[...])
        hid = (g * jax.nn.sigmoid(g) * up).astype(BF16)
        ys_ref[...] = _dot(hid, wdb[...])

    @pl.when(j >= nused_ref[0])
    def _():
        ys_ref[...] = jnp.zeros(ys_ref.shape, F32)


def _experts(blk_e, nused, xs, wg, wu, wd):
    n_rows, d = xs.shape
    f = wg.shape[2]
    tm = EXPERT_TILE
    grid_spec = pltpu.PrefetchScalarGridSpec(
        num_scalar_prefetch=2,
        grid=(n_rows // tm,),
        in_specs=[
            pl.BlockSpec((tm, d), lambda j, be, nu: (jnp.minimum(j, nu[0] - 1), 0)),
            pl.BlockSpec((1, d, f), lambda j, be, nu: (be[j], 0, 0)),
            pl.BlockSpec((1, d, f), lambda j, be, nu: (be[j], 0, 0)),
            pl.BlockSpec((1, f, d), lambda j, be, nu: (be[j], 0, 0)),
        ],
        out_specs=pl.BlockSpec((tm, d), lambda j, be, nu: (j, 0)),
        scratch_shapes=[pltpu.VMEM((d, f), BF16), pltpu.VMEM((d, f), BF16), pltpu.VMEM((f, d), BF16)],
    )
    return pl.pallas_call(
        _expert_body,
        grid_spec=grid_spec,
        out_shape=jax.ShapeDtypeStruct((n_rows, d), F32),
        compiler_params=pltpu.CompilerParams(dimension_semantics=("arbitrary",), vmem_limit_bytes=VMEM_LIMIT),
        name="expert_ffn",
    )(blk_e, nused, xs, wg, wu, wd)


def _combine_body(alpha, x1_ref, y0_ref, y1_ref, mod_ref, wm_ref, g_ref, b_ref, o_ref):
    wm = wm_ref[...]
    f = wm[:, 0:1] * y0_ref[...] + wm[:, 1:2] * y1_ref[...]
    g2 = mod_ref[0][5:6]
    o_ref[...] = _layer_norm_rows(alpha * x1_ref[...] + (1.0 + g2) * f, g_ref[...], b_ref[...])


def _combine(x1, yy, mod, wm, ln_g, ln_b, seq, alpha):
    n_tok, d = x1.shape
    tn = min(COMBINE_TILE, seq)
    per_seq = seq // tn
    nblk = n_tok // tn
    rows = lambda: pl.BlockSpec((tn, d), lambda i: (i, 0))
    return pl.pallas_call(
        functools.partial(_combine_body, alpha),
        grid=(nblk,),
        in_specs=[rows(), rows(), pl.BlockSpec((tn, d), lambda i: (nblk + i, 0)),
                  pl.BlockSpec((1, 6, d), lambda i: (i // per_seq, 0, 0)),
                  pl.BlockSpec((tn, LANES), lambda i: (i, 0)),
                  pl.BlockSpec((1, d), lambda i: (0, 0)),
                  pl.BlockSpec((1, d), lambda i: (0, 0))],
        out_specs=rows(),
        out_shape=jax.ShapeDtypeStruct((n_tok, d), F32),
        compiler_params=pltpu.CompilerParams(dimension_semantics=("arbitrary",), vmem_limit_bytes=VMEM_LIMIT),
        name="combine_ln",
    )(x1, yy, yy, mod, wm, ln_g.reshape(1, d), ln_b.reshape(1, d))


def kernel(x, c, w_ada, b_ada, w_in, b_in, conv_w, conv_b, conv_ln_g, conv_ln_b, pool_w, pool_scale, w_out, b_out, ln1_g, ln1_b, w_router, router_bias, w_gate, w_up, w_down, ln2_g, ln2_b):
    bsz, seq, d = x.shape
    depth = w_ada.shape[0]
    n_experts = w_router.shape[1]
    n_tok = bsz * seq
    alpha = float((2 * depth) ** 0.25)
    tm = EXPERT_TILE
    n_rows = 2 * n_tok + n_experts * tm

    mod_all = _ada_mod(c, w_ada, b_ada).reshape(depth, bsz, 6, d)

    wr_hi, wr_lo = _split_bf16(w_router)
    zeros = jnp.zeros((d, LANES - 2 * n_experts), BF16)
    wr1 = jnp.concatenate([wr_hi, wr_lo, zeros], axis=1)
    wr2 = jnp.concatenate([wr_hi, jnp.zeros((d, LANES - n_experts), BF16)], axis=1)

    for l in range(depth):
        lw = dict(w_in=w_in[l].astype(BF16), b_in=b_in[l], conv_w=conv_w[l], conv_b=conv_b[l],
                  conv_ln_g=conv_ln_g[l], conv_ln_b=conv_ln_b[l], pool_w=pool_w[l].astype(BF16),
                  pool_scale=pool_scale[l], w_out=w_out[l].astype(BF16), b_out=b_out[l],
                  ln1_g=ln1_g[l], ln1_b=ln1_b[l])
        mod = mod_all[l]
        x1, u2, eidx, rank, wm, cnt = _mixer(x, mod, lw, wr1, wr2, router_bias, alpha)

        counts = cnt[:, 0].astype(I32)
        tiles = (counts + tm - 1) // tm
        tile_end = jnp.cumsum(tiles)
        pstart = ((tile_end - tiles) * tm).astype(I32)
        nused = tile_end[-1:].astype(I32)
        blk_e = jnp.minimum(
            jnp.sum(tile_end[None, :] <= jnp.arange(n_rows // tm, dtype=I32)[:, None], axis=1), n_experts - 1
        ).astype(I32)

        dest = _dest_rows(pstart, eidx, rank)
        xs = _dispatch(u2.reshape(n_tok, d), dest[0], dest[1], n_rows)
        ys = _experts(blk_e, nused, xs, w_gate[l], w_up[l], w_down[l])
        yy = _gather_rows(ys, dest.reshape(2 * n_tok))
        x = _combine(x1.reshape(n_tok, d), yy, mod, wm, ln2_g[l], ln2_b[l], seq, alpha)
        x = x.reshape(bsz, seq, d)
    return x
```

```python
import functools

import jax
import jax.numpy as jnp
from jax import lax
from jax.experimental import pallas as pl
from jax.experimental.pallas import tpu as pltpu
from jax.experimental.pallas import tpu_sc as plsc

F32 = jnp.float32
BF16 = jnp.bfloat16
I32 = jnp.int32

CONV_KERNEL = 31
POOL_WINDOWS = (2, 4, 8, 16)
N_EXPERT_GROUPS = 4
LN_EPS = 1e-5

SUBLANES = 8
LANES = 128

SEQ_TILE = 512
SEQ_SUB_TILE = 256
EXPERT_TILE = 512
COMBINE_TILE = 512
BATCH_CHUNKS = 2
CONV_HALO_VREGS = 32
POOL_HALO = 16
VMEM_LIMIT = 56 * 1024 * 1024
SC_CORES = 2
SC_SUBCORES = 16
SC_ROWS_PER_STEP = 32


def _split_bf16(a):
    hi = a.astype(BF16)
    lo = (a - hi.astype(F32)).astype(BF16)
    return hi, lo


def _dot(a, b):
    return jnp.dot(a, b, preferred_element_type=F32)


def _ada_body(c_ref, w_ref, b_ref, o_ref):
    c = c_ref[...]
    ca = c * jax.nn.sigmoid(c)
    chi, clo = _split_bf16(ca)
    whi, wlo = _split_bf16(w_ref[0])
    o_ref[0] = _dot(chi, whi) + _dot(chi, wlo) + _dot(clo, whi) + b_ref[0]


def _ada_mod(c, w_ada, b_ada):
    n_layers, d, n6 = w_ada.shape
    bsz = c.shape[0]
    tn = n6 // 6
    return pl.pallas_call(
        _ada_body,
        grid=(n_layers, n6 // tn),
        in_specs=[
            pl.BlockSpec((bsz, d), lambda l, j: (0, 0)),
            pl.BlockSpec((1, d, tn), lambda l, j: (l, 0, j)),
            pl.BlockSpec((1, 1, tn), lambda l, j: (l, 0, j)),
        ],
        out_specs=pl.BlockSpec((1, bsz, tn), lambda l, j: (l, 0, j)),
        out_shape=jax.ShapeDtypeStruct((n_layers, bsz, n6), F32),
        compiler_params=pltpu.CompilerParams(dimension_semantics=("arbitrary", "arbitrary"), vmem_limit_bytes=VMEM_LIMIT),
        name="ada_mod",
    )(c, w_ada, b_ada.reshape(n_layers, 1, n6))


def _layer_norm_rows(z, g, b):
    mu = jnp.mean(z, axis=-1, keepdims=True)
    zc = z - mu
    var = jnp.mean(zc * zc, axis=-1, keepdims=True)
    return zc * lax.rsqrt(var + LN_EPS) * g + b


def _mixer_body(alpha, n_experts, sub,
                x_ref, mod_ref, win_ref, bin_ref, cw_ref, cb_ref, cg_ref, cbeta_ref, pw_ref, ps_ref,
                wout_ref, bout_ref, l1g_ref, l1b_ref, wr_ref, rb_ref, perm_ref, permt_ref, tri_ref,
                x1_ref, u2_ref, e_ref, rank_ref, wm_ref, cnt_ref,
                ebuf, cbuf, pbuf, vprev, phalo, carry):
    ts = x_ref.shape[1]
    cw = cw_ref.shape[1]
    pw = ps_ref.shape[1]
    gw = pw // len(POOL_WINDOWS)
    fine = sub // SUBLANES
    halo = CONV_HALO_VREGS
    b = pl.program_id(0)
    i = pl.program_id(1)

    @pl.when(i == 0)
    def _():
        vprev[...] = jnp.zeros(vprev.shape, F32)
        phalo[...] = jnp.zeros(phalo.shape, F32)

    @pl.when(jnp.logical_and(b == 0, i == 0))
    def _():
        carry[...] = jnp.zeros(carry.shape, F32)

    mod = mod_ref[0]
    sh1, sc1, g1 = mod[0:1], mod[1:2], mod[2:3]
    sh2, sc2 = mod[3:4], mod[4:5]

    for s in range(ts // sub):
        r0 = s * sub
        x = x_ref[0, r0:r0 + sub, :]
        u = (x * (1.0 + sc1) + sh1).astype(BF16)
        h = _dot(u, win_ref[...]) + bin_ref[...]
        a = h[:, :cw]
        gate = h[:, cw:2 * cw]
        p = h[:, 2 * cw:]

        v = (a * jax.nn.sigmoid(gate)).astype(BF16)
        vb = _dot(perm_ref[...], v)
        ebuf[s, halo * SUBLANES:, :] = vb
        sl = lax.broadcasted_iota(I32, (SUBLANES, cw), 0)
        for mm in range(halo):
            src = (fine - halo + mm) * SUBLANES
            cur = vb[src:src + SUBLANES, :]
            prev = vprev[mm * SUBLANES:(mm + 1) * SUBLANES, :]
            ebuf[s, mm * SUBLANES:(mm + 1) * SUBLANES, :] = pltpu.roll(
                jnp.where(sl == SUBLANES - 1, prev, cur), 1, 0)
        vprev[...] = vb[(fine - halo) * SUBLANES:, :]
        pbuf[s, 0:POOL_HALO, :] = phalo[...]
        pbuf[s, POOL_HALO:, :] = p
        phalo[...] = p[sub - POOL_HALO:, :]

    for s in range(ts // sub):
        r0 = s * sub
        x = x_ref[0, r0:r0 + sub, :]
        chunk = 8 * SUBLANES
        first = halo - (CONV_KERNEL - 1)
        for q0 in range(0, sub, chunk):
            for c0 in range(0, cw, LANES):
                acc = jnp.broadcast_to(cb_ref[:, c0:c0 + LANES], (chunk, LANES))
                for k in range(CONV_KERNEL):
                    off = q0 + (first + k) * SUBLANES
                    acc = acc + cw_ref[k:k + 1, c0:c0 + LANES] * ebuf[s, off:off + chunk, c0:c0 + LANES]
                cbuf[s, q0:q0 + chunk, c0:c0 + LANES] = acc
        yc = _layer_norm_rows(cbuf[s], cg_ref[...], cbeta_ref[...])
        ya_b = (yc * jax.nn.sigmoid(yc)).astype(BF16)
        ya = _dot(permt_ref[...], ya_b).astype(BF16)

        ext = pbuf[s]
        p = ext[POOL_HALO:, :]
        w_sum = ext + pltpu.roll(ext, 1, 0)
        parts = [w_sum[:, :gw]]
        shift = 2
        for _ in range(len(POOL_WINDOWS) - 1):
            w_sum = w_sum[:, gw:]
            w_sum = w_sum + pltpu.roll(w_sum, shift, 0)
            parts.append(w_sum[:, :gw])
            shift *= 2
        wsum = jnp.concatenate(parts, axis=1)[POOL_HALO:, :]
        pos = (lax.broadcasted_iota(I32, (sub, gw), 0) + (i * ts + r0 + 1)).astype(F32)
        cnt = jnp.concatenate([jnp.minimum(pos, float(wlen)) for wlen in POOL_WINDOWS], axis=1)
        dpool = (wsum / cnt - p).astype(BF16)
        yb_parts = [_dot(dpool[:, gi * gw:(gi + 1) * gw], pw_ref[gi]) for gi in range(len(POOL_WINDOWS))]
        yb = (jnp.concatenate(yb_parts, axis=1) * ps_ref[...]).astype(BF16)

        mix = _dot(ya, wout_ref[0:cw, :]) + _dot(yb, wout_ref[cw:, :]) + bout_ref[...]
        x1 = _layer_norm_rows(alpha * x + (1.0 + g1) * mix, l1g_ref[...], l1b_ref[...])
        x1_ref[0, r0:r0 + sub, :] = x1
        u2 = x1 * (1.0 + sc2) + sh2
        u2_ref[0, r0:r0 + sub, :] = u2

        logits = _dot(u2.astype(BF16), wr_ref[...]).T[0:n_experts, :]
        scores = jax.nn.sigmoid(logits)
        sel = scores + rb_ref[...]
        epg = n_experts // N_EXPERT_GROUPS
        io = lax.broadcasted_iota(I32, (epg, sub), 0).astype(F32)
        neg = jnp.float32(-jnp.inf)
        best = None
        for g in range(N_EXPERT_GROUPS):
            sg = sel[g * epg:(g + 1) * epg, :]
            scg = scores[g * epg:(g + 1) * epg, :]
            m1 = jnp.max(sg, axis=0, keepdims=True)
            i1 = jnp.min(jnp.where(sg == m1, io, float(epg)), axis=0, keepdims=True)
            rest = jnp.where(io == i1, neg, sg)
            m2 = jnp.max(rest, axis=0, keepdims=True)
            i2 = jnp.min(jnp.where(jnp.logical_and(rest == m2, io != i1), io, float(epg)), axis=0, keepdims=True)
            s1 = jnp.sum(jnp.where(io == i1, scg, 0.0), axis=0, keepdims=True)
            s2 = jnp.sum(jnp.where(io == i2, scg, 0.0), axis=0, keepdims=True)
            gs = m1 + m2
            cand = (gs, i1 + float(g * epg), i2 + float(g * epg), s1, s2)
            if best is None:
                best = cand
            else:
                upd = gs > best[0]
                best = tuple(jnp.where(upd, cn, bs) for cn, bs in zip(cand, best))
        _, e1, e2, s1, s2 = best
        denom = s1 + s2
        w1 = s1 / denom
        w2 = s2 / denom

        ioe = lax.broadcasted_iota(I32, (n_experts, sub), 0).astype(F32)
        hit1 = ioe == e1
        hit2 = ioe == e2
        onehot = jnp.where(jnp.logical_or(hit1, hit2), 1.0, 0.0)
        before = _dot(onehot.astype(BF16), tri_ref[...]) + carry[:, 0:1]
        r1 = jnp.sum(jnp.where(hit1, before, 0.0), axis=0, keepdims=True)
        r2 = jnp.sum(jnp.where(hit2, before, 0.0), axis=0, keepdims=True)
        carry[...] = carry[...] + jnp.sum(onehot, axis=1, keepdims=True)

        e_ref[0:1, r0:r0 + sub] = e1.astype(I32)
        e_ref[1:2, r0:r0 + sub] = e2.astype(I32)
        rank_ref[0:1, r0:r0 + sub] = r1.astype(I32)
        rank_ref[1:2, r0:r0 + sub] = r2.astype(I32)
        rowi = lax.broadcasted_iota(I32, (LANES, sub), 0)
        wpad = jnp.where(rowi == 0, w1, jnp.where(rowi == 1, w2, 0.0))
        wm_ref[r0:r0 + sub, :] = wpad.T

    cnt_ref[...] = carry[...]


def _row_perm(n):
    r = jnp.arange(n)
    t = (n // SUBLANES) * (r % SUBLANES) + r // SUBLANES
    return (t[:, None] == jnp.arange(n)[None, :]).astype(BF16)


def _mixer(x, mod, lw, wr, rbias, alpha, bsz, x_b0, mod_b0):
    _, seq, d = x.shape
    ts = min(SEQ_TILE, seq)
    sub = min(SEQ_SUB_TILE, ts)
    nsub = ts // sub
    ns = seq // ts
    n_tok = bsz * seq
    n_experts = rbias.shape[0]
    cw = lw["conv_w"].shape[1]
    pw = lw["pool_scale"].shape[0]
    perm = _row_perm(sub)
    tri = (jnp.arange(sub)[:, None] < jnp.arange(sub)[None, :]).astype(BF16)
    row = lambda a: a.reshape(1, -1)
    full = lambda shape: pl.BlockSpec(shape, lambda b, i: (0,) * len(shape))
    ins = [
        (x, pl.BlockSpec((1, ts, d), lambda b, i: (b + x_b0, i, 0))),
        (mod, pl.BlockSpec((1, 6, d), lambda b, i: (b + mod_b0, 0, 0))),
        (lw["w_in"], None), (row(lw["b_in"]), None),
        (jnp.pad(lw["conv_w"], ((0, 1), (0, 0))), None), (row(lw["conv_b"]), None),
        (row(lw["conv_ln_g"]), None), (row(lw["conv_ln_b"]), None),
        (lw["pool_w"], None), (row(lw["pool_scale"]), None),
        (lw["w_out"], None), (row(lw["b_out"]), None),
        (row(lw["ln1_g"]), None), (row(lw["ln1_b"]), None),
        (wr, None), (rbias.reshape(-1, 1), None),
        (perm, None), (perm.T, None), (tri, None),
    ]
    args = [a for a, _ in ins]
    specs = [s if s is not None else full(a.shape) for a, s in ins]
    out_shape = (
        jax.ShapeDtypeStruct((bsz, seq, d), F32),
        jax.ShapeDtypeStruct((bsz, seq, d), F32),
        jax.ShapeDtypeStruct((2, n_tok), I32),
        jax.ShapeDtypeStruct((2, n_tok), I32),
        jax.ShapeDtypeStruct((n_tok, LANES), F32),
        jax.ShapeDtypeStruct((n_experts, LANES), F32),
    )
    out_specs = (
        pl.BlockSpec((1, ts, d), lambda b, i: (b, i, 0)),
        pl.BlockSpec((1, ts, d), lambda b, i: (b, i, 0)),
        pl.BlockSpec((2, ts), lambda b, i: (0, b * ns + i)),
        pl.BlockSpec((2, ts), lambda b, i: (0, b * ns + i)),
        pl.BlockSpec((ts, LANES), lambda b, i: (b * ns + i, 0)),
        pl.BlockSpec((n_experts, LANES), lambda b, i: (0, 0)),
    )
    return pl.pallas_call(
        functools.partial(_mixer_body, alpha, n_experts, sub),
        grid=(bsz, ns),
        in_specs=specs,
        out_specs=out_specs,
        out_shape=out_shape,
        scratch_shapes=[
            pltpu.VMEM((nsub, CONV_HALO_VREGS * SUBLANES + sub, cw), F32),
            pltpu.VMEM((nsub, sub, cw), F32),
            pltpu.VMEM((nsub, POOL_HALO + sub, pw), F32),
            pltpu.VMEM((CONV_HALO_VREGS * SUBLANES, cw), F32),
            pltpu.VMEM((POOL_HALO, pw), F32),
            pltpu.VMEM((n_experts, LANES), F32),
        ],
        compiler_params=pltpu.CompilerParams(dimension_semantics=("arbitrary", "arbitrary"), vmem_limit_bytes=VMEM_LIMIT),
        name="mixer_router",
    )(*args)


def _dest_body(n_experts, pstart_ref, e_ref, rank_ref, dest_ref):
    e = e_ref[...]
    dest = rank_ref[...]
    for ex in range(n_experts):
        dest = dest + jnp.where(e == ex, pstart_ref[ex], 0)
    dest_ref[...] = dest


def _dest_rows(pstart, eidx, rank):
    n_tok = eidx.shape[1]
    tn = min(8192, n_tok)
    grid_spec = pltpu.PrefetchScalarGridSpec(
        num_scalar_prefetch=1,
        grid=(n_tok // tn,),
        in_specs=[pl.BlockSpec((2, tn), lambda i, ps: (0, i)), pl.BlockSpec((2, tn), lambda i, ps: (0, i))],
        out_specs=pl.BlockSpec((2, tn), lambda i, ps: (0, i)),
    )
    return pl.pallas_call(
        functools.partial(_dest_body, pstart.shape[0]),
        grid_spec=grid_spec,
        out_shape=jax.ShapeDtypeStruct((2, n_tok), I32),
        compiler_params=pltpu.CompilerParams(dimension_semantics=("arbitrary",)),
        name="dest_rows",
    )(pstart, eidx, rank)


def _sc_mesh():
    return plsc.VectorSubcoreMesh(core_axis_name="c", subcore_axis_name="s",
                                  num_cores=SC_CORES, num_subcores=SC_SUBCORES)


def _dispatch(u2, d0, d1, n_rows):
    n_tok, d = u2.shape
    workers = SC_CORES * SC_SUBCORES
    per_w = n_tok // workers
    ch = SC_ROWS_PER_STEP
    nch = per_w // ch

    @functools.partial(
        pl.kernel, mesh=_sc_mesh(),
        out_type=jax.ShapeDtypeStruct((n_rows, d), F32),
        scratch_types=[pltpu.VMEM((nch, ch), I32), pltpu.VMEM((nch, ch), I32), pltpu.VMEM((2, ch, d), F32),
                       pltpu.SemaphoreType.DMA((2,)), pltpu.SemaphoreType.DMA((2,))],
        name="sc_dispatch",
    )
    def run(u_hbm, d0_hbm, d1_hbm, xs_hbm, i0, i1, rows, rsem, ssem):
        wid = lax.axis_index("s") * SC_CORES + lax.axis_index("c")
        base = wid * per_w
        pltpu.sync_copy(d0_hbm.at[pl.ds(wid * nch, nch)], i0)
        pltpu.sync_copy(d1_hbm.at[pl.ds(wid * nch, nch)], i1)

        def read(j, b):
            return pltpu.make_async_copy(u_hbm.at[pl.ds(base + j * ch, ch)], rows.at[b], rsem.at[b])

        def scatter(idx, j, b):
            return pltpu.make_async_copy(rows.at[b], xs_hbm.at[idx.at[j]], ssem.at[b])

        read(0, 0).start()

        @pl.loop(0, nch, step=2)
        def _(j0):
            for b in range(2):
                j = j0 + b
                read(j, b).wait()
                scatter(i0, j, b).start()
                scatter(i1, j, b).start()

                @pl.when(j >= 1)
                def _():
                    scatter(i0, j - 1, 1 - b).wait()
                    scatter(i1, j - 1, 1 - b).wait()

                @pl.when(j + 1 < nch)
                def _():
                    read(j + 1, 1 - b).start()

        scatter(i0, nch - 1, 1).wait()
        scatter(i1, nch - 1, 1).wait()

    return run(u2, d0.reshape(n_tok // ch, ch), d1.reshape(n_tok // ch, ch))


def _gather_rows(ys, dd):
    n_out = dd.shape[0]
    d = ys.shape[1]
    workers = SC_CORES * SC_SUBCORES
    per_w = n_out // workers
    ch = SC_ROWS_PER_STEP
    nch = per_w // ch

    @functools.partial(
        pl.kernel, mesh=_sc_mesh(),
        out_type=jax.ShapeDtypeStruct((n_out, d), F32),
        scratch_types=[pltpu.VMEM((nch, ch), I32), pltpu.VMEM((2, ch, d), F32),
                       pltpu.SemaphoreType.DMA((2,)), pltpu.SemaphoreType.DMA((2,))],
        name="sc_gather",
    )
    def run(ys_hbm, dd_hbm, out_hbm, idx, rows, gsem, wsem):
        wid = lax.axis_index("s") * SC_CORES + lax.axis_index("c")
        base = wid * per_w
        pltpu.sync_copy(dd_hbm.at[pl.ds(wid * nch, nch)], idx)

        def gather(j, b):
            return pltpu.make_async_copy(ys_hbm.at[idx.at[j]], rows.at[b], gsem.at[b])

        def write(j, b):
            return pltpu.make_async_copy(rows.at[b], out_hbm.at[pl.ds(base + j * ch, ch)], wsem.at[b])

        gather(0, 0).start()

        @pl.loop(0, nch, step=2)
        def _(j0):
            for b in range(2):
                j = j0 + b
                gather(j, b).wait()
                write(j, b).start()

                @pl.when(j >= 1)
                def _():
                    write(j - 1, 1 - b).wait()

                @pl.when(j + 1 < nch)
                def _():
                    gather(j + 1, 1 - b).start()

        write(nch - 1, 1).wait()

    return run(ys, dd.reshape(n_out // ch, ch))


def _expert_body(blk_ref, nused_ref, xs_ref, wg_ref, wu_ref, wd_ref, ys_ref, wgb, wub, wdb):
    j = pl.program_id(0)
    used = j < nused_ref[0]
    new_expert = jnp.logical_or(j == 0, blk_ref[j] != blk_ref[jnp.maximum(j - 1, 0)])

    @pl.when(jnp.logical_and(used, new_expert))
    def _():
        wgb[...] = wg_ref[0, 0].astype(BF16)
        wub[...] = wu_ref[0, 0].astype(BF16)
        wdb[...] = wd_ref[0, 0].astype(BF16)

    @pl.when(used)
    def _():
        xb = xs_ref[...].astype(BF16)
        g = _dot(xb, wgb[...])
        up = _dot(xb, wub[...])
        hid = (g * jax.nn.sigmoid(g) * up).astype(BF16)
        ys_ref[...] = _dot(hid, wdb[...])

    @pl.when(j >= nused_ref[0])
    def _():
        ys_ref[...] = jnp.zeros(ys_ref.shape, F32)


def _experts(blk_e, nused, xs, wg, wu, wd, layer):
    n_rows, d = xs.shape
    f = wg.shape[3]
    tm = EXPERT_TILE
    grid_spec = pltpu.PrefetchScalarGridSpec(
        num_scalar_prefetch=2,
        grid=(n_rows // tm,),
        in_specs=[
            pl.BlockSpec((tm, d), lambda j, be, nu: (jnp.minimum(j, nu[0] - 1), 0)),
            pl.BlockSpec((1, 1, d, f), lambda j, be, nu: (layer, be[j], 0, 0)),
            pl.BlockSpec((1, 1, d, f), lambda j, be, nu: (layer, be[j], 0, 0)),
            pl.BlockSpec((1, 1, f, d), lambda j, be, nu: (layer, be[j], 0, 0)),
        ],
        out_specs=pl.BlockSpec((tm, d), lambda j, be, nu: (j, 0)),
        scratch_shapes=[pltpu.VMEM((d, f), BF16), pltpu.VMEM((d, f), BF16), pltpu.VMEM((f, d), BF16)],
    )
    return pl.pallas_call(
        _expert_body,
        grid_spec=grid_spec,
        out_shape=jax.ShapeDtypeStruct((n_rows, d), F32),
        compiler_params=pltpu.CompilerParams(dimension_semantics=("arbitrary",), vmem_limit_bytes=VMEM_LIMIT),
        name="expert_ffn",
    )(blk_e, nused, xs, wg, wu, wd)


def _combine_body(alpha, x1_ref, y0_ref, y1_ref, mod_ref, wm_ref, g_ref, b_ref, *rest):
    o_ref = rest[-1]
    wm = wm_ref[...]
    f = wm[:, 0:1] * y0_ref[...] + wm[:, 1:2] * y1_ref[...]
    g2 = mod_ref[0][5:6]
    o_ref[...] = _layer_norm_rows(alpha * x1_ref[...] + (1.0 + g2) * f, g_ref[...], b_ref[...])


def _combine(x1, yy, mod, wm, ln_g, ln_b, seq, alpha, mod_b0, out_rows, out_row0, out_prev):
    n_tok, d = x1.shape
    tn = min(COMBINE_TILE, seq)
    per_seq = seq // tn
    nblk = n_tok // tn
    blk0 = out_row0 // tn
    rows = lambda: pl.BlockSpec((tn, d), lambda i: (i, 0))
    in_specs = [rows(), rows(), pl.BlockSpec((tn, d), lambda i: (nblk + i, 0)),
                pl.BlockSpec((1, 6, d), lambda i: (i // per_seq + mod_b0, 0, 0)),
                pl.BlockSpec((tn, LANES), lambda i: (i, 0)),
                pl.BlockSpec((1, d), lambda i: (0, 0)),
                pl.BlockSpec((1, d), lambda i: (0, 0))]
    args = [x1, yy, yy, mod, wm, ln_g.reshape(1, d), ln_b.reshape(1, d)]
    aliases = {}
    if out_prev is not None:
        in_specs.append(pl.BlockSpec(memory_space=pl.ANY))
        args.append(out_prev)
        aliases = {len(args) - 1: 0}
    return pl.pallas_call(
        functools.partial(_combine_body, alpha),
        grid=(nblk,),
        in_specs=in_specs,
        out_specs=pl.BlockSpec((tn, d), lambda i: (blk0 + i, 0)),
        out_shape=jax.ShapeDtypeStruct((out_rows, d), F32),
        input_output_aliases=aliases,
        compiler_params=pltpu.CompilerParams(dimension_semantics=("arbitrary",), vmem_limit_bytes=VMEM_LIMIT),
        name="combine_ln",
    )(*args)


def kernel(x, c, w_ada, b_ada, w_in, b_in, conv_w, conv_b, conv_ln_g, conv_ln_b, pool_w, pool_scale, w_out, b_out, ln1_g, ln1_b, w_router, router_bias, w_gate, w_up, w_down, ln2_g, ln2_b):
    bsz, seq, d = x.shape
    depth = w_ada.shape[0]
    n_experts = w_router.shape[1]
    alpha = float((2 * depth) ** 0.25)
    tm = EXPERT_TILE
    n_chunks = BATCH_CHUNKS if bsz % BATCH_CHUNKS == 0 else 1
    bc = bsz // n_chunks
    n_tok = bc * seq
    n_rows = 2 * n_tok + n_experts * tm

    mod_all = _ada_mod(c, w_ada, b_ada).reshape(depth, bsz, 6, d)
    wr = jnp.pad(w_router.astype(BF16), ((0, 0), (0, LANES - n_experts)))

    chunks = [x] * n_chunks
    starts = [ci * bc for ci in range(n_chunks)]
    out = None
    for l in range(depth):
        lw = dict(w_in=w_in[l].astype(BF16), b_in=b_in[l], conv_w=conv_w[l], conv_b=conv_b[l],
                  conv_ln_g=conv_ln_g[l], conv_ln_b=conv_ln_b[l], pool_w=pool_w[l].astype(BF16),
                  pool_scale=pool_scale[l], w_out=w_out[l].astype(BF16), b_out=b_out[l],
                  ln1_g=ln1_g[l], ln1_b=ln1_b[l])
        mod = mod_all[l]
        last = l == depth - 1
        st = []
        for ci in range(n_chunks):
            x1, u2, eidx, rank, wm, cnt = _mixer(chunks[ci], mod, lw, wr, router_bias, alpha,
                                                 bc, starts[ci], ci * bc)
            counts = cnt[:, 0].astype(I32)
            tiles = (counts + tm - 1) // tm
            tile_end = jnp.cumsum(tiles)
            pstart = ((tile_end - tiles) * tm).astype(I32)
            nused = tile_end[-1:].astype(I32)
            blk_e = jnp.minimum(
                jnp.sum(tile_end[None, :] <= jnp.arange(n_rows // tm, dtype=I32)[:, None], axis=1), n_experts - 1
            ).astype(I32)
            dest = _dest_rows(pstart, eidx, rank)
            xs = _dispatch(u2.reshape(n_tok, d), dest[0], dest[1], n_rows)
            st.append((x1, wm, blk_e, nused, dest, xs))
        ys = [_experts(blk_e, nused, xs, w_gate, w_up, w_down, l) for (_, _, blk_e, nused, _, xs) in st]
        yy = [_gather_rows(ys[ci], st[ci][4].reshape(2 * n_tok)) for ci in range(n_chunks)]
        for ci in range(n_chunks):
            x1, wm = st[ci][0], st[ci][1]
            if last:
                out = _combine(x1.reshape(n_tok, d), yy[ci], mod, wm, ln2_g[l], ln2_b[l], seq, alpha,
                               ci * bc, bsz * seq, ci * n_tok, out)
            else:
                xc = _combine(x1.reshape(n_tok, d), yy[ci], mod, wm, ln2_g[l], ln2_b[l], seq, alpha,
                              ci * bc, n_tok, 0, None)
                chunks[ci] = xc.reshape(bc, seq, d)
                starts[ci] = 0
    return out.reshape(bsz, seq, d)
```

```python
import functools

import jax
import jax.numpy as jnp
from jax import lax
from jax.experimental import pallas as pl
from jax.experimental.pallas import tpu as pltpu
from jax.experimental.pallas import tpu_sc as plsc

F32 = jnp.float32
BF16 = jnp.bfloat16
I32 = jnp.int32
U32 = jnp.uint32

CONV_KERNEL = 31
POOL_WINDOWS = (2, 4, 8, 16)
N_EXPERT_GROUPS = 4
LN_EPS = 1e-5

SUBLANES = 8
LANES = 128

SEQ_TILE = 512
SEQ_SUB_TILE = 256
EXPERT_TILE = 512
COMBINE_TILE = 512
BATCH_CHUNKS = 2
CONV_HALO_VREGS = 32
POOL_HALO = 16
VMEM_LIMIT = 56 * 1024 * 1024
SC_CORES = 2
SC_SUBCORES = 16
SC_ROWS_PER_STEP = 64


def _split_bf16(a):
    hi = a.astype(BF16)
    lo = (a - hi.astype(F32)).astype(BF16)
    return hi, lo


def _dot(a, b):
    return jnp.dot(a, b, preferred_element_type=F32)


def _pack_halves(y):
    h = y.shape[1] // 2
    lo = lax.bitcast_convert_type(y[:, :h].astype(BF16).astype(F32), U32)
    hi = lax.bitcast_convert_type(y[:, h:].astype(BF16).astype(F32), U32)
    return (lo >> 16) | hi


def _unpack_halves(p):
    lo = lax.bitcast_convert_type(p << 16, F32)
    hi = lax.bitcast_convert_type(p & jnp.uint32(0xFFFF0000), F32)
    return lo, hi


def _ada_body(c_ref, w_ref, b_ref, o_ref):
    c = c_ref[...]
    ca = c * jax.nn.sigmoid(c)
    chi, clo = _split_bf16(ca)
    whi, wlo = _split_bf16(w_ref[0])
    o_ref[0] = _dot(chi, whi) + _dot(chi, wlo) + _dot(clo, whi) + b_ref[0]


def _ada_mod(c, w_ada, b_ada):
    n_layers, d, n6 = w_ada.shape
    bsz = c.shape[0]
    tn = n6 // 6
    return pl.pallas_call(
        _ada_body,
        grid=(n_layers, n6 // tn),
        in_specs=[
            pl.BlockSpec((bsz, d), lambda l, j: (0, 0)),
            pl.BlockSpec((1, d, tn), lambda l, j: (l, 0, j)),
            pl.BlockSpec((1, 1, tn), lambda l, j: (l, 0, j)),
        ],
        out_specs=pl.BlockSpec((1, bsz, tn), lambda l, j: (l, 0, j)),
        out_shape=jax.ShapeDtypeStruct((n_layers, bsz, n6), F32),
        compiler_params=pltpu.CompilerParams(dimension_semantics=("arbitrary", "arbitrary"), vmem_limit_bytes=VMEM_LIMIT),
        name="ada_mod",
    )(c, w_ada, b_ada.reshape(n_layers, 1, n6))


def _layer_norm_rows(z, g, b):
    mu = jnp.mean(z, axis=-1, keepdims=True)
    zc = z - mu
    var = jnp.mean(zc * zc, axis=-1, keepdims=True)
    return zc * lax.rsqrt(var + LN_EPS) * g + b


def _mixer_body(alpha, n_experts, sub,
                x_ref, mod_ref, win_ref, bin_ref, cw_ref, cb_ref, cg_ref, cbeta_ref, pw_ref, ps_ref,
                wout_ref, bout_ref, l1g_ref, l1b_ref, wr_ref, rb_ref, perm_ref, permt_ref, tri_ref,
                x1_ref, u2_ref, e_ref, rank_ref, wm_ref, cnt_ref,
                ebuf, cbuf, pbuf, vprev, phalo, carry):
    ts = x_ref.shape[1]
    cw = cw_ref.shape[1]
    pw = ps_ref.shape[1]
    gw = pw // len(POOL_WINDOWS)
    fine = sub // SUBLANES
    halo = CONV_HALO_VREGS
    b = pl.program_id(0)
    i = pl.program_id(1)

    @pl.when(i == 0)
    def _():
        vprev[...] = jnp.zeros(vprev.shape, F32)
        phalo[...] = jnp.zeros(phalo.shape, F32)

    @pl.when(jnp.logical_and(b == 0, i == 0))
    def _():
        carry[...] = jnp.zeros(carry.shape, F32)

    mod = mod_ref[0]
    sh1, sc1, g1 = mod[0:1], mod[1:2], mod[2:3]
    sh2, sc2 = mod[3:4], mod[4:5]

    for s in range(ts // sub):
        r0 = s * sub
        x = x_ref[0, r0:r0 + sub, :]
        u = (x * (1.0 + sc1) + sh1).astype(BF16)
        h = _dot(u, win_ref[...]) + bin_ref[...]
        a = h[:, :cw]
        gate = h[:, cw:2 * cw]
        p = h[:, 2 * cw:]

        v = (a * jax.nn.sigmoid(gate)).astype(BF16)
        vb = _dot(perm_ref[...], v)
        ebuf[s, halo * SUBLANES:, :] = vb
        sl = lax.broadcasted_iota(I32, (SUBLANES, cw), 0)
        for mm in range(halo):
            src = (fine - halo + mm) * SUBLANES
            cur = vb[src:src + SUBLANES, :]
            prev = vprev[mm * SUBLANES:(mm + 1) * SUBLANES, :]
            ebuf[s, mm * SUBLANES:(mm + 1) * SUBLANES, :] = pltpu.roll(
                jnp.where(sl == SUBLANES - 1, prev, cur), 1, 0)
        vprev[...] = vb[(fine - halo) * SUBLANES:, :]
        pbuf[s, 0:POOL_HALO, :] = phalo[...]
        pbuf[s, POOL_HALO:, :] = p
        phalo[...] = p[sub - POOL_HALO:, :]

    for s in range(ts // sub):
        r0 = s * sub
        x = x_ref[0, r0:r0 + sub, :]
        chunk = 8 * SUBLANES
        first = halo - (CONV_KERNEL - 1)
        for q0 in range(0, sub, chunk):
            for c0 in range(0, cw, LANES):
                acc = jnp.broadcast_to(cb_ref[:, c0:c0 + LANES], (chunk, LANES))
                for k in range(CONV_KERNEL):
                    off = q0 + (first + k) * SUBLANES
                    acc = acc + cw_ref[k:k + 1, c0:c0 + LANES] * ebuf[s, off:off + chunk, c0:c0 + LANES]
                cbuf[s, q0:q0 + chunk, c0:c0 + LANES] = acc
        yc = _layer_norm_rows(cbuf[s], cg_ref[...], cbeta_ref[...])
        ya_b = (yc * jax.nn.sigmoid(yc)).astype(BF16)
        ya = _dot(permt_ref[...], ya_b).astype(BF16)

        ext = pbuf[s]
        p = ext[POOL_HALO:, :]
        w_sum = ext + pltpu.roll(ext, 1, 0)
        parts = [w_sum[:, :gw]]
        shift = 2
        for _ in range(len(POOL_WINDOWS) - 1):
            w_sum = w_sum[:, gw:]
            w_sum = w_sum + pltpu.roll(w_sum, shift, 0)
            parts.append(w_sum[:, :gw])
            shift *= 2
        wsum = jnp.concatenate(parts, axis=1)[POOL_HALO:, :]
        pos = (lax.broadcasted_iota(I32, (sub, gw), 0) + (i * ts + r0 + 1)).astype(F32)
        cnt = jnp.concatenate([jnp.minimum(pos, float(wlen)) for wlen in POOL_WINDOWS], axis=1)
        dpool = (wsum / cnt - p).astype(BF16)
        yb_parts = [_dot(dpool[:, gi * gw:(gi + 1) * gw], pw_ref[gi]) for gi in range(len(POOL_WINDOWS))]
        yb = (jnp.concatenate(yb_parts, axis=1) * ps_ref[...]).astype(BF16)

        mix = _dot(ya, wout_ref[0:cw, :]) + _dot(yb, wout_ref[cw:, :]) + bout_ref[...]
        x1 = _layer_norm_rows(alpha * x + (1.0 + g1) * mix, l1g_ref[...], l1b_ref[...])
        x1_ref[0, r0:r0 + sub, :] = x1
        u2 = x1 * (1.0 + sc2) + sh2
        u2_ref[0, r0:r0 + sub, :] = _pack_halves(u2)

        logits = _dot(u2.astype(BF16), wr_ref[...]).T[0:n_experts, :]
        scores = jax.nn.sigmoid(logits)
        sel = scores + rb_ref[...]
        epg = n_experts // N_EXPERT_GROUPS
        io = lax.broadcasted_iota(I32, (epg, sub), 0).astype(F32)
        neg = jnp.float32(-jnp.inf)
        best = None
        for g in range(N_EXPERT_GROUPS):
            sg = sel[g * epg:(g + 1) * epg, :]
            scg = scores[g * epg:(g + 1) * epg, :]
            m1 = jnp.max(sg, axis=0, keepdims=True)
            i1 = jnp.min(jnp.where(sg == m1, io, float(epg)), axis=0, keepdims=True)
            rest = jnp.where(io == i1, neg, sg)
            m2 = jnp.max(rest, axis=0, keepdims=True)
            i2 = jnp.min(jnp.where(jnp.logical_and(rest == m2, io != i1), io, float(epg)), axis=0, keepdims=True)
            s1 = jnp.sum(jnp.where(io == i1, scg, 0.0), axis=0, keepdims=True)
            s2 = jnp.sum(jnp.where(io == i2, scg, 0.0), axis=0, keepdims=True)
            gs = m1 + m2
            cand = (gs, i1 + float(g * epg), i2 + float(g * epg), s1, s2)
            if best is None:
                best = cand
            else:
                upd = gs > best[0]
                best = tuple(jnp.where(upd, cn, bs) for cn, bs in zip(cand, best))
        _, e1, e2, s1, s2 = best
        denom = s1 + s2
        w1 = s1 / denom
        w2 = s2 / denom

        ioe = lax.broadcasted_iota(I32, (n_experts, sub), 0).astype(F32)
        hit1 = ioe == e1
        hit2 = ioe == e2
        onehot = jnp.where(jnp.logical_or(hit1, hit2), 1.0, 0.0)
        before = _dot(onehot.astype(BF16), tri_ref[...]) + carry[:, 0:1]
        r1 = jnp.sum(jnp.where(hit1, before, 0.0), axis=0, keepdims=True)
        r2 = jnp.sum(jnp.where(hit2, before, 0.0), axis=0, keepdims=True)
        carry[...] = carry[...] + jnp.sum(onehot, axis=1, keepdims=True)

        e_ref[0:1, r0:r0 + sub] = e1.astype(I32)
        e_ref[1:2, r0:r0 + sub] = e2.astype(I32)
        rank_ref[0:1, r0:r0 + sub] = r1.astype(I32)
        rank_ref[1:2, r0:r0 + sub] = r2.astype(I32)
        rowi = lax.broadcasted_iota(I32, (LANES, sub), 0)
        wpad = jnp.where(rowi == 0, w1, jnp.where(rowi == 1, w2, 0.0))
        wm_ref[r0:r0 + sub, :] = wpad.T

    cnt_ref[...] = carry[...]


def _row_perm(n):
    r = jnp.arange(n)
    t = (n // SUBLANES) * (r % SUBLANES) + r // SUBLANES
    return (t[:, None] == jnp.arange(n)[None, :]).astype(BF16)


def _mixer(x, mod, lw, wr, rbias, alpha, bsz, x_b0, mod_b0):
    _, seq, d = x.shape
    ts = min(SEQ_TILE, seq)
    sub = min(SEQ_SUB_TILE, ts)
    nsub = ts // sub
    ns = seq // ts
    n_tok = bsz * seq
    n_experts = rbias.shape[0]
    cw = lw["conv_w"].shape[1]
    pw = lw["pool_scale"].shape[0]
    perm = _row_perm(sub)
    tri = (jnp.arange(sub)[:, None] < jnp.arange(sub)[None, :]).astype(BF16)
    row = lambda a: a.reshape(1, -1)
    full = lambda shape: pl.BlockSpec(shape, lambda b, i: (0,) * len(shape))
    ins = [
        (x, pl.BlockSpec((1, ts, d), lambda b, i: (b + x_b0, i, 0))),
        (mod, pl.BlockSpec((1, 6, d), lambda b, i: (b + mod_b0, 0, 0))),
        (lw["w_in"], None), (row(lw["b_in"]), None),
        (jnp.pad(lw["conv_w"], ((0, 1), (0, 0))), None), (row(lw["conv_b"]), None),
        (row(lw["conv_ln_g"]), None), (row(lw["conv_ln_b"]), None),
        (lw["pool_w"], None), (row(lw["pool_scale"]), None),
        (lw["w_out"], None), (row(lw["b_out"]), None),
        (row(lw["ln1_g"]), None), (row(lw["ln1_b"]), None),
        (wr, None), (rbias.reshape(-1, 1), None),
        (perm, None), (perm.T, None), (tri, None),
    ]
    args = [a for a, _ in ins]
    specs = [s if s is not None else full(a.shape) for a, s in ins]
    out_shape = (
        jax.ShapeDtypeStruct((bsz, seq, d), F32),
        jax.ShapeDtypeStruct((bsz, seq, d // 2), U32),
        jax.ShapeDtypeStruct((2, n_tok), I32),
        jax.ShapeDtypeStruct((2, n_tok), I32),
        jax.ShapeDtypeStruct((n_tok, LANES), F32),
        jax.ShapeDtypeStruct((n_experts, LANES), F32),
    )
    out_specs = (
        pl.BlockSpec((1, ts, d), lambda b, i: (b, i, 0)),
        pl.BlockSpec((1, ts, d // 2), lambda b, i: (b, i, 0)),
        pl.BlockSpec((2, ts), lambda b, i: (0, b * ns + i)),
        pl.BlockSpec((2, ts), lambda b, i: (0, b * ns + i)),
        pl.BlockSpec((ts, LANES), lambda b, i: (b * ns + i, 0)),
        pl.BlockSpec((n_experts, LANES), lambda b, i: (0, 0)),
    )
    return pl.pallas_call(
        functools.partial(_mixer_body, alpha, n_experts, sub),
        grid=(bsz, ns),
        in_specs=specs,
        out_specs=out_specs,
        out_shape=out_shape,
        scratch_shapes=[
            pltpu.VMEM((nsub, CONV_HALO_VREGS * SUBLANES + sub, cw), F32),
            pltpu.VMEM((nsub, sub, cw), F32),
            pltpu.VMEM((nsub, POOL_HALO + sub, pw), F32),
            pltpu.VMEM((CONV_HALO_VREGS * SUBLANES, cw), F32),
            pltpu.VMEM((POOL_HALO, pw), F32),
            pltpu.VMEM((n_experts, LANES), F32),
        ],
        compiler_params=pltpu.CompilerParams(dimension_semantics=("arbitrary", "arbitrary"), vmem_limit_bytes=VMEM_LIMIT),
        name="mixer_router",
    )(*args)


def _dest_body(n_experts, pstart_ref, e_ref, rank_ref, dest_ref):
    e = e_ref[...]
    dest = rank_ref[...]
    for ex in range(n_experts):
        dest = dest + jnp.where(e == ex, pstart_ref[ex], 0)
    dest_ref[...] = dest


def _dest_rows(pstart, eidx, rank):
    n_tok = eidx.shape[1]
    tn = min(8192, n_tok)
    grid_spec = pltpu.PrefetchScalarGridSpec(
        num_scalar_prefetch=1,
        grid=(n_tok // tn,),
        in_specs=[pl.BlockSpec((2, tn), lambda i, ps: (0, i)), pl.BlockSpec((2, tn), lambda i, ps: (0, i))],
        out_specs=pl.BlockSpec((2, tn), lambda i, ps: (0, i)),
    )
    return pl.pallas_call(
        functools.partial(_dest_body, pstart.shape[0]),
        grid_spec=grid_spec,
        out_shape=jax.ShapeDtypeStruct((2, n_tok), I32),
        compiler_params=pltpu.CompilerParams(dimension_semantics=("arbitrary",)),
        name="dest_rows",
    )(pstart, eidx, rank)


def _sc_mesh():
    return plsc.VectorSubcoreMesh(core_axis_name="c", subcore_axis_name="s",
                                  num_cores=SC_CORES, num_subcores=SC_SUBCORES)


def _dispatch(u2, d0, d1, n_rows):
    n_tok, d = u2.shape
    workers = SC_CORES * SC_SUBCORES
    per_w = n_tok // workers
    ch = SC_ROWS_PER_STEP
    nch = per_w // ch

    @functools.partial(
        pl.kernel, mesh=_sc_mesh(),
        out_type=jax.ShapeDtypeStruct((n_rows, d), u2.dtype),
        scratch_types=[pltpu.VMEM((nch, ch), I32), pltpu.VMEM((nch, ch), I32), pltpu.VMEM((2, ch, d), u2.dtype),
                       pltpu.SemaphoreType.DMA((2,)), pltpu.SemaphoreType.DMA((2,))],
        name="sc_dispatch",
    )
    def run(u_hbm, d0_hbm, d1_hbm, xs_hbm, i0, i1, rows, rsem, ssem):
        wid = lax.axis_index("s") * SC_CORES + lax.axis_index("c")
        base = wid * per_w
        pltpu.sync_copy(d0_hbm.at[pl.ds(wid * nch, nch)], i0)
        pltpu.sync_copy(d1_hbm.at[pl.ds(wid * nch, nch)], i1)

        def read(j, b):
            return pltpu.make_async_copy(u_hbm.at[pl.ds(base + j * ch, ch)], rows.at[b], rsem.at[b])

        def scatter(idx, j, b):
            return pltpu.make_async_copy(rows.at[b], xs_hbm.at[idx.at[j]], ssem.at[b])

        read(0, 0).start()

        @pl.loop(0, nch, step=2)
        def _(j0):
            for b in range(2):
                j = j0 + b
                read(j, b).wait()
                scatter(i0, j, b).start()
                scatter(i1, j, b).start()

                @pl.when(j >= 1)
                def _():
                    scatter(i0, j - 1, 1 - b).wait()
                    scatter(i1, j - 1, 1 - b).wait()

                @pl.when(j + 1 < nch)
                def _():
                    read(j + 1, 1 - b).start()

        scatter(i0, nch - 1, 1).wait()
        scatter(i1, nch - 1, 1).wait()

    return run(u2, d0.reshape(n_tok // ch, ch), d1.reshape(n_tok // ch, ch))


def _gather_rows(ys, dd):
    n_out = dd.shape[0]
    d = ys.shape[1]
    workers = SC_CORES * SC_SUBCORES
    per_w = n_out // workers
    ch = SC_ROWS_PER_STEP
    nch = per_w // ch

    @functools.partial(
        pl.kernel, mesh=_sc_mesh(),
        out_type=jax.ShapeDtypeStruct((n_out, d), ys.dtype),
        scratch_types=[pltpu.VMEM((nch, ch), I32), pltpu.VMEM((2, ch, d), ys.dtype),
                       pltpu.SemaphoreType.DMA((2,)), pltpu.SemaphoreType.DMA((2,))],
        name="sc_gather",
    )
    def run(ys_hbm, dd_hbm, out_hbm, idx, rows, gsem, wsem):
        wid = lax.axis_index("s") * SC_CORES + lax.axis_index("c")
        base = wid * per_w
        pltpu.sync_copy(dd_hbm.at[pl.ds(wid * nch, nch)], idx)

        def gather(j, b):
            return pltpu.make_async_copy(ys_hbm.at[idx.at[j]], rows.at[b], gsem.at[b])

        def write(j, b):
            return pltpu.make_async_copy(rows.at[b], out_hbm.at[pl.ds(base + j * ch, ch)], wsem.at[b])

        gather(0, 0).start()

        @pl.loop(0, nch, step=2)
        def _(j0):
            for b in range(2):
                j = j0 + b
                gather(j, b).wait()
                write(j, b).start()

                @pl.when(j >= 1)
                def _():
                    write(j - 1, 1 - b).wait()

                @pl.when(j + 1 < nch)
                def _():
                    gather(j + 1, 1 - b).start()

        write(nch - 1, 1).wait()

    return run(ys, dd.reshape(n_out // ch, ch))


def _expert_body(blk_ref, nused_ref, xs_ref, wg_ref, wu_ref, wd_ref, ys_ref, wgb, wub, wdb):
    j = pl.program_id(0)
    used = j < nused_ref[0]
    new_expert = jnp.logical_or(j == 0, blk_ref[j] != blk_ref[jnp.maximum(j - 1, 0)])

    @pl.when(jnp.logical_and(used, new_expert))
    def _():
        wgb[...] = wg_ref[0, 0].astype(BF16)
        wub[...] = wu_ref[0, 0].astype(BF16)
        wdb[...] = wd_ref[0, 0].astype(BF16)

    @pl.when(used)
    def _():
        lo, hi = _unpack_halves(xs_ref[...])
        lo = lo.astype(BF16)
        hi = hi.astype(BF16)
        half = lo.shape[1]
        g = _dot(lo, wgb[0:half, :]) + _dot(hi, wgb[half:, :])
        up = _dot(lo, wub[0:half, :]) + _dot(hi, wub[half:, :])
        hid = (g * jax.nn.sigmoid(g) * up).astype(BF16)
        ys_ref[...] = _pack_halves(_dot(hid, wdb[...]))

    @pl.when(j >= nused_ref[0])
    def _():
        ys_ref[...] = jnp.zeros(ys_ref.shape, U32)


def _experts(blk_e, nused, xs, wg, wu, wd, layer):
    n_rows = xs.shape[0]
    d = wg.shape[2]
    f = wg.shape[3]
    tm = EXPERT_TILE
    grid_spec = pltpu.PrefetchScalarGridSpec(
        num_scalar_prefetch=2,
        grid=(n_rows // tm,),
        in_specs=[
            pl.BlockSpec((tm, d // 2), lambda j, be, nu: (jnp.minimum(j, nu[0] - 1), 0)),
            pl.BlockSpec((1, 1, d, f), lambda j, be, nu: (layer, be[j], 0, 0)),
            pl.BlockSpec((1, 1, d, f), lambda j, be, nu: (layer, be[j], 0, 0)),
            pl.BlockSpec((1, 1, f, d), lambda j, be, nu: (layer, be[j], 0, 0)),
        ],
        out_specs=pl.BlockSpec((tm, d // 2), lambda j, be, nu: (j, 0)),
        scratch_shapes=[pltpu.VMEM((d, f), BF16), pltpu.VMEM((d, f), BF16), pltpu.VMEM((f, d), BF16)],
    )
    return pl.pallas_call(
        _expert_body,
        grid_spec=grid_spec,
        out_shape=jax.ShapeDtypeStruct((n_rows, d // 2), U32),
        compiler_params=pltpu.CompilerParams(dimension_semantics=("arbitrary",), vmem_limit_bytes=VMEM_LIMIT),
        name="expert_ffn",
    )(blk_e, nused, xs, wg, wu, wd)


def _combine_body(alpha, x1_ref, y0_ref, y1_ref, mod_ref, wm_ref, g_ref, b_ref, *rest):
    o_ref = rest[-1]
    wm = wm_ref[...]
    lo0, hi0 = _unpack_halves(y0_ref[...])
    lo1, hi1 = _unpack_halves(y1_ref[...])
    w0, w1 = wm[:, 0:1], wm[:, 1:2]
    f = jnp.concatenate([w0 * lo0 + w1 * lo1, w0 * hi0 + w1 * hi1], axis=1)
    g2 = mod_ref[0][5:6]
    o_ref[...] = _layer_norm_rows(alpha * x1_ref[...] + (1.0 + g2) * f, g_ref[...], b_ref[...])


def _combine(x1, yy, mod, wm, ln_g, ln_b, seq, alpha, mod_b0, out_rows, out_row0, out_prev):
    n_tok, d = x1.shape
    tn = min(COMBINE_TILE, seq)
    per_seq = seq // tn
    nblk = n_tok // tn
    blk0 = out_row0 // tn
    rows = lambda: pl.BlockSpec((tn, d), lambda i: (i, 0))
    in_specs = [rows(), pl.BlockSpec((tn, d // 2), lambda i: (i, 0)),
                pl.BlockSpec((tn, d // 2), lambda i: (nblk + i, 0)),
                pl.BlockSpec((1, 6, d), lambda i: (i // per_seq + mod_b0, 0, 0)),
                pl.BlockSpec((tn, LANES), lambda i: (i, 0)),
                pl.BlockSpec((1, d), lambda i: (0, 0)),
                pl.BlockSpec((1, d), lambda i: (0, 0))]
    args = [x1, yy, yy, mod, wm, ln_g.reshape(1, d), ln_b.reshape(1, d)]
    aliases = {}
    if out_prev is not None:
        in_specs.append(pl.BlockSpec(memory_space=pl.ANY))
        args.append(out_prev)
        aliases = {len(args) - 1: 0}
    return pl.pallas_call(
        functools.partial(_combine_body, alpha),
        grid=(nblk,),
        in_specs=in_specs,
        out_specs=pl.BlockSpec((tn, d), lambda i: (blk0 + i, 0)),
        out_shape=jax.ShapeDtypeStruct((out_rows, d), F32),
        input_output_aliases=aliases,
        compiler_params=pltpu.CompilerParams(dimension_semantics=("arbitrary",), vmem_limit_bytes=VMEM_LIMIT),
        name="combine_ln",
    )(*args)


def kernel(x, c, w_ada, b_ada, w_in, b_in, conv_w, conv_b, conv_ln_g, conv_ln_b, pool_w, pool_scale, w_out, b_out, ln1_g, ln1_b, w_router, router_bias, w_gate, w_up, w_down, ln2_g, ln2_b):
    bsz, seq, d = x.shape
    depth = w_ada.shape[0]
    n_experts = w_router.shape[1]
    alpha = float((2 * depth) ** 0.25)
    tm = EXPERT_TILE
    n_chunks = BATCH_CHUNKS if bsz % BATCH_CHUNKS == 0 else 1
    bc = bsz // n_chunks
    n_tok = bc * seq
    n_rows = 2 * n_tok + n_experts * tm

    mod_all = _ada_mod(c, w_ada, b_ada).reshape(depth, bsz, 6, d)
    wr = jnp.pad(w_router.astype(BF16), ((0, 0), (0, LANES - n_experts)))

    chunks = [x] * n_chunks
    starts = [ci * bc for ci in range(n_chunks)]
    out = None
    for l in range(depth):
        lw = dict(w_in=w_in[l].astype(BF16), b_in=b_in[l], conv_w=conv_w[l], conv_b=conv_b[l],
                  conv_ln_g=conv_ln_g[l], conv_ln_b=conv_ln_b[l], pool_w=pool_w[l].astype(BF16),
                  pool_scale=pool_scale[l], w_out=w_out[l].astype(BF16), b_out=b_out[l],
                  ln1_g=ln1_g[l], ln1_b=ln1_b[l])
        mod = mod_all[l]
        last = l == depth - 1
        st = []
        for ci in range(n_chunks):
            x1, u2, eidx, rank, wm, cnt = _mixer(chunks[ci], mod, lw, wr, router_bias, alpha,
                                                 bc, starts[ci], ci * bc)
            counts = cnt[:, 0].astype(I32)
            tiles = (counts + tm - 1) // tm
            tile_end = jnp.cumsum(tiles)
            pstart = ((tile_end - tiles) * tm).astype(I32)
            nused = tile_end[-1:].astype(I32)
            blk_e = jnp.minimum(
                jnp.sum(tile_end[None, :] <= jnp.arange(n_rows // tm, dtype=I32)[:, None], axis=1), n_experts - 1
            ).astype(I32)
            dest = _dest_rows(pstart, eidx, rank)
            xs = _dispatch(u2.reshape(n_tok, d // 2), dest[0], dest[1], n_rows)
            st.append((x1, wm, blk_e, nused, dest, xs))
        ys = [_experts(blk_e, nused, xs, w_gate, w_up, w_down, l) for (_, _, blk_e, nused, _, xs) in st]
        yy = [_gather_rows(ys[ci], st[ci][4].reshape(2 * n_tok)) for ci in range(n_chunks)]
        for ci in range(n_chunks):
            x1, wm = st[ci][0], st[ci][1]
            if last:
                out = _combine(x1.reshape(n_tok, d), yy[ci], mod, wm, ln2_g[l], ln2_b[l], seq, alpha,
                               ci * bc, bsz * seq, ci * n_tok, out)
            else:
                xc = _combine(x1.reshape(n_tok, d), yy[ci], mod, wm, ln2_g[l], ln2_b[l], seq, alpha,
                              ci * bc, n_tok, 0, None)
                chunks[ci] = xc.reshape(bc, seq, d)
                starts[ci] = 0
    return out.reshape(bsz, seq, d)
```

```python
import functools

import jax
import jax.numpy as jnp
from jax import lax
from jax.experimental import pallas as pl
from jax.experimental.pallas import tpu as pltpu
from jax.experimental.pallas import tpu_sc as plsc

F32 = jnp.float32
BF16 = jnp.bfloat16
I32 = jnp.int32
U32 = jnp.uint32

CONV_KERNEL = 31
POOL_WINDOWS = (2, 4, 8, 16)
N_EXPERT_GROUPS = 4
LN_EPS = 1e-5

SUBLANES = 8
LANES = 128

SEQ_TILE = 512
SEQ_SUB_TILE = 256
EXPERT_TILE = 512
EXPERT_F_SPLIT = 2
COMBINE_TILE = 512
BATCH_CHUNKS = 2
CONV_HALO_VREGS = 32
POOL_HALO = 16
VMEM_LIMIT = 56 * 1024 * 1024
SC_CORES = 2
SC_SUBCORES = 16
SC_ROWS_PER_STEP = 64


def _split_bf16(a):
    hi = a.astype(BF16)
    lo = (a - hi.astype(F32)).astype(BF16)
    return hi, lo


def _dot(a, b):
    return jnp.dot(a, b, preferred_element_type=F32)


def _pack_halves(y):
    h = y.shape[1] // 2
    lo = lax.bitcast_convert_type(y[:, :h].astype(BF16).astype(F32), U32)
    hi = lax.bitcast_convert_type(y[:, h:].astype(BF16).astype(F32), U32)
    return (lo >> 16) | hi


def _unpack_halves(p):
    lo = lax.bitcast_convert_type(p << 16, F32)
    hi = lax.bitcast_convert_type(p & jnp.uint32(0xFFFF0000), F32)
    return lo, hi


def _ada_body(c_ref, w_ref, b_ref, o_ref):
    c = c_ref[...]
    ca = c * jax.nn.sigmoid(c)
    chi, clo = _split_bf16(ca)
    whi, wlo = _split_bf16(w_ref[0])
    o_ref[0] = _dot(chi, whi) + _dot(chi, wlo) + _dot(clo, whi) + b_ref[0]


def _ada_mod(c, w_ada, b_ada):
    n_layers, d, n6 = w_ada.shape
    bsz = c.shape[0]
    tn = n6 // 6
    return pl.pallas_call(
        _ada_body,
        grid=(n_layers, n6 // tn),
        in_specs=[
            pl.BlockSpec((bsz, d), lambda l, j: (0, 0)),
            pl.BlockSpec((1, d, tn), lambda l, j: (l, 0, j)),
            pl.BlockSpec((1, 1, tn), lambda l, j: (l, 0, j)),
        ],
        out_specs=pl.BlockSpec((1, bsz, tn), lambda l, j: (l, 0, j)),
        out_shape=jax.ShapeDtypeStruct((n_layers, bsz, n6), F32),
        compiler_params=pltpu.CompilerParams(dimension_semantics=("arbitrary", "arbitrary"), vmem_limit_bytes=VMEM_LIMIT),
        name="ada_mod",
    )(c, w_ada, b_ada.reshape(n_layers, 1, n6))


def _layer_norm_rows(z, g, b):
    mu = jnp.mean(z, axis=-1, keepdims=True)
    zc = z - mu
    var = jnp.mean(zc * zc, axis=-1, keepdims=True)
    return zc * lax.rsqrt(var + LN_EPS) * g + b


def _mixer_body(alpha, n_experts, sub,
                x_ref, mod_ref, win_ref, bin_ref, cw_ref, cb_ref, cg_ref, cbeta_ref, pw_ref, ps_ref,
                wout_ref, bout_ref, l1g_ref, l1b_ref, wr_ref, rb_ref, perm_ref, permt_ref, tri_ref,
                x1_ref, u2_ref, e_ref, rank_ref, wm_ref, cnt_ref,
                ebuf, cbuf, pbuf, vprev, phalo, carry):
    ts = x_ref.shape[1]
    cw = cg_ref.shape[1]
    pw = ps_ref.shape[1]
    gw = pw // len(POOL_WINDOWS)
    fine = sub // SUBLANES
    halo = CONV_HALO_VREGS
    b = pl.program_id(0)
    i = pl.program_id(1)

    @pl.when(i == 0)
    def _():
        vprev[...] = jnp.zeros(vprev.shape, F32)
        phalo[...] = jnp.zeros(phalo.shape, F32)

    @pl.when(jnp.logical_and(b == 0, i == 0))
    def _():
        carry[...] = jnp.zeros(carry.shape, F32)

    mod = mod_ref[0]
    sh1, sc1, g1 = mod[0:1], mod[1:2], mod[2:3]
    sh2, sc2 = mod[3:4], mod[4:5]

    for s in range(ts // sub):
        r0 = s * sub
        x = x_ref[0, r0:r0 + sub, :]
        u = (x * (1.0 + sc1) + sh1).astype(BF16)
        h = _dot(u, win_ref[...]) + bin_ref[...]
        a = h[:, :cw]
        gate = h[:, cw:2 * cw]
        p = h[:, 2 * cw:]

        v = (a * jax.nn.sigmoid(gate)).astype(BF16)
        vb = _dot(perm_ref[...], v)
        sl = lax.broadcasted_iota(I32, (SUBLANES, LANES), 0)
        for cb in range(cw // LANES):
            vcol = vb[:, cb * LANES:(cb + 1) * LANES]
            ebuf[s, cb, halo * SUBLANES:, :] = vcol
            for mm in range(halo):
                src = (fine - halo + mm) * SUBLANES
                cur = vcol[src:src + SUBLANES, :]
                prev = vprev[mm * SUBLANES:(mm + 1) * SUBLANES, cb * LANES:(cb + 1) * LANES]
                ebuf[s, cb, mm * SUBLANES:(mm + 1) * SUBLANES, :] = pltpu.roll(
                    jnp.where(sl == SUBLANES - 1, prev, cur), 1, 0)
        vprev[...] = vb[(fine - halo) * SUBLANES:, :]
        pbuf[s, 0:POOL_HALO, :] = phalo[...]
        pbuf[s, POOL_HALO:, :] = p
        phalo[...] = p[sub - POOL_HALO:, :]

    for s in range(ts // sub):
        r0 = s * sub
        x = x_ref[0, r0:r0 + sub, :]
        chunk = 8 * SUBLANES
        first = halo - (CONV_KERNEL - 1)

        def conv_cols(cb, carry_, s=s):
            def conv_rows(qi, carry__):
                q0 = pl.multiple_of(qi * chunk, chunk)
                acc = jnp.broadcast_to(cb_ref[cb], (chunk, LANES))
                for k in range(CONV_KERNEL):
                    off = q0 + (first + k) * SUBLANES
                    acc = acc + cw_ref[cb, k:k + 1, :] * ebuf[s, cb, pl.ds(off, chunk), :]
                cbuf[s, cb, pl.ds(q0, chunk), :] = acc
                return carry__

            return lax.fori_loop(0, sub // chunk, conv_rows, carry_)

        lax.fori_loop(0, cw // LANES, conv_cols, 0)
        yconv = jnp.concatenate([cbuf[s, cb] for cb in range(cw // LANES)], axis=1)
        yc = _layer_norm_rows(yconv, cg_ref[...], cbeta_ref[...])
        ya_b = (yc * jax.nn.sigmoid(yc)).astype(BF16)
        ya = _dot(permt_ref[...], ya_b).astype(BF16)

        ext = pbuf[s]
        p = ext[POOL_HALO:, :]
        w_sum = ext + pltpu.roll(ext, 1, 0)
        parts = [w_sum[:, :gw]]
        shift = 2
        for _ in range(len(POOL_WINDOWS) - 1):
            w_sum = w_sum[:, gw:]
            w_sum = w_sum + pltpu.roll(w_sum, shift, 0)
            parts.append(w_sum[:, :gw])
            shift *= 2
        wsum = jnp.concatenate(parts, axis=1)[POOL_HALO:, :]
        pos = (lax.broadcasted_iota(I32, (sub, gw), 0) + (i * ts + r0 + 1)).astype(F32)
        cnt = jnp.concatenate([jnp.minimum(pos, float(wlen)) for wlen in POOL_WINDOWS], axis=1)
        dpool = (wsum / cnt - p).astype(BF16)
        yb_parts = [_dot(dpool[:, gi * gw:(gi + 1) * gw], pw_ref[gi]) for gi in range(len(POOL_WINDOWS))]
        yb = (jnp.concatenate(yb_parts, axis=1) * ps_ref[...]).astype(BF16)

        mix = _dot(ya, wout_ref[0:cw, :]) + _dot(yb, wout_ref[cw:, :]) + bout_ref[...]
        x1 = _layer_norm_rows(alpha * x + (1.0 + g1) * mix, l1g_ref[...], l1b_ref[...])
        x1_ref[0, r0:r0 + sub, :] = x1
        u2 = x1 * (1.0 + sc2) + sh2
        u2_ref[0, r0:r0 + sub, :] = _pack_halves(u2)

        logits = _dot(u2.astype(BF16), wr_ref[...]).T[0:n_experts, :]
        scores = jax.nn.sigmoid(logits)
        sel = scores + rb_ref[...]
        epg = n_experts // N_EXPERT_GROUPS
        io = lax.broadcasted_iota(I32, (epg, sub), 0).astype(F32)
        neg = jnp.float32(-jnp.inf)
        best = None
        for g in range(N_EXPERT_GROUPS):
            sg = sel[g * epg:(g + 1) * epg, :]
            scg = scores[g * epg:(g + 1) * epg, :]
            m1 = jnp.max(sg, axis=0, keepdims=True)
            i1 = jnp.min(jnp.where(sg == m1, io, float(epg)), axis=0, keepdims=True)
            rest = jnp.where(io == i1, neg, sg)
            m2 = jnp.max(rest, axis=0, keepdims=True)
            i2 = jnp.min(jnp.where(jnp.logical_and(rest == m2, io != i1), io, float(epg)), axis=0, keepdims=True)
            s1 = jnp.sum(jnp.where(io == i1, scg, 0.0), axis=0, keepdims=True)
            s2 = jnp.sum(jnp.where(io == i2, scg, 0.0), axis=0, keepdims=True)
            gs = m1 + m2
            cand = (gs, i1 + float(g * epg), i2 + float(g * epg), s1, s2)
            if best is None:
                best = cand
            else:
                upd = gs > best[0]
                best = tuple(jnp.where(upd, cn, bs) for cn, bs in zip(cand, best))
        _, e1, e2, s1, s2 = best
        denom = s1 + s2
        w1 = s1 / denom
        w2 = s2 / denom

        ioe = lax.broadcasted_iota(I32, (n_experts, sub), 0).astype(F32)
        hit1 = ioe == e1
        hit2 = ioe == e2
        onehot = jnp.where(jnp.logical_or(hit1, hit2), 1.0, 0.0)
        before = _dot(onehot.astype(BF16), tri_ref[...]) + carry[:, 0:1]
        r1 = jnp.sum(jnp.where(hit1, before, 0.0), axis=0, keepdims=True)
        r2 = jnp.sum(jnp.where(hit2, before, 0.0), axis=0, keepdims=True)
        carry[...] = carry[...] + jnp.sum(onehot, axis=1, keepdims=True)

        e_ref[0:1, r0:r0 + sub] = e1.astype(I32)
        e_ref[1:2, r0:r0 + sub] = e2.astype(I32)
        rank_ref[0:1, r0:r0 + sub] = r1.astype(I32)
        rank_ref[1:2, r0:r0 + sub] = r2.astype(I32)
        rowi = lax.broadcasted_iota(I32, (LANES, sub), 0)
        wpad = jnp.where(rowi == 0, w1, jnp.where(rowi == 1, w2, 0.0))
        wm_ref[r0:r0 + sub, :] = wpad.T

    cnt_ref[...] = carry[...]


def _row_perm(n):
    r = jnp.arange(n)
    t = (n // SUBLANES) * (r % SUBLANES) + r // SUBLANES
    return (t[:, None] == jnp.arange(n)[None, :]).astype(BF16)


def _mixer(x, mod, lw, wr, rbias, alpha, bsz, x_b0, mod_b0):
    _, seq, d = x.shape
    ts = min(SEQ_TILE, seq)
    sub = min(SEQ_SUB_TILE, ts)
    nsub = ts // sub
    ns = seq // ts
    n_tok = bsz * seq
    n_experts = rbias.shape[0]
    cw = lw["conv_w"].shape[1]
    pw = lw["pool_scale"].shape[0]
    perm = _row_perm(sub)
    tri = (jnp.arange(sub)[:, None] < jnp.arange(sub)[None, :]).astype(BF16)
    row = lambda a: a.reshape(1, -1)
    lane_blocks = lambda a: a.reshape(a.shape[0], -1, LANES).transpose(1, 0, 2)
    full = lambda shape: pl.BlockSpec(shape, lambda b, i: (0,) * len(shape))
    ins = [
        (x, pl.BlockSpec((1, ts, d), lambda b, i: (b + x_b0, i, 0))),
        (mod, pl.BlockSpec((1, 6, d), lambda b, i: (b + mod_b0, 0, 0))),
        (lw["w_in"], None), (row(lw["b_in"]), None),
        (lane_blocks(jnp.pad(lw["conv_w"], ((0, 1), (0, 0)))), None), (lane_blocks(row(lw["conv_b"])), None),
        (row(lw["conv_ln_g"]), None), (row(lw["conv_ln_b"]), None),
        (lw["pool_w"], None), (row(lw["pool_scale"]), None),
        (lw["w_out"], None), (row(lw["b_out"]), None),
        (row(lw["ln1_g"]), None), (row(lw["ln1_b"]), None),
        (wr, None), (rbias.reshape(-1, 1), None),
        (perm, None), (perm.T, None), (tri, None),
    ]
    args = [a for a, _ in ins]
    specs = [s if s is not None else full(a.shape) for a, s in ins]
    out_shape = (
        jax.ShapeDtypeStruct((bsz, seq, d), F32),
        jax.ShapeDtypeStruct((bsz, seq, d // 2), U32),
        jax.ShapeDtypeStruct((2, n_tok), I32),
        jax.ShapeDtypeStruct((2, n_tok), I32),
        jax.ShapeDtypeStruct((n_tok, LANES), F32),
        jax.ShapeDtypeStruct((n_experts, LANES), F32),
    )
    out_specs = (
        pl.BlockSpec((1, ts, d), lambda b, i: (b, i, 0)),
        pl.BlockSpec((1, ts, d // 2), lambda b, i: (b, i, 0)),
        pl.BlockSpec((2, ts), lambda b, i: (0, b * ns + i)),
        pl.BlockSpec((2, ts), lambda b, i: (0, b * ns + i)),
        pl.BlockSpec((ts, LANES), lambda b, i: (b * ns + i, 0)),
        pl.BlockSpec((n_experts, LANES), lambda b, i: (0, 0)),
    )
    return pl.pallas_call(
        functools.partial(_mixer_body, alpha, n_experts, sub),
        grid=(bsz, ns),
        in_specs=specs,
        out_specs=out_specs,
        out_shape=out_shape,
        scratch_shapes=[
            pltpu.VMEM((nsub, cw // LANES, CONV_HALO_VREGS * SUBLANES + sub, LANES), F32),
            pltpu.VMEM((nsub, cw // LANES, sub, LANES), F32),
            pltpu.VMEM((nsub, POOL_HALO + sub, pw), F32),
            pltpu.VMEM((CONV_HALO_VREGS * SUBLANES, cw), F32),
            pltpu.VMEM((POOL_HALO, pw), F32),
            pltpu.VMEM((n_experts, LANES), F32),
        ],
        compiler_params=pltpu.CompilerParams(dimension_semantics=("arbitrary", "arbitrary"), vmem_limit_bytes=VMEM_LIMIT),
        name="mixer_router",
    )(*args)


def _dest_body(n_experts, pstart_ref, e_ref, rank_ref, dest_ref):
    e = e_ref[...]
    dest = rank_ref[...]
    for ex in range(n_experts):
        dest = dest + jnp.where(e == ex, pstart_ref[ex], 0)
    dest_ref[...] = dest


def _dest_rows(pstart, eidx, rank):
    n_tok = eidx.shape[1]
    tn = min(8192, n_tok)
    grid_spec = pltpu.PrefetchScalarGridSpec(
        num_scalar_prefetch=1,
        grid=(n_tok // tn,),
        in_specs=[pl.BlockSpec((2, tn), lambda i, ps: (0, i)), pl.BlockSpec((2, tn), lambda i, ps: (0, i))],
        out_specs=pl.BlockSpec((2, tn), lambda i, ps: (0, i)),
    )
    return pl.pallas_call(
        functools.partial(_dest_body, pstart.shape[0]),
        grid_spec=grid_spec,
        out_shape=jax.ShapeDtypeStruct((2, n_tok), I32),
        compiler_params=pltpu.CompilerParams(dimension_semantics=("arbitrary",)),
        name="dest_rows",
    )(pstart, eidx, rank)


def _sc_mesh():
    return plsc.VectorSubcoreMesh(core_axis_name="c", subcore_axis_name="s",
                                  num_cores=SC_CORES, num_subcores=SC_SUBCORES)


def _dispatch(u2, d0, d1, n_rows):
    n_tok, d = u2.shape
    workers = SC_CORES * SC_SUBCORES
    per_w = n_tok // workers
    ch = SC_ROWS_PER_STEP
    nch = per_w // ch

    @functools.partial(
        pl.kernel, mesh=_sc_mesh(),
        out_type=jax.ShapeDtypeStruct((n_rows, d), u2.dtype),
        scratch_types=[pltpu.VMEM((nch, ch), I32), pltpu.VMEM((nch, ch), I32), pltpu.VMEM((2, ch, d), u2.dtype),
                       pltpu.SemaphoreType.DMA((2,)), pltpu.SemaphoreType.DMA((2,))],
        name="sc_dispatch",
    )
    def run(u_hbm, d0_hbm, d1_hbm, xs_hbm, i0, i1, rows, rsem, ssem):
        wid = lax.axis_index("s") * SC_CORES + lax.axis_index("c")
        base = wid * per_w
        pltpu.sync_copy(d0_hbm.at[pl.ds(wid * nch, nch)], i0)
        pltpu.sync_copy(d1_hbm.at[pl.ds(wid * nch, nch)], i1)

        def read(j, b):
            return pltpu.make_async_copy(u_hbm.at[pl.ds(base + j * ch, ch)], rows.at[b], rsem.at[b])

        def scatter(idx, j, b):
            return pltpu.make_async_copy(rows.at[b], xs_hbm.at[idx.at[j]], ssem.at[b])

        read(0, 0).start()

        @pl.loop(0, nch, step=2)
        def _(j0):
            for b in range(2):
                j = j0 + b
                read(j, b).wait()
                scatter(i0, j, b).start()
                scatter(i1, j, b).start()

                @pl.when(j >= 1)
                def _():
                    scatter(i0, j - 1, 1 - b).wait()
                    scatter(i1, j - 1, 1 - b).wait()

                @pl.when(j + 1 < nch)
                def _():
                    read(j + 1, 1 - b).start()

        scatter(i0, nch - 1, 1).wait()
        scatter(i1, nch - 1, 1).wait()

    return run(u2, d0.reshape(n_tok // ch, ch), d1.reshape(n_tok // ch, ch))


def _gather_rows(ys, dd):
    n_out = dd.shape[0]
    d = ys.shape[1]
    workers = SC_CORES * SC_SUBCORES
    per_w = n_out // workers
    ch = SC_ROWS_PER_STEP
    nch = per_w // ch

    @functools.partial(
        pl.kernel, mesh=_sc_mesh(),
        out_type=jax.ShapeDtypeStruct((n_out, d), ys.dtype),
        scratch_types=[pltpu.VMEM((nch, ch), I32), pltpu.VMEM((2, ch, d), ys.dtype),
                       pltpu.SemaphoreType.DMA((2,)), pltpu.SemaphoreType.DMA((2,))],
        name="sc_gather",
    )
    def run(ys_hbm, dd_hbm, out_hbm, idx, rows, gsem, wsem):
        wid = lax.axis_index("s") * SC_CORES + lax.axis_index("c")
        base = wid * per_w
        pltpu.sync_copy(dd_hbm.at[pl.ds(wid * nch, nch)], idx)

        def gather(j, b):
            return pltpu.make_async_copy(ys_hbm.at[idx.at[j]], rows.at[b], gsem.at[b])

        def write(j, b):
            return pltpu.make_async_copy(rows.at[b], out_hbm.at[pl.ds(base + j * ch, ch)], wsem.at[b])

        gather(0, 0).start()

        @pl.loop(0, nch, step=2)
        def _(j0):
            for b in range(2):
                j = j0 + b
                gather(j, b).wait()
                write(j, b).start()

                @pl.when(j >= 1)
                def _():
                    write(j - 1, 1 - b).wait()

                @pl.when(j + 1 < nch)
                def _():
                    gather(j + 1, 1 - b).start()

        write(nch - 1, 1).wait()

    return run(ys, dd.reshape(n_out // ch, ch))


def _expert_body(blk_ref, nused_ref, xs_ref, wg_ref, wu_ref, wd_ref, ys_ref, wgb, wub, wdb):
    j = pl.program_id(0)
    used = j < nused_ref[0]
    new_expert = jnp.logical_or(j == 0, blk_ref[j] != blk_ref[jnp.maximum(j - 1, 0)])

    @pl.when(jnp.logical_and(used, new_expert))
    def _():
        wgb[...] = wg_ref[0, 0].astype(BF16)
        wub[...] = wu_ref[0, 0].astype(BF16)
        wdb[...] = wd_ref[0, 0].astype(BF16)

    @pl.when(used)
    def _():
        lo, hi = _unpack_halves(xs_ref[...])
        lo = lo.astype(BF16)
        hi = hi.astype(BF16)
        half = lo.shape[1]
        fw = wgb.shape[1] // EXPERT_F_SPLIT
        acc = None
        for c in range(EXPERT_F_SPLIT):
            cs = slice(c * fw, (c + 1) * fw)
            g = _dot(lo, wgb[0:half, cs]) + _dot(hi, wgb[half:, cs])
            up = _dot(lo, wub[0:half, cs]) + _dot(hi, wub[half:, cs])
            hid = (g * jax.nn.sigmoid(g) * up).astype(BF16)
            part = _dot(hid, wdb[cs, :])
            acc = part if acc is None else acc + part
        ys_ref[...] = _pack_halves(acc)

    @pl.when(j >= nused_ref[0])
    def _():
        ys_ref[...] = jnp.zeros(ys_ref.shape, U32)


def _experts(blk_e, nused, xs, wg, wu, wd, layer):
    n_rows = xs.shape[0]
    d = wg.shape[2]
    f = wg.shape[3]
    tm = EXPERT_TILE
    grid_spec = pltpu.PrefetchScalarGridSpec(
        num_scalar_prefetch=2,
        grid=(n_rows // tm,),
        in_specs=[
            pl.BlockSpec((tm, d // 2), lambda j, be, nu: (jnp.minimum(j, nu[0] - 1), 0)),
            pl.BlockSpec((1, 1, d, f), lambda j, be, nu: (layer, be[j], 0, 0)),
            pl.BlockSpec((1, 1, d, f), lambda j, be, nu: (layer, be[j], 0, 0)),
            pl.BlockSpec((1, 1, f, d), lambda j, be, nu: (layer, be[j], 0, 0)),
        ],
        out_specs=pl.BlockSpec((tm, d // 2), lambda j, be, nu: (j, 0)),
        scratch_shapes=[pltpu.VMEM((d, f), BF16), pltpu.VMEM((d, f), BF16), pltpu.VMEM((f, d), BF16)],
    )
    return pl.pallas_call(
        _expert_body,
        grid_spec=grid_spec,
        out_shape=jax.ShapeDtypeStruct((n_rows, d // 2), U32),
        compiler_params=pltpu.CompilerParams(dimension_semantics=("arbitrary",), vmem_limit_bytes=VMEM_LIMIT),
        name="expert_ffn",
    )(blk_e, nused, xs, wg, wu, wd)


def _combine_body(alpha, x1_ref, y0_ref, y1_ref, mod_ref, wm_ref, g_ref, b_ref, *rest):
    o_ref = rest[-1]
    wm = wm_ref[...]
    lo0, hi0 = _unpack_halves(y0_ref[...])
    lo1, hi1 = _unpack_halves(y1_ref[...])
    w0, w1 = wm[:, 0:1], wm[:, 1:2]
    f = jnp.concatenate([w0 * lo0 + w1 * lo1, w0 * hi0 + w1 * hi1], axis=1)
    g2 = mod_ref[0][5:6]
    o_ref[...] = _layer_norm_rows(alpha * x1_ref[...] + (1.0 + g2) * f, g_ref[...], b_ref[...])


def _combine(x1, yy, mod, wm, ln_g, ln_b, seq, alpha, mod_b0, out_rows, out_row0, out_prev):
    n_tok, d = x1.shape
    tn = min(COMBINE_TILE, seq)
    per_seq = seq // tn
    nblk = n_tok // tn
    blk0 = out_row0 // tn
    rows = lambda: pl.BlockSpec((tn, d), lambda i: (i, 0))
    in_specs = [rows(), pl.BlockSpec((tn, d // 2), lambda i: (i, 0)),
                pl.BlockSpec((tn, d // 2), lambda i: (nblk + i, 0)),
                pl.BlockSpec((1, 6, d), lambda i: (i // per_seq + mod_b0, 0, 0)),
                pl.BlockSpec((tn, LANES), lambda i: (i, 0)),
                pl.BlockSpec((1, d), lambda i: (0, 0)),
                pl.BlockSpec((1, d), lambda i: (0, 0))]
    args = [x1, yy, yy, mod, wm, ln_g.reshape(1, d), ln_b.reshape(1, d)]
    aliases = {}
    if out_prev is not None:
        in_specs.append(pl.BlockSpec(memory_space=pl.ANY))
        args.append(out_prev)
        aliases = {len(args) - 1: 0}
    return pl.pallas_call(
        functools.partial(_combine_body, alpha),
        grid=(nblk,),
        in_specs=in_specs,
        out_specs=pl.BlockSpec((tn, d), lambda i: (blk0 + i, 0)),
        out_shape=jax.ShapeDtypeStruct((out_rows, d), F32),
        input_output_aliases=aliases,
        compiler_params=pltpu.CompilerParams(dimension_semantics=("arbitrary",), vmem_limit_bytes=VMEM_LIMIT),
        name="combine_ln",
    )(*args)


def kernel(x, c, w_ada, b_ada, w_in, b_in, conv_w, conv_b, conv_ln_g, conv_ln_b, pool_w, pool_scale, w_out, b_out, ln1_g, ln1_b, w_router, router_bias, w_gate, w_up, w_down, ln2_g, ln2_b):
    bsz, seq, d = x.shape
    depth = w_ada.shape[0]
    n_experts = w_router.shape[1]
    alpha = float((2 * depth) ** 0.25)
    tm = EXPERT_TILE
    n_chunks = BATCH_CHUNKS if bsz % BATCH_CHUNKS == 0 else 1
    bc = bsz // n_chunks
    n_tok = bc * seq
    n_rows = 2 * n_tok + n_experts * tm

    mod_all = _ada_mod(c, w_ada, b_ada).reshape(depth, bsz, 6, d)
    wr = jnp.pad(w_router.astype(BF16), ((0, 0), (0, LANES - n_experts)))

    chunks = [x] * n_chunks
    starts = [ci * bc for ci in range(n_chunks)]
    out = None
    for l in range(depth):
        lw = dict(w_in=w_in[l].astype(BF16), b_in=b_in[l], conv_w=conv_w[l], conv_b=conv_b[l],
                  conv_ln_g=conv_ln_g[l], conv_ln_b=conv_ln_b[l], pool_w=pool_w[l].astype(BF16),
                  pool_scale=pool_scale[l], w_out=w_out[l].astype(BF16), b_out=b_out[l],
                  ln1_g=ln1_g[l], ln1_b=ln1_b[l])
        mod = mod_all[l]
        last = l == depth - 1
        st = []
        for ci in range(n_chunks):
            x1, u2, eidx, rank, wm, cnt = _mixer(chunks[ci], mod, lw, wr, router_bias, alpha,
                                                 bc, starts[ci], ci * bc)
            counts = cnt[:, 0].astype(I32)
            tiles = (counts + tm - 1) // tm
            tile_end = jnp.cumsum(tiles)
            pstart = ((tile_end - tiles) * tm).astype(I32)
            nused = tile_end[-1:].astype(I32)
            blk_e = jnp.minimum(
                jnp.sum(tile_end[None, :] <= jnp.arange(n_rows // tm, dtype=I32)[:, None], axis=1), n_experts - 1
            ).astype(I32)
            dest = _dest_rows(pstart, eidx, rank)
            xs = _dispatch(u2.reshape(n_tok, d // 2), dest[0], dest[1], n_rows)
            st.append((x1, wm, blk_e, nused, dest, xs))
        ys = [_experts(blk_e, nused, xs, w_gate, w_up, w_down, l) for (_, _, blk_e, nused, _, xs) in st]
        yy = [_gather_rows(ys[ci], st[ci][4].reshape(2 * n_tok)) for ci in range(n_chunks)]
        for ci in range(n_chunks):
            x1, wm = st[ci][0], st[ci][1]
            if last:
                out = _combine(x1.reshape(n_tok, d), yy[ci], mod, wm, ln2_g[l], ln2_b[l], seq, alpha,
                               ci * bc, bsz * seq, ci * n_tok, out)
            else:
                xc = _combine(x1.reshape(n_tok, d), yy[ci], mod, wm, ln2_g[l], ln2_b[l], seq, alpha,
                              ci * bc, n_tok, 0, None)
                chunks[ci] = xc.reshape(bc, seq, d)
                starts[ci] = 0
    return out.reshape(bsz, seq, d)
```

```python
import functools

import jax
import jax.numpy as jnp
from jax import lax
from jax.experimental import pallas as pl
from jax.experimental.pallas import tpu as pltpu
from jax.experimental.pallas import tpu_sc as plsc

F32 = jnp.float32
BF16 = jnp.bfloat16
I32 = jnp.int32
U32 = jnp.uint32

CONV_KERNEL = 31
POOL_WINDOWS = (2, 4, 8, 16)
N_EXPERT_GROUPS = 4
LN_EPS = 1e-5

SUBLANES = 8
LANES = 128

SEQ_TILE = 512
SEQ_SUB_TILE = 256
EXPERT_TILE = 512
EXPERT_F_SPLIT = 2
COMBINE_TILE = 512
BATCH_CHUNKS = 2
CONV_HALO_VREGS = 32
POOL_HALO = 16
VMEM_LIMIT = 56 * 1024 * 1024
SC_CORES = 2
SC_SUBCORES = 16
SC_ROWS_PER_STEP = 64


def _split_bf16(a):
    hi = a.astype(BF16)
    lo = (a - hi.astype(F32)).astype(BF16)
    return hi, lo


def _dot(a, b):
    return jnp.dot(a, b, preferred_element_type=F32)


def _pack_halves(y):
    h = y.shape[1] // 2
    lo = lax.bitcast_convert_type(y[:, :h].astype(BF16).astype(F32), U32)
    hi = lax.bitcast_convert_type(y[:, h:].astype(BF16).astype(F32), U32)
    return (lo >> 16) | hi


def _unpack_halves(p):
    lo = lax.bitcast_convert_type(p << 16, F32)
    hi = lax.bitcast_convert_type(p & jnp.uint32(0xFFFF0000), F32)
    return lo, hi


def _ada_body(c_ref, w_ref, b_ref, o_ref):
    c = c_ref[...]
    ca = c * jax.nn.sigmoid(c)
    chi, clo = _split_bf16(ca)
    whi, wlo = _split_bf16(w_ref[0])
    o_ref[0] = _dot(chi, whi) + _dot(chi, wlo) + _dot(clo, whi) + b_ref[0]


def _ada_mod(c, w_ada, b_ada):
    n_layers, d, n6 = w_ada.shape
    bsz = c.shape[0]
    tn = n6 // 6
    return pl.pallas_call(
        _ada_body,
        grid=(n_layers, n6 // tn),
        in_specs=[
            pl.BlockSpec((bsz, d), lambda l, j: (0, 0)),
            pl.BlockSpec((1, d, tn), lambda l, j: (l, 0, j)),
            pl.BlockSpec((1, 1, tn), lambda l, j: (l, 0, j)),
        ],
        out_specs=pl.BlockSpec((1, bsz, tn), lambda l, j: (l, 0, j)),
        out_shape=jax.ShapeDtypeStruct((n_layers, bsz, n6), F32),
        compiler_params=pltpu.CompilerParams(dimension_semantics=("arbitrary", "arbitrary"), vmem_limit_bytes=VMEM_LIMIT),
        name="ada_mod",
    )(c, w_ada, b_ada.reshape(n_layers, 1, n6))


def _layer_norm_rows(z, g, b):
    mu = jnp.mean(z, axis=-1, keepdims=True)
    zc = z - mu
    var = jnp.mean(zc * zc, axis=-1, keepdims=True)
    return zc * lax.rsqrt(var + LN_EPS) * g + b


def _combine_rows(alpha, x1, y0, y1, wm, g2, ln_g, ln_b):
    lo0, hi0 = _unpack_halves(y0)
    lo1, hi1 = _unpack_halves(y1)
    w0, w1 = wm[:, 0:1], wm[:, 1:2]
    f = jnp.concatenate([w0 * lo0 + w1 * lo1, w0 * hi0 + w1 * hi1], axis=1)
    return _layer_norm_rows(alpha * x1 + (1.0 + g2) * f, ln_g, ln_b)


def _mixer_body(alpha, n_experts, sub, fused, *refs):
    if fused:
        xp_ref, y0_ref, y1_ref, wmp_ref, modp_ref, l2g_ref, l2b_ref = refs[:7]
        refs = refs[7:]
        x_ref = None
    else:
        x_ref = refs[0]
        refs = refs[1:]
    (mod_ref, win_ref, bin_ref, cw_ref, cb_ref, cg_ref, cbeta_ref, pw_ref, ps_ref,
     wout_ref, bout_ref, l1g_ref, l1b_ref, wr_ref, rb_ref, perm_ref, permt_ref, tri_ref,
     x1_ref, u2_ref, e_ref, rank_ref, wm_ref, cnt_ref,
     ebuf, cbuf, pbuf, vprev, phalo, carry, xbuf) = refs
    ts = x1_ref.shape[1]
    cw = cg_ref.shape[1]
    pw = ps_ref.shape[1]
    gw = pw // len(POOL_WINDOWS)
    fine = sub // SUBLANES
    halo = CONV_HALO_VREGS
    b = pl.program_id(0)
    i = pl.program_id(1)

    @pl.when(i == 0)
    def _():
        vprev[...] = jnp.zeros(vprev.shape, F32)
        phalo[...] = jnp.zeros(phalo.shape, F32)

    @pl.when(jnp.logical_and(b == 0, i == 0))
    def _():
        carry[...] = jnp.zeros(carry.shape, F32)

    mod = mod_ref[0]
    sh1, sc1, g1 = mod[0:1], mod[1:2], mod[2:3]
    sh2, sc2 = mod[3:4], mod[4:5]

    for s in range(ts // sub):
        r0 = s * sub
        if fused:
            x = _combine_rows(alpha, xp_ref[0, r0:r0 + sub, :], y0_ref[r0:r0 + sub, :], y1_ref[r0:r0 + sub, :],
                              wmp_ref[r0:r0 + sub, :], modp_ref[0][5:6], l2g_ref[...], l2b_ref[...])
            xbuf[r0:r0 + sub, :] = x
        else:
            x = x_ref[0, r0:r0 + sub, :]
        u = (x * (1.0 + sc1) + sh1).astype(BF16)
        h = _dot(u, win_ref[...]) + bin_ref[...]
        a = h[:, :cw]
        gate = h[:, cw:2 * cw]
        p = h[:, 2 * cw:]

        v = (a * jax.nn.sigmoid(gate)).astype(BF16)
        vb = _dot(perm_ref[...], v)
        ebuf[s, halo * SUBLANES:, :] = vb
        sl = lax.broadcasted_iota(I32, (SUBLANES, cw), 0)
        for mm in range(halo):
            src = (fine - halo + mm) * SUBLANES
            cur = vb[src:src + SUBLANES, :]
            prev = vprev[mm * SUBLANES:(mm + 1) * SUBLANES, :]
            ebuf[s, mm * SUBLANES:(mm + 1) * SUBLANES, :] = pltpu.roll(
                jnp.where(sl == SUBLANES - 1, prev, cur), 1, 0)
        vprev[...] = vb[(fine - halo) * SUBLANES:, :]
        pbuf[s, 0:POOL_HALO, :] = phalo[...]
        pbuf[s, POOL_HALO:, :] = p
        phalo[...] = p[sub - POOL_HALO:, :]

    for s in range(ts // sub):
        r0 = s * sub
        x = xbuf[r0:r0 + sub, :] if fused else x_ref[0, r0:r0 + sub, :]
        chunk = 8 * SUBLANES
        first = halo - (CONV_KERNEL - 1)
        for q0 in range(0, sub, chunk):
            for c0 in range(0, cw, LANES):
                acc = jnp.broadcast_to(cb_ref[:, c0:c0 + LANES], (chunk, LANES))
                for k in range(CONV_KERNEL):
                    off = q0 + (first + k) * SUBLANES
                    acc = acc + cw_ref[k:k + 1, c0:c0 + LANES] * ebuf[s, off:off + chunk, c0:c0 + LANES]
                cbuf[s, q0:q0 + chunk, c0:c0 + LANES] = acc
        yc = _layer_norm_rows(cbuf[s], cg_ref[...], cbeta_ref[...])
        ya_b = (yc * jax.nn.sigmoid(yc)).astype(BF16)
        ya = _dot(permt_ref[...], ya_b).astype(BF16)

        ext = pbuf[s]
        p = ext[POOL_HALO:, :]
        w_sum = ext + pltpu.roll(ext, 1, 0)
        parts = [w_sum[:, :gw]]
        shift = 2
        for _ in range(len(POOL_WINDOWS) - 1):
            w_sum = w_sum[:, gw:]
            w_sum = w_sum + pltpu.roll(w_sum, shift, 0)
            parts.append(w_sum[:, :gw])
            shift *= 2
        wsum = jnp.concatenate(parts, axis=1)[POOL_HALO:, :]
        pos = (lax.broadcasted_iota(I32, (sub, gw), 0) + (i * ts + r0 + 1)).astype(F32)
        cnt = jnp.concatenate([jnp.minimum(pos, float(wlen)) for wlen in POOL_WINDOWS], axis=1)
        dpool = (wsum / cnt - p).astype(BF16)
        yb_parts = [_dot(dpool[:, gi * gw:(gi + 1) * gw], pw_ref[gi]) for gi in range(len(POOL_WINDOWS))]
        yb = (jnp.concatenate(yb_parts, axis=1) * ps_ref[...]).astype(BF16)

        mix = _dot(ya, wout_ref[0:cw, :]) + _dot(yb, wout_ref[cw:, :]) + bout_ref[...]
        x1 = _layer_norm_rows(alpha * x + (1.0 + g1) * mix, l1g_ref[...], l1b_ref[...])
        x1_ref[0, r0:r0 + sub, :] = x1
        u2 = x1 * (1.0 + sc2) + sh2
        u2_ref[0, r0:r0 + sub, :] = _pack_halves(u2)

        logits = _dot(u2.astype(BF16), wr_ref[...]).T[0:n_experts, :]
        scores = jax.nn.sigmoid(logits)
        sel = scores + rb_ref[...]
        epg = n_experts // N_EXPERT_GROUPS
        io = lax.broadcasted_iota(I32, (epg, sub), 0).astype(F32)
        neg = jnp.float32(-jnp.inf)
        best = None
        for g in range(N_EXPERT_GROUPS):
            sg = sel[g * epg:(g + 1) * epg, :]
            scg = scores[g * epg:(g + 1) * epg, :]
            m1 = jnp.max(sg, axis=0, keepdims=True)
            i1 = jnp.min(jnp.where(sg == m1, io, float(epg)), axis=0, keepdims=True)
            rest = jnp.where(io == i1, neg, sg)
            m2 = jnp.max(rest, axis=0, keepdims=True)
            i2 = jnp.min(jnp.where(jnp.logical_and(rest == m2, io != i1), io, float(epg)), axis=0, keepdims=True)
            s1 = jnp.sum(jnp.where(io == i1, scg, 0.0), axis=0, keepdims=True)
            s2 = jnp.sum(jnp.where(io == i2, scg, 0.0), axis=0, keepdims=True)
            gs = m1 + m2
            cand = (gs, i1 + float(g * epg), i2 + float(g * epg), s1, s2)
            if best is None:
                best = cand
            else:
                upd = gs > best[0]
                best = tuple(jnp.where(upd, cn, bs) for cn, bs in zip(cand, best))
        _, e1, e2, s1, s2 = best
        denom = s1 + s2
        w1 = s1 / denom
        w2 = s2 / denom

        ioe = lax.broadcasted_iota(I32, (n_experts, sub), 0).astype(F32)
        hit1 = ioe == e1
        hit2 = ioe == e2
        onehot = jnp.where(jnp.logical_or(hit1, hit2), 1.0, 0.0)
        before = _dot(onehot.astype(BF16), tri_ref[...]) + carry[:, 0:1]
        r1 = jnp.sum(jnp.where(hit1, before, 0.0), axis=0, keepdims=True)
        r2 = jnp.sum(jnp.where(hit2, before, 0.0), axis=0, keepdims=True)
        carry[...] = carry[...] + jnp.sum(onehot, axis=1, keepdims=True)

        e_ref[0:1, r0:r0 + sub] = e1.astype(I32)
        e_ref[1:2, r0:r0 + sub] = e2.astype(I32)
        rank_ref[0:1, r0:r0 + sub] = r1.astype(I32)
        rank_ref[1:2, r0:r0 + sub] = r2.astype(I32)
        rowi = lax.broadcasted_iota(I32, (LANES, sub), 0)
        wpad = jnp.where(rowi == 0, w1, jnp.where(rowi == 1, w2, 0.0))
        wm_ref[r0:r0 + sub, :] = wpad.T

    cnt_ref[...] = carry[...]


def _row_perm(n):
    r = jnp.arange(n)
    t = (n // SUBLANES) * (r % SUBLANES) + r // SUBLANES
    return (t[:, None] == jnp.arange(n)[None, :]).astype(BF16)


def _mixer(x, mod, lw, wr, rbias, alpha, bsz, x_b0, mod_b0, prev=None):
    _, seq, d = x.shape
    ts = min(SEQ_TILE, seq)
    sub = min(SEQ_SUB_TILE, ts)
    nsub = ts // sub
    ns = seq // ts
    n_tok = bsz * seq
    n_experts = rbias.shape[0]
    cw = lw["conv_w"].shape[1]
    pw = lw["pool_scale"].shape[0]
    perm = _row_perm(sub)
    tri = (jnp.arange(sub)[:, None] < jnp.arange(sub)[None, :]).astype(BF16)
    row = lambda a: a.reshape(1, -1)
    full = lambda shape: pl.BlockSpec(shape, lambda b, i: (0,) * len(shape))
    if prev is None:
        ins = [(x, pl.BlockSpec((1, ts, d), lambda b, i: (b + x_b0, i, 0)))]
    else:
        yy, wmp, modp, l2g, l2b = prev
        nblk = n_tok // ts
        ins = [
            (x, pl.BlockSpec((1, ts, d), lambda b, i: (b + x_b0, i, 0))),
            (yy, pl.BlockSpec((ts, d // 2), lambda b, i: (b * ns + i, 0))),
            (yy, pl.BlockSpec((ts, d // 2), lambda b, i: (nblk + b * ns + i, 0))),
            (wmp, pl.BlockSpec((ts, LANES), lambda b, i: (b * ns + i, 0))),
            (modp, pl.BlockSpec((1, 6, d), lambda b, i: (b + mod_b0, 0, 0))),
            (row(l2g), None), (row(l2b), None),
        ]
    ins += [
        (mod, pl.BlockSpec((1, 6, d), lambda b, i: (b + mod_b0, 0, 0))),
        (lw["w_in"], None), (row(lw["b_in"]), None),
        (jnp.pad(lw["conv_w"], ((0, 1), (0, 0))), None), (row(lw["conv_b"]), None),
        (row(lw["conv_ln_g"]), None), (row(lw["conv_ln_b"]), None),
        (lw["pool_w"], None), (row(lw["pool_scale"]), None),
        (lw["w_out"], None), (row(lw["b_out"]), None),
        (row(lw["ln1_g"]), None), (row(lw["ln1_b"]), None),
        (wr, None), (rbias.reshape(-1, 1), None),
        (perm, None), (perm.T, None), (tri, None),
    ]
    args = [a for a, _ in ins]
    specs = [s if s is not None else full(a.shape) for a, s in ins]
    out_shape = (
        jax.ShapeDtypeStruct((bsz, seq, d), F32),
        jax.ShapeDtypeStruct((bsz, seq, d // 2), U32),
        jax.ShapeDtypeStruct((2, n_tok), I32),
        jax.ShapeDtypeStruct((2, n_tok), I32),
        jax.ShapeDtypeStruct((n_tok, LANES), F32),
        jax.ShapeDtypeStruct((n_experts, LANES), F32),
    )
    out_specs = (
        pl.BlockSpec((1, ts, d), lambda b, i: (b, i, 0)),
        pl.BlockSpec((1, ts, d // 2), lambda b, i: (b, i, 0)),
        pl.BlockSpec((2, ts), lambda b, i: (0, b * ns + i)),
        pl.BlockSpec((2, ts), lambda b, i: (0, b * ns + i)),
        pl.BlockSpec((ts, LANES), lambda b, i: (b * ns + i, 0)),
        pl.BlockSpec((n_experts, LANES), lambda b, i: (0, 0)),
    )
    return pl.pallas_call(
        functools.partial(_mixer_body, alpha, n_experts, sub, prev is not None),
        grid=(bsz, ns),
        in_specs=specs,
        out_specs=out_specs,
        out_shape=out_shape,
        scratch_shapes=[
            pltpu.VMEM((nsub, CONV_HALO_VREGS * SUBLANES + sub, cw), F32),
            pltpu.VMEM((nsub, sub, cw), F32),
            pltpu.VMEM((nsub, POOL_HALO + sub, pw), F32),
            pltpu.VMEM((CONV_HALO_VREGS * SUBLANES, cw), F32),
            pltpu.VMEM((POOL_HALO, pw), F32),
            pltpu.VMEM((n_experts, LANES), F32),
            pltpu.VMEM((ts, d) if prev is not None else (SUBLANES, LANES), F32),
        ],
        compiler_params=pltpu.CompilerParams(dimension_semantics=("arbitrary", "arbitrary"), vmem_limit_bytes=VMEM_LIMIT),
        name="mixer_router",
    )(*args)


def _dest_body(n_experts, pstart_ref, e_ref, rank_ref, dest_ref):
    e = e_ref[...]
    dest = rank_ref[...]
    for ex in range(n_experts):
        dest = dest + jnp.where(e == ex, pstart_ref[ex], 0)
    dest_ref[...] = dest


def _dest_rows(pstart, eidx, rank):
    n_tok = eidx.shape[1]
    tn = min(8192, n_tok)
    grid_spec = pltpu.PrefetchScalarGridSpec(
        num_scalar_prefetch=1,
        grid=(n_tok // tn,),
        in_specs=[pl.BlockSpec((2, tn), lambda i, ps: (0, i)), pl.BlockSpec((2, tn), lambda i, ps: (0, i))],
        out_specs=pl.BlockSpec((2, tn), lambda i, ps: (0, i)),
    )
    return pl.pallas_call(
        functools.partial(_dest_body, pstart.shape[0]),
        grid_spec=grid_spec,
        out_shape=jax.ShapeDtypeStruct((2, n_tok), I32),
        compiler_params=pltpu.CompilerParams(dimension_semantics=("arbitrary",)),
        name="dest_rows",
    )(pstart, eidx, rank)


def _sc_mesh():
    return plsc.VectorSubcoreMesh(core_axis_name="c", subcore_axis_name="s",
                                  num_cores=SC_CORES, num_subcores=SC_SUBCORES)


def _dispatch(u2, d0, d1, n_rows):
    n_tok, d = u2.shape
    workers = SC_CORES * SC_SUBCORES
    per_w = n_tok // workers
    ch = SC_ROWS_PER_STEP
    nch = per_w // ch

    @functools.partial(
        pl.kernel, mesh=_sc_mesh(),
        out_type=jax.ShapeDtypeStruct((n_rows, d), u2.dtype),
        scratch_types=[pltpu.VMEM((nch, ch), I32), pltpu.VMEM((nch, ch), I32), pltpu.VMEM((2, ch, d), u2.dtype),
                       pltpu.SemaphoreType.DMA((2,)), pltpu.SemaphoreType.DMA((2,))],
        name="sc_dispatch",
    )
    def run(u_hbm, d0_hbm, d1_hbm, xs_hbm, i0, i1, rows, rsem, ssem):
        wid = lax.axis_index("s") * SC_CORES + lax.axis_index("c")
        base = wid * per_w
        pltpu.sync_copy(d0_hbm.at[pl.ds(wid * nch, nch)], i0)
        pltpu.sync_copy(d1_hbm.at[pl.ds(wid * nch, nch)], i1)

        def read(j, b):
            return pltpu.make_async_copy(u_hbm.at[pl.ds(base + j * ch, ch)], rows.at[b], rsem.at[b])

        def scatter(idx, j, b):
            return pltpu.make_async_copy(rows.at[b], xs_hbm.at[idx.at[j]], ssem.at[b])

        read(0, 0).start()

        @pl.loop(0, nch, step=2)
        def _(j0):
            for b in range(2):
                j = j0 + b
                read(j, b).wait()
                scatter(i0, j, b).start()
                scatter(i1, j, b).start()

                @pl.when(j >= 1)
                def _():
                    scatter(i0, j - 1, 1 - b).wait()
                    scatter(i1, j - 1, 1 - b).wait()

                @pl.when(j + 1 < nch)
                def _():
                    read(j + 1, 1 - b).start()

        scatter(i0, nch - 1, 1).wait()
        scatter(i1, nch - 1, 1).wait()

    return run(u2, d0.reshape(n_tok // ch, ch), d1.reshape(n_tok // ch, ch))


def _gather_rows(ys, dd):
    n_out = dd.shape[0]
    d = ys.shape[1]
    workers = SC_CORES * SC_SUBCORES
    per_w = n_out // workers
    ch = SC_ROWS_PER_STEP
    nch = per_w // ch

    @functools.partial(
        pl.kernel, mesh=_sc_mesh(),
        out_type=jax.ShapeDtypeStruct((n_out, d), ys.dtype),
        scratch_types=[pltpu.VMEM((nch, ch), I32), pltpu.VMEM((2, ch, d), ys.dtype),
                       pltpu.SemaphoreType.DMA((2,)), pltpu.SemaphoreType.DMA((2,))],
        name="sc_gather",
    )
    def run(ys_hbm, dd_hbm, out_hbm, idx, rows, gsem, wsem):
        wid = lax.axis_index("s") * SC_CORES + lax.axis_index("c")
        base = wid * per_w
        pltpu.sync_copy(dd_hbm.at[pl.ds(wid * nch, nch)], idx)

        def gather(j, b):
            return pltpu.make_async_copy(ys_hbm.at[idx.at[j]], rows.at[b], gsem.at[b])

        def write(j, b):
            return pltpu.make_async_copy(rows.at[b], out_hbm.at[pl.ds(base + j * ch, ch)], wsem.at[b])

        gather(0, 0).start()

        @pl.loop(0, nch, step=2)
        def _(j0):
            for b in range(2):
                j = j0 + b
                gather(j, b).wait()
                write(j, b).start()

                @pl.when(j >= 1)
                def _():
                    write(j - 1, 1 - b).wait()

                @pl.when(j + 1 < nch)
                def _():
                    gather(j + 1, 1 - b).start()

        write(nch - 1, 1).wait()

    return run(ys, dd.reshape(n_out // ch, ch))


def _expert_body(blk_ref, nused_ref, xs_ref, wg_ref, wu_ref, wd_ref, ys_ref, wgb, wub, wdb):
    j = pl.program_id(0)
    used = j < nused_ref[0]
    new_expert = jnp.logical_or(j == 0, blk_ref[j] != blk_ref[jnp.maximum(j - 1, 0)])

    @pl.when(jnp.logical_and(used, new_expert))
    def _():
        wgb[...] = wg_ref[0, 0].astype(BF16)
        wub[...] = wu_ref[0, 0].astype(BF16)
        wdb[...] = wd_ref[0, 0].astype(BF16)

    @pl.when(used)
    def _():
        lo, hi = _unpack_halves(xs_ref[...])
        lo = lo.astype(BF16)
        hi = hi.astype(BF16)
        half = lo.shape[1]
        fw = wgb.shape[1] // EXPERT_F_SPLIT
        acc = None
        for c in range(EXPERT_F_SPLIT):
            cs = slice(c * fw, (c + 1) * fw)
            g = _dot(lo, wgb[0:half, cs]) + _dot(hi, wgb[half:, cs])
            up = _dot(lo, wub[0:half, cs]) + _dot(hi, wub[half:, cs])
            hid = (g * jax.nn.sigmoid(g) * up).astype(BF16)
            part = _dot(hid, wdb[cs, :])
            acc = part if acc is None else acc + part
        ys_ref[...] = _pack_halves(acc)

    @pl.when(j >= nused_ref[0])
    def _():
        ys_ref[...] = jnp.zeros(ys_ref.shape, U32)


def _experts(blk_e, nused, xs, wg, wu, wd, layer):
    n_rows = xs.shape[0]
    d = wg.shape[2]
    f = wg.shape[3]
    tm = EXPERT_TILE
    grid_spec = pltpu.PrefetchScalarGridSpec(
        num_scalar_prefetch=2,
        grid=(n_rows // tm,),
        in_specs=[
            pl.BlockSpec((tm, d // 2), lambda j, be, nu: (jnp.minimum(j, nu[0] - 1), 0)),
            pl.BlockSpec((1, 1, d, f), lambda j, be, nu: (layer, be[j], 0, 0)),
            pl.BlockSpec((1, 1, d, f), lambda j, be, nu: (layer, be[j], 0, 0)),
            pl.BlockSpec((1, 1, f, d), lambda j, be, nu: (layer, be[j], 0, 0)),
        ],
        out_specs=pl.BlockSpec((tm, d // 2), lambda j, be, nu: (j, 0)),
        scratch_shapes=[pltpu.VMEM((d, f), BF16), pltpu.VMEM((d, f), BF16), pltpu.VMEM((f, d), BF16)],
    )
    return pl.pallas_call(
        _expert_body,
        grid_spec=grid_spec,
        out_shape=jax.ShapeDtypeStruct((n_rows, d // 2), U32),
        compiler_params=pltpu.CompilerParams(dimension_semantics=("arbitrary",), vmem_limit_bytes=VMEM_LIMIT),
        name="expert_ffn",
    )(blk_e, nused, xs, wg, wu, wd)


def _combine_body(alpha, x1_ref, y0_ref, y1_ref, mod_ref, wm_ref, g_ref, b_ref, *rest):
    o_ref = rest[-1]
    o_ref[...] = _combine_rows(alpha, x1_ref[...], y0_ref[...], y1_ref[...], wm_ref[...],
                               mod_ref[0][5:6], g_ref[...], b_ref[...])


def _combine(x1, yy, mod, wm, ln_g, ln_b, seq, alpha, mod_b0, out_rows, out_row0, out_prev):
    n_tok, d = x1.shape
    tn = min(COMBINE_TILE, seq)
    per_seq = seq // tn
    nblk = n_tok // tn
    blk0 = out_row0 // tn
    rows = lambda: pl.BlockSpec((tn, d), lambda i: (i, 0))
    in_specs = [rows(), pl.BlockSpec((tn, d // 2), lambda i: (i, 0)),
                pl.BlockSpec((tn, d // 2), lambda i: (nblk + i, 0)),
                pl.BlockSpec((1, 6, d), lambda i: (i // per_seq + mod_b0, 0, 0)),
                pl.BlockSpec((tn, LANES), lambda i: (i, 0)),
                pl.BlockSpec((1, d), lambda i: (0, 0)),
                pl.BlockSpec((1, d), lambda i: (0, 0))]
    args = [x1, yy, yy, mod, wm, ln_g.reshape(1, d), ln_b.reshape(1, d)]
    aliases = {}
    if out_prev is not None:
        in_specs.append(pl.BlockSpec(memory_space=pl.ANY))
        args.append(out_prev)
        aliases = {len(args) - 1: 0}
    return pl.pallas_call(
        functools.partial(_combine_body, alpha),
        grid=(nblk,),
        in_specs=in_specs,
        out_specs=pl.BlockSpec((tn, d), lambda i: (blk0 + i, 0)),
        out_shape=jax.ShapeDtypeStruct((out_rows, d), F32),
        input_output_aliases=aliases,
        compiler_params=pltpu.CompilerParams(dimension_semantics=("arbitrary",), vmem_limit_bytes=VMEM_LIMIT),
        name="combine_ln",
    )(*args)


def kernel(x, c, w_ada, b_ada, w_in, b_in, conv_w, conv_b, conv_ln_g, conv_ln_b, pool_w, pool_scale, w_out, b_out, ln1_g, ln1_b, w_router, router_bias, w_gate, w_up, w_down, ln2_g, ln2_b):
    bsz, seq, d = x.shape
    depth = w_ada.shape[0]
    n_experts = w_router.shape[1]
    alpha = float((2 * depth) ** 0.25)
    tm = EXPERT_TILE
    n_chunks = BATCH_CHUNKS if bsz % BATCH_CHUNKS == 0 else 1
    bc = bsz // n_chunks
    n_tok = bc * seq
    n_rows = 2 * n_tok + n_experts * tm

    mod_all = _ada_mod(c, w_ada, b_ada).reshape(depth, bsz, 6, d)
    wr = jnp.pad(w_router.astype(BF16), ((0, 0), (0, LANES - n_experts)))

    chunks = [x] * n_chunks
    starts = [ci * bc for ci in range(n_chunks)]
    prevs = [None] * n_chunks
    out = None
    for l in range(depth):
        lw = dict(w_in=w_in[l].astype(BF16), b_in=b_in[l], conv_w=conv_w[l], conv_b=conv_b[l],
                  conv_ln_g=conv_ln_g[l], conv_ln_b=conv_ln_b[l], pool_w=pool_w[l].astype(BF16),
                  pool_scale=pool_scale[l], w_out=w_out[l].astype(BF16), b_out=b_out[l],
                  ln1_g=ln1_g[l], ln1_b=ln1_b[l])
        mod = mod_all[l]
        last = l == depth - 1
        st = []
        for ci in range(n_chunks):
            x1, u2, eidx, rank, wm, cnt = _mixer(chunks[ci], mod, lw, wr, router_bias, alpha,
                                                 bc, starts[ci], ci * bc, prevs[ci])
            counts = cnt[:, 0].astype(I32)
            tiles = (counts + tm - 1) // tm
            tile_end = jnp.cumsum(tiles)
            pstart = ((tile_end - tiles) * tm).astype(I32)
            nused = tile_end[-1:].astype(I32)
            blk_e = jnp.minimum(
                jnp.sum(tile_end[None, :] <= jnp.arange(n_rows // tm, dtype=I32)[:, None], axis=1), n_experts - 1
            ).astype(I32)
            dest = _dest_rows(pstart, eidx, rank)
            xs = _dispatch(u2.reshape(n_tok, d // 2), dest[0], dest[1], n_rows)
            st.append((x1, wm, blk_e, nused, dest, xs))
        ys = [_experts(blk_e, nused, xs, w_gate, w_up, w_down, l) for (_, _, blk_e, nused, _, xs) in st]
        yy = [_gather_rows(ys[ci], st[ci][4].reshape(2 * n_tok)) for ci in range(n_chunks)]
        for ci in range(n_chunks):
            x1, wm = st[ci][0], st[ci][1]
            if last:
                out = _combine(x1.reshape(n_tok, d), yy[ci], mod, wm, ln2_g[l], ln2_b[l], seq, alpha,
                               ci * bc, bsz * seq, ci * n_tok, out)
            else:
                chunks[ci] = x1
                starts[ci] = 0
                prevs[ci] = (yy[ci], wm, mod, ln2_g[l], ln2_b[l])
    return out.reshape(bsz, seq, d)
```

```python
import functools

import jax
import jax.numpy as jnp
from jax import lax
from jax.experimental import pallas as pl
from jax.experimental.pallas import tpu as pltpu
from jax.experimental.pallas import tpu_sc as plsc

F32 = jnp.float32
BF16 = jnp.bfloat16
I32 = jnp.int32
U32 = jnp.uint32

CONV_KERNEL = 31
POOL_WINDOWS = (2, 4, 8, 16)
N_EXPERT_GROUPS = 4
LN_EPS = 1e-5

SUBLANES = 8
LANES = 128

SEQ_TILE = 512
SEQ_SUB_TILE = 256
STAGE_LEAD = 2
EXPERT_TILE = 1024
EXPERT_F_SPLIT = 2
COMBINE_TILE = 1024
BATCH_CHUNKS = 2
CONV_HALO_VREGS = 32
POOL_HALO = 16
VMEM_LIMIT = 56 * 1024 * 1024
SC_CORES = 2
SC_SUBCORES = 16
SC_ROWS_PER_STEP = 64


def _split_bf16(a):
    hi = a.astype(BF16)
    lo = (a - hi.astype(F32)).astype(BF16)
    return hi, lo


def _dot(a, b):
    return jnp.dot(a, b, preferred_element_type=F32)


def _pack_halves(y):
    h = y.shape[1] // 2
    lo = lax.bitcast_convert_type(y[:, :h].astype(BF16).astype(F32), U32)
    hi = lax.bitcast_convert_type(y[:, h:].astype(BF16).astype(F32), U32)
    return (lo >> 16) | hi


def _unpack_halves(p):
    lo = lax.bitcast_convert_type(p << 16, F32)
    hi = lax.bitcast_convert_type(p & jnp.uint32(0xFFFF0000), F32)
    return lo, hi


def _ada_body(c_ref, w_ref, b_ref, o_ref):
    c = c_ref[...]
    ca = c * jax.nn.sigmoid(c)
    chi, clo = _split_bf16(ca)
    whi, wlo = _split_bf16(w_ref[0])
    o_ref[0] = _dot(chi, whi) + _dot(chi, wlo) + _dot(clo, whi) + b_ref[0]


def _ada_mod(c, w_ada, b_ada):
    n_layers, d, n6 = w_ada.shape
    bsz = c.shape[0]
    tn = n6 // 6
    return pl.pallas_call(
        _ada_body,
        grid=(n_layers, n6 // tn),
        in_specs=[
            pl.BlockSpec((bsz, d), lambda l, j: (0, 0)),
            pl.BlockSpec((1, d, tn), lambda l, j: (l, 0, j)),
            pl.BlockSpec((1, 1, tn), lambda l, j: (l, 0, j)),
        ],
        out_specs=pl.BlockSpec((1, bsz, tn), lambda l, j: (l, 0, j)),
        out_shape=jax.ShapeDtypeStruct((n_layers, bsz, n6), F32),
        compiler_params=pltpu.CompilerParams(dimension_semantics=("arbitrary", "arbitrary"), vmem_limit_bytes=VMEM_LIMIT),
        name="ada_mod",
    )(c, w_ada, b_ada.reshape(n_layers, 1, n6))


def _layer_norm_rows(z, g, b):
    mu = jnp.mean(z, axis=-1, keepdims=True)
    zc = z - mu
    var = jnp.mean(zc * zc, axis=-1, keepdims=True)
    return zc * lax.rsqrt(var + LN_EPS) * g + b


def _combine_rows(alpha, x1, y0, y1, wm, g2, ln_g, ln_b):
    lo0, hi0 = _unpack_halves(y0)
    lo1, hi1 = _unpack_halves(y1)
    w0, w1 = wm[:, 0:1], wm[:, 1:2]
    f = jnp.concatenate([w0 * lo0 + w1 * lo1, w0 * hi0 + w1 * hi1], axis=1)
    return _layer_norm_rows(alpha * x1 + (1.0 + g2) * f, ln_g, ln_b)


def _mixer_body(alpha, n_experts, sub, fused, *refs):
    if fused:
        xp_ref, y0_ref, y1_ref, wmp_ref, modp_ref, l2g_ref, l2b_ref = refs[:7]
        refs = refs[7:]
        x_ref = None
    else:
        x_ref = refs[0]
        refs = refs[1:]
    (mod_ref, win_ref, bin_ref, cw_ref, cb_ref, cg_ref, cbeta_ref, pw_ref, ps_ref,
     wout_ref, bout_ref, l1g_ref, l1b_ref, wr_ref, rb_ref, perm_ref, permt_ref, tri_ref,
     x1_ref, u2_ref, e_ref, rank_ref, wm_ref, cnt_ref,
     ebuf, cbuf, pbuf, vprev, phalo, carry, xbuf) = refs
    ts = x1_ref.shape[1]
    cw = cg_ref.shape[1]
    pw = ps_ref.shape[1]
    gw = pw // len(POOL_WINDOWS)
    fine = sub // SUBLANES
    halo = CONV_HALO_VREGS
    b = pl.program_id(0)
    i = pl.program_id(1)

    @pl.when(i == 0)
    def _():
        vprev[...] = jnp.zeros(vprev.shape, F32)
        phalo[...] = jnp.zeros(phalo.shape, F32)

    @pl.when(jnp.logical_and(b == 0, i == 0))
    def _():
        carry[...] = jnp.zeros(carry.shape, F32)

    mod = mod_ref[0]
    sh1, sc1, g1 = mod[0:1], mod[1:2], mod[2:3]
    sh2, sc2 = mod[3:4], mod[4:5]

    def stage1(s):
        r0 = s * sub
        if fused:
            x = _combine_rows(alpha, xp_ref[0, r0:r0 + sub, :], y0_ref[r0:r0 + sub, :], y1_ref[r0:r0 + sub, :],
                              wmp_ref[r0:r0 + sub, :], modp_ref[0][5:6], l2g_ref[...], l2b_ref[...])
            xbuf[r0:r0 + sub, :] = x
        else:
            x = x_ref[0, r0:r0 + sub, :]
        u = (x * (1.0 + sc1) + sh1).astype(BF16)
        h = _dot(u, win_ref[...]) + bin_ref[...]
        a = h[:, :cw]
        gate = h[:, cw:2 * cw]
        p = h[:, 2 * cw:]

        v = (a * jax.nn.sigmoid(gate)).astype(BF16)
        vb = _dot(perm_ref[...], v)
        ebuf[s, halo * SUBLANES:, :] = vb
        sl = lax.broadcasted_iota(I32, (SUBLANES, cw), 0)
        for mm in range(halo):
            src = (fine - halo + mm) * SUBLANES
            cur = vb[src:src + SUBLANES, :]
            prev = vprev[mm * SUBLANES:(mm + 1) * SUBLANES, :]
            ebuf[s, mm * SUBLANES:(mm + 1) * SUBLANES, :] = pltpu.roll(
                jnp.where(sl == SUBLANES - 1, prev, cur), 1, 0)
        vprev[...] = vb[(fine - halo) * SUBLANES:, :]
        pbuf[s, 0:POOL_HALO, :] = phalo[...]
        pbuf[s, POOL_HALO:, :] = p
        phalo[...] = p[sub - POOL_HALO:, :]

    def stage2(s):
        r0 = s * sub
        x = xbuf[r0:r0 + sub, :] if fused else x_ref[0, r0:r0 + sub, :]
        chunk = 8 * SUBLANES
        first = halo - (CONV_KERNEL - 1)
        for q0 in range(0, sub, chunk):
            for c0 in range(0, cw, LANES):
                acc = jnp.broadcast_to(cb_ref[:, c0:c0 + LANES], (chunk, LANES))
                for k in range(CONV_KERNEL):
                    off = q0 + (first + k) * SUBLANES
                    acc = acc + cw_ref[k:k + 1, c0:c0 + LANES] * ebuf[s, off:off + chunk, c0:c0 + LANES]
                cbuf[s, q0:q0 + chunk, c0:c0 + LANES] = acc
        yc = _layer_norm_rows(cbuf[s], cg_ref[...], cbeta_ref[...])
        ya_b = (yc * jax.nn.sigmoid(yc)).astype(BF16)
        ya = _dot(permt_ref[...], ya_b).astype(BF16)

        ext = pbuf[s]
        p = ext[POOL_HALO:, :]
        w_sum = ext + pltpu.roll(ext, 1, 0)
        parts = [w_sum[:, :gw]]
        shift = 2
        for _ in range(len(POOL_WINDOWS) - 1):
            w_sum = w_sum[:, gw:]
            w_sum = w_sum + pltpu.roll(w_sum, shift, 0)
            parts.append(w_sum[:, :gw])
            shift *= 2
        wsum = jnp.concatenate(parts, axis=1)[POOL_HALO:, :]
        pos = (lax.broadcasted_iota(I32, (sub, gw), 0) + (i * ts + r0 + 1)).astype(F32)
        cnt = jnp.concatenate([jnp.minimum(pos, float(wlen)) for wlen in POOL_WINDOWS], axis=1)
        dpool = (wsum / cnt - p).astype(BF16)
        yb_parts = [_dot(dpool[:, gi * gw:(gi + 1) * gw], pw_ref[gi]) for gi in range(len(POOL_WINDOWS))]
        yb = (jnp.concatenate(yb_parts, axis=1) * ps_ref[...]).astype(BF16)

        mix = _dot(ya, wout_ref[0:cw, :]) + _dot(yb, wout_ref[cw:, :]) + bout_ref[...]
        x1 = _layer_norm_rows(alpha * x + (1.0 + g1) * mix, l1g_ref[...], l1b_ref[...])
        x1_ref[0, r0:r0 + sub, :] = x1
        u2 = x1 * (1.0 + sc2) + sh2
        u2_ref[0, r0:r0 + sub, :] = _pack_halves(u2)

        logits = _dot(u2.astype(BF16), wr_ref[...]).T[0:n_experts, :]
        scores = jax.nn.sigmoid(logits)
        sel = scores + rb_ref[...]
        epg = n_experts // N_EXPERT_GROUPS
        io = lax.broadcasted_iota(I32, (epg, sub), 0).astype(F32)
        neg = jnp.float32(-jnp.inf)
        best = None
        for g in range(N_EXPERT_GROUPS):
            sg = sel[g * epg:(g + 1) * epg, :]
            scg = scores[g * epg:(g + 1) * epg, :]
            m1 = jnp.max(sg, axis=0, keepdims=True)
            i1 = jnp.min(jnp.where(sg == m1, io, float(epg)), axis=0, keepdims=True)
            rest = jnp.where(io == i1, neg, sg)
            m2 = jnp.max(rest, axis=0, keepdims=True)
            i2 = jnp.min(jnp.where(jnp.logical_and(rest == m2, io != i1), io, float(epg)), axis=0, keepdims=True)
            s1 = jnp.sum(jnp.where(io == i1, scg, 0.0), axis=0, keepdims=True)
            s2 = jnp.sum(jnp.where(io == i2, scg, 0.0), axis=0, keepdims=True)
            gs = m1 + m2
            cand = (gs, i1 + float(g * epg), i2 + float(g * epg), s1, s2)
            if best is None:
                best = cand
            else:
                upd = gs > best[0]
                best = tuple(jnp.where(upd, cn, bs) for cn, bs in zip(cand, best))
        _, e1, e2, s1, s2 = best
        denom = s1 + s2
        w1 = s1 / denom
        w2 = s2 / denom

        ioe = lax.broadcasted_iota(I32, (n_experts, sub), 0).astype(F32)
        hit1 = ioe == e1
        hit2 = ioe == e2
        onehot = jnp.where(jnp.logical_or(hit1, hit2), 1.0, 0.0)
        before = _dot(onehot.astype(BF16), tri_ref[...]) + carry[:, 0:1]
        r1 = jnp.sum(jnp.where(hit1, before, 0.0), axis=0, keepdims=True)
        r2 = jnp.sum(jnp.where(hit2, before, 0.0), axis=0, keepdims=True)
        carry[...] = carry[...] + jnp.sum(onehot, axis=1, keepdims=True)

        e_ref[0:1, r0:r0 + sub] = e1.astype(I32)
        e_ref[1:2, r0:r0 + sub] = e2.astype(I32)
        rank_ref[0:1, r0:r0 + sub] = r1.astype(I32)
        rank_ref[1:2, r0:r0 + sub] = r2.astype(I32)
        rowi = lax.broadcasted_iota(I32, (LANES, sub), 0)
        wpad = jnp.where(rowi == 0, w1, jnp.where(rowi == 1, w2, 0.0))
        wm_ref[r0:r0 + sub, :] = wpad.T

    nsub = ts // sub
    for s in range(min(STAGE_LEAD, nsub)):
        stage1(s)
    for s in range(nsub):
        if s + STAGE_LEAD < nsub:
            stage1(s + STAGE_LEAD)
        stage2(s)
    cnt_ref[...] = carry[...]


def _row_perm(n):
    r = jnp.arange(n)
    t = (n // SUBLANES) * (r % SUBLANES) + r // SUBLANES
    return (t[:, None] == jnp.arange(n)[None, :]).astype(BF16)


def _mixer(x, mod, lw, wr, rbias, alpha, bsz, x_b0, mod_b0, prev=None):
    _, seq, d = x.shape
    ts = min(SEQ_TILE, seq)
    sub = min(SEQ_SUB_TILE, ts)
    nsub = ts // sub
    ns = seq // ts
    n_tok = bsz * seq
    n_experts = rbias.shape[0]
    cw = lw["conv_w"].shape[1]
    pw = lw["pool_scale"].shape[0]
    perm = _row_perm(sub)
    tri = (jnp.arange(sub)[:, None] < jnp.arange(sub)[None, :]).astype(BF16)
    row = lambda a: a.reshape(1, -1)
    full = lambda shape: pl.BlockSpec(shape, lambda b, i: (0,) * len(shape))
    if prev is None:
        ins = [(x, pl.BlockSpec((1, ts, d), lambda b, i: (b + x_b0, i, 0)))]
    else:
        yy, wmp, modp, l2g, l2b = prev
        nblk = n_tok // ts
        ins = [
            (x, pl.BlockSpec((1, ts, d), lambda b, i: (b + x_b0, i, 0))),
            (yy, pl.BlockSpec((ts, d // 2), lambda b, i: (b * ns + i, 0))),
            (yy, pl.BlockSpec((ts, d // 2), lambda b, i: (nblk + b * ns + i, 0))),
            (wmp, pl.BlockSpec((ts, LANES), lambda b, i: (b * ns + i, 0))),
            (modp, pl.BlockSpec((1, 6, d), lambda b, i: (b + mod_b0, 0, 0))),
            (row(l2g), None), (row(l2b), None),
        ]
    ins += [
        (mod, pl.BlockSpec((1, 6, d), lambda b, i: (b + mod_b0, 0, 0))),
        (lw["w_in"], None), (row(lw["b_in"]), None),
        (jnp.pad(lw["conv_w"], ((0, 1), (0, 0))), None), (row(lw["conv_b"]), None),
        (row(lw["conv_ln_g"]), None), (row(lw["conv_ln_b"]), None),
        (lw["pool_w"], None), (row(lw["pool_scale"]), None),
        (lw["w_out"], None), (row(lw["b_out"]), None),
        (row(lw["ln1_g"]), None), (row(lw["ln1_b"]), None),
        (wr, None), (rbias.reshape(-1, 1), None),
        (perm, None), (perm.T, None), (tri, None),
    ]
    args = [a for a, _ in ins]
    specs = [s if s is not None else full(a.shape) for a, s in ins]
    out_shape = (
        jax.ShapeDtypeStruct((bsz, seq, d), F32),
        jax.ShapeDtypeStruct((bsz, seq, d // 2), U32),
        jax.ShapeDtypeStruct((2, n_tok), I32),
        jax.ShapeDtypeStruct((2, n_tok), I32),
        jax.ShapeDtypeStruct((n_tok, LANES), F32),
        jax.ShapeDtypeStruct((n_experts, LANES), F32),
    )
    out_specs = (
        pl.BlockSpec((1, ts, d), lambda b, i: (b, i, 0)),
        pl.BlockSpec((1, ts, d // 2), lambda b, i: (b, i, 0)),
        pl.BlockSpec((2, ts), lambda b, i: (0, b * ns + i)),
        pl.BlockSpec((2, ts), lambda b, i: (0, b * ns + i)),
        pl.BlockSpec((ts, LANES), lambda b, i: (b * ns + i, 0)),
        pl.BlockSpec((n_experts, LANES), lambda b, i: (0, 0)),
    )
    return pl.pallas_call(
        functools.partial(_mixer_body, alpha, n_experts, sub, prev is not None),
        grid=(bsz, ns),
        in_specs=specs,
        out_specs=out_specs,
        out_shape=out_shape,
        scratch_shapes=[
            pltpu.VMEM((nsub, CONV_HALO_VREGS * SUBLANES + sub, cw), F32),
            pltpu.VMEM((nsub, sub, cw), F32),
            pltpu.VMEM((nsub, POOL_HALO + sub, pw), F32),
            pltpu.VMEM((CONV_HALO_VREGS * SUBLANES, cw), F32),
            pltpu.VMEM((POOL_HALO, pw), F32),
            pltpu.VMEM((n_experts, LANES), F32),
            pltpu.VMEM((ts, d) if prev is not None else (SUBLANES, LANES), F32),
        ],
        compiler_params=pltpu.CompilerParams(dimension_semantics=("arbitrary", "arbitrary"), vmem_limit_bytes=VMEM_LIMIT),
        name="mixer_router",
    )(*args)


def _dest_body(n_experts, pstart_ref, e_ref, rank_ref, dest_ref):
    e = e_ref[...]
    dest = rank_ref[...]
    for ex in range(n_experts):
        dest = dest + jnp.where(e == ex, pstart_ref[ex], 0)
    dest_ref[...] = dest


def _dest_rows(pstart, eidx, rank):
    n_tok = eidx.shape[1]
    tn = min(8192, n_tok)
    grid_spec = pltpu.PrefetchScalarGridSpec(
        num_scalar_prefetch=1,
        grid=(n_tok // tn,),
        in_specs=[pl.BlockSpec((2, tn), lambda i, ps: (0, i)), pl.BlockSpec((2, tn), lambda i, ps: (0, i))],
        out_specs=pl.BlockSpec((2, tn), lambda i, ps: (0, i)),
    )
    return pl.pallas_call(
        functools.partial(_dest_body, pstart.shape[0]),
        grid_spec=grid_spec,
        out_shape=jax.ShapeDtypeStruct((2, n_tok), I32),
        compiler_params=pltpu.CompilerParams(dimension_semantics=("arbitrary",)),
        name="dest_rows",
    )(pstart, eidx, rank)


def _sc_mesh():
    return plsc.VectorSubcoreMesh(core_axis_name="c", subcore_axis_name="s",
                                  num_cores=SC_CORES, num_subcores=SC_SUBCORES)


def _dispatch(u2, d0, d1, n_rows):
    n_tok, d = u2.shape
    workers = SC_CORES * SC_SUBCORES
    per_w = n_tok // workers
    ch = SC_ROWS_PER_STEP
    nch = per_w // ch

    @functools.partial(
        pl.kernel, mesh=_sc_mesh(),
        out_type=jax.ShapeDtypeStruct((n_rows, d), u2.dtype),
        scratch_types=[pltpu.VMEM((nch, ch), I32), pltpu.VMEM((nch, ch), I32), pltpu.VMEM((2, ch, d), u2.dtype),
                       pltpu.SemaphoreType.DMA((2,)), pltpu.SemaphoreType.DMA((2,))],
        name="sc_dispatch",
    )
    def run(u_hbm, d0_hbm, d1_hbm, xs_hbm, i0, i1, rows, rsem, ssem):
        wid = lax.axis_index("s") * SC_CORES + lax.axis_index("c")
        base = wid * per_w
        pltpu.sync_copy(d0_hbm.at[pl.ds(wid * nch, nch)], i0)
        pltpu.sync_copy(d1_hbm.at[pl.ds(wid * nch, nch)], i1)

        def read(j, b):
            return pltpu.make_async_copy(u_hbm.at[pl.ds(base + j * ch, ch)], rows.at[b], rsem.at[b])

        def scatter(idx, j, b):
            return pltpu.make_async_copy(rows.at[b], xs_hbm.at[idx.at[j]], ssem.at[b])

        read(0, 0).start()

        @pl.loop(0, nch, step=2)
        def _(j0):
            for b in range(2):
                j = j0 + b
                read(j, b).wait()
                scatter(i0, j, b).start()
                scatter(i1, j, b).start()

                @pl.when(j >= 1)
                def _():
                    scatter(i0, j - 1, 1 - b).wait()
                    scatter(i1, j - 1, 1 - b).wait()

                @pl.when(j + 1 < nch)
                def _():
                    read(j + 1, 1 - b).start()

        scatter(i0, nch - 1, 1).wait()
        scatter(i1, nch - 1, 1).wait()

    return run(u2, d0.reshape(n_tok // ch, ch), d1.reshape(n_tok // ch, ch))


def _gather_rows(ys, dd):
    n_out = dd.shape[0]
    d = ys.shape[1]
    workers = SC_CORES * SC_SUBCORES
    per_w = n_out // workers
    ch = SC_ROWS_PER_STEP
    nch = per_w // ch

    @functools.partial(
        pl.kernel, mesh=_sc_mesh(),
        out_type=jax.ShapeDtypeStruct((n_out, d), ys.dtype),
        scratch_types=[pltpu.VMEM((nch, ch), I32), pltpu.VMEM((2, ch, d), ys.dtype),
                       pltpu.SemaphoreType.DMA((2,)), pltpu.SemaphoreType.DMA((2,))],
        name="sc_gather",
    )
    def run(ys_hbm, dd_hbm, out_hbm, idx, rows, gsem, wsem):
        wid = lax.axis_index("s") * SC_CORES + lax.axis_index("c")
        base = wid * per_w
        pltpu.sync_copy(dd_hbm.at[pl.ds(wid * nch, nch)], idx)

        def gather(j, b):
            return pltpu.make_async_copy(ys_hbm.at[idx.at[j]], rows.at[b], gsem.at[b])

        def write(j, b):
            return pltpu.make_async_copy(rows.at[b], out_hbm.at[pl.ds(base + j * ch, ch)], wsem.at[b])

        gather(0, 0).start()

        @pl.loop(0, nch, step=2)
        def _(j0):
            for b in range(2):
                j = j0 + b
                gather(j, b).wait()
                write(j, b).start()

                @pl.when(j >= 1)
                def _():
                    write(j - 1, 1 - b).wait()

                @pl.when(j + 1 < nch)
                def _():
                    gather(j + 1, 1 - b).start()

        write(nch - 1, 1).wait()

    return run(ys, dd.reshape(n_out // ch, ch))


def _expert_body(blk_ref, nused_ref, xs_ref, wg_ref, wu_ref, wd_ref, ys_ref, wgb, wub, wdb):
    j = pl.program_id(0)
    used = j < nused_ref[0]
    new_expert = jnp.logical_or(j == 0, blk_ref[j] != blk_ref[jnp.maximum(j - 1, 0)])

    @pl.when(jnp.logical_and(used, new_expert))
    def _():
        wgb[...] = wg_ref[0, 0].astype(BF16)
        wub[...] = wu_ref[0, 0].astype(BF16)
        wdb[...] = wd_ref[0, 0].astype(BF16)

    @pl.when(used)
    def _():
        lo, hi = _unpack_halves(xs_ref[...])
        lo = lo.astype(BF16)
        hi = hi.astype(BF16)
        half = lo.shape[1]
        fw = wgb.shape[1] // EXPERT_F_SPLIT
        acc = None
        for c in range(EXPERT_F_SPLIT):
            cs = slice(c * fw, (c + 1) * fw)
            g = _dot(lo, wgb[0:half, cs]) + _dot(hi, wgb[half:, cs])
            up = _dot(lo, wub[0:half, cs]) + _dot(hi, wub[half:, cs])
            hid = (g * jax.nn.sigmoid(g) * up).astype(BF16)
            part = _dot(hid, wdb[cs, :])
            acc = part if acc is None else acc + part
        ys_ref[...] = _pack_halves(acc)

    @pl.when(j >= nused_ref[0])
    def _():
        ys_ref[...] = jnp.zeros(ys_ref.shape, U32)


def _experts(blk_e, nused, xs, wg, wu, wd, layer):
    n_rows = xs.shape[0]
    d = wg.shape[2]
    f = wg.shape[3]
    tm = EXPERT_TILE
    grid_spec = pltpu.PrefetchScalarGridSpec(
        num_scalar_prefetch=2,
        grid=(n_rows // tm,),
        in_specs=[
            pl.BlockSpec((tm, d // 2), lambda j, be, nu: (jnp.minimum(j, nu[0] - 1), 0)),
            pl.BlockSpec((1, 1, d, f), lambda j, be, nu: (layer, be[j], 0, 0)),
            pl.BlockSpec((1, 1, d, f), lambda j, be, nu: (layer, be[j], 0, 0)),
            pl.BlockSpec((1, 1, f, d), lambda j, be, nu: (layer, be[j], 0, 0)),
        ],
        out_specs=pl.BlockSpec((tm, d // 2), lambda j, be, nu: (j, 0)),
        scratch_shapes=[pltpu.VMEM((d, f), BF16), pltpu.VMEM((d, f), BF16), pltpu.VMEM((f, d), BF16)],
    )
    return pl.pallas_call(
        _expert_body,
        grid_spec=grid_spec,
        out_shape=jax.ShapeDtypeStruct((n_rows, d // 2), U32),
        compiler_params=pltpu.CompilerParams(dimension_semantics=("arbitrary",), vmem_limit_bytes=VMEM_LIMIT),
        name="expert_ffn",
    )(blk_e, nused, xs, wg, wu, wd)


def _combine_body(alpha, x1_ref, y0_ref, y1_ref, mod_ref, wm_ref, g_ref, b_ref, *rest):
    o_ref = rest[-1]
    o_ref[...] = _combine_rows(alpha, x1_ref[...], y0_ref[...], y1_ref[...], wm_ref[...],
                               mod_ref[0][5:6], g_ref[...], b_ref[...])


def _combine(x1, yy, mod, wm, ln_g, ln_b, seq, alpha, mod_b0, out_rows, out_row0, out_prev):
    n_tok, d = x1.shape
    tn = min(COMBINE_TILE, seq)
    per_seq = seq // tn
    nblk = n_tok // tn
    blk0 = out_row0 // tn
    rows = lambda: pl.BlockSpec((tn, d), lambda i: (i, 0))
    in_specs = [rows(), pl.BlockSpec((tn, d // 2), lambda i: (i, 0)),
                pl.BlockSpec((tn, d // 2), lambda i: (nblk + i, 0)),
                pl.BlockSpec((1, 6, d), lambda i: (i // per_seq + mod_b0, 0, 0)),
                pl.BlockSpec((tn, LANES), lambda i: (i, 0)),
                pl.BlockSpec((1, d), lambda i: (0, 0)),
                pl.BlockSpec((1, d), lambda i: (0, 0))]
    args = [x1, yy, yy, mod, wm, ln_g.reshape(1, d), ln_b.reshape(1, d)]
    aliases = {}
    if out_prev is not None:
        in_specs.append(pl.BlockSpec(memory_space=pl.ANY))
        args.append(out_prev)
        aliases = {len(args) - 1: 0}
    return pl.pallas_call(
        functools.partial(_combine_body, alpha),
        grid=(nblk,),
        in_specs=in_specs,
        out_specs=pl.BlockSpec((tn, d), lambda i: (blk0 + i, 0)),
        out_shape=jax.ShapeDtypeStruct((out_rows, d), F32),
        input_output_aliases=aliases,
        compiler_params=pltpu.CompilerParams(dimension_semantics=("arbitrary",), vmem_limit_bytes=VMEM_LIMIT),
        name="combine_ln",
    )(*args)


def kernel(x, c, w_ada, b_ada, w_in, b_in, conv_w, conv_b, conv_ln_g, conv_ln_b, pool_w, pool_scale, w_out, b_out, ln1_g, ln1_b, w_router, router_bias, w_gate, w_up, w_down, ln2_g, ln2_b):
    bsz, seq, d = x.shape
    depth = w_ada.shape[0]
    n_experts = w_router.shape[1]
    alpha = float((2 * depth) ** 0.25)
    tm = EXPERT_TILE
    n_chunks = BATCH_CHUNKS if bsz % BATCH_CHUNKS == 0 else 1
    bc = bsz // n_chunks
    n_tok = bc * seq
    n_rows = 2 * n_tok + n_experts * tm

    mod_all = _ada_mod(c, w_ada, b_ada).reshape(depth, bsz, 6, d)
    wr = jnp.pad(w_router.astype(BF16), ((0, 0), (0, LANES - n_experts)))

    chunks = [x] * n_chunks
    starts = [ci * bc for ci in range(n_chunks)]
    prevs = [None] * n_chunks
    out = None
    for l in range(depth):
        lw = dict(w_in=w_in[l].astype(BF16), b_in=b_in[l], conv_w=conv_w[l], conv_b=conv_b[l],
                  conv_ln_g=conv_ln_g[l], conv_ln_b=conv_ln_b[l], pool_w=pool_w[l].astype(BF16),
                  pool_scale=pool_scale[l], w_out=w_out[l].astype(BF16), b_out=b_out[l],
                  ln1_g=ln1_g[l], ln1_b=ln1_b[l])
        mod = mod_all[l]
        last = l == depth - 1
        st = []
        for ci in range(n_chunks):
            x1, u2, eidx, rank, wm, cnt = _mixer(chunks[ci], mod, lw, wr, router_bias, alpha,
                                                 bc, starts[ci], ci * bc, prevs[ci])
            counts = cnt[:, 0].astype(I32)
            tiles = (counts + tm - 1) // tm
            tile_end = jnp.cumsum(tiles)
            pstart = ((tile_end - tiles) * tm).astype(I32)
            nused = tile_end[-1:].astype(I32)
            blk_e = jnp.minimum(
                jnp.sum(tile_end[None, :] <= jnp.arange(n_rows // tm, dtype=I32)[:, None], axis=1), n_experts - 1
            ).astype(I32)
            dest = _dest_rows(pstart, eidx, rank)
            xs = _dispatch(u2.reshape(n_tok, d // 2), dest[0], dest[1], n_rows)
            st.append((x1, wm, blk_e, nused, dest, xs))
        ys = [_experts(blk_e, nused, xs, w_gate, w_up, w_down, l) for (_, _, blk_e, nused, _, xs) in st]
        yy = [_gather_rows(ys[ci], st[ci][4].reshape(2 * n_tok)) for ci in range(n_chunks)]
        for ci in range(n_chunks):
            x1, wm = st[ci][0], st[ci][1]
            if last:
                out = _combine(x1.reshape(n_tok, d), yy[ci], mod, wm, ln2_g[l], ln2_b[l], seq, alpha,
                               ci * bc, bsz * seq, ci * n_tok, out)
            else:
                chunks[ci] = x1
                starts[ci] = 0
                prevs[ci] = (yy[ci], wm, mod, ln2_g[l], ln2_b[l])
    return out.reshape(bsz, seq, d)
```

```python
import functools

import jax
import jax.numpy as jnp
from jax import lax
from jax.experimental import pallas as pl
from jax.experimental.pallas import tpu as pltpu
from jax.experimental.pallas import tpu_sc as plsc

F32 = jnp.float32
BF16 = jnp.bfloat16
I32 = jnp.int32
U32 = jnp.uint32

CONV_KERNEL = 31
POOL_WINDOWS = (2, 4, 8, 16)
N_EXPERT_GROUPS = 4
LN_EPS = 1e-5

SUBLANES = 8
LANES = 128

SEQ_TILE = 1024
SEQ_SUB_TILE = 256
STAGE_LEAD = 4
EXPERT_TILE = 1024
EXPERT_F_SPLIT = 2
COMBINE_TILE = 1024
BATCH_CHUNKS = 2
CONV_HALO_VREGS = 32
POOL_HALO = 16
VMEM_LIMIT = 56 * 1024 * 1024
SC_CORES = 2
SC_SUBCORES = 16
SC_ROWS_PER_STEP = 64


def _split_bf16(a):
    hi = a.astype(BF16)
    lo = (a - hi.astype(F32)).astype(BF16)
    return hi, lo


def _dot(a, b):
    return jnp.dot(a, b, preferred_element_type=F32)


def _pack_halves(y):
    h = y.shape[1] // 2
    lo = lax.bitcast_convert_type(y[:, :h].astype(BF16).astype(F32), U32)
    hi = lax.bitcast_convert_type(y[:, h:].astype(BF16).astype(F32), U32)
    return (lo >> 16) | hi


def _unpack_halves(p):
    lo = lax.bitcast_convert_type(p << 16, F32)
    hi = lax.bitcast_convert_type(p & jnp.uint32(0xFFFF0000), F32)
    return lo, hi


def _ada_body(c_ref, w_ref, b_ref, o_ref):
    c = c_ref[...]
    ca = c * jax.nn.sigmoid(c)
    chi, clo = _split_bf16(ca)
    whi, wlo = _split_bf16(w_ref[0])
    o_ref[0] = _dot(chi, whi) + _dot(chi, wlo) + _dot(clo, whi) + b_ref[0]


def _ada_mod(c, w_ada, b_ada):
    n_layers, d, n6 = w_ada.shape
    bsz = c.shape[0]
    tn = n6 // 6
    return pl.pallas_call(
        _ada_body,
        grid=(n_layers, n6 // tn),
        in_specs=[
            pl.BlockSpec((bsz, d), lambda l, j: (0, 0)),
            pl.BlockSpec((1, d, tn), lambda l, j: (l, 0, j)),
            pl.BlockSpec((1, 1, tn), lambda l, j: (l, 0, j)),
        ],
        out_specs=pl.BlockSpec((1, bsz, tn), lambda l, j: (l, 0, j)),
        out_shape=jax.ShapeDtypeStruct((n_layers, bsz, n6), F32),
        compiler_params=pltpu.CompilerParams(dimension_semantics=("arbitrary", "arbitrary"), vmem_limit_bytes=VMEM_LIMIT),
        name="ada_mod",
    )(c, w_ada, b_ada.reshape(n_layers, 1, n6))


def _layer_norm_rows(z, g, b):
    mu = jnp.mean(z, axis=-1, keepdims=True)
    zc = z - mu
    var = jnp.mean(zc * zc, axis=-1, keepdims=True)
    return zc * lax.rsqrt(var + LN_EPS) * g + b


def _combine_rows(alpha, x1, y0, y1, wm, g2, ln_g, ln_b):
    lo0, hi0 = _unpack_halves(y0)
    lo1, hi1 = _unpack_halves(y1)
    w0, w1 = wm[:, 0:1], wm[:, 1:2]
    f = jnp.concatenate([w0 * lo0 + w1 * lo1, w0 * hi0 + w1 * hi1], axis=1)
    return _layer_norm_rows(alpha * x1 + (1.0 + g2) * f, ln_g, ln_b)


def _mixer_body(alpha, n_experts, sub, fused, *refs):
    if fused:
        xp_ref, y0_ref, y1_ref, wmp_ref, modp_ref, l2g_ref, l2b_ref = refs[:7]
        refs = refs[7:]
        x_ref = None
    else:
        x_ref = refs[0]
        refs = refs[1:]
    (mod_ref, win_ref, bin_ref, cw_ref, cb_ref, cg_ref, cbeta_ref, pw_ref, ps_ref,
     wout_ref, bout_ref, l1g_ref, l1b_ref, wr_ref, rb_ref, perm_ref, permt_ref, tri_ref,
     x1_ref, u2_ref, e_ref, rank_ref, wm_ref, cnt_ref,
     ebuf, cbuf, pbuf, vprev, phalo, carry, xbuf) = refs
    ts = x1_ref.shape[1]
    cw = cg_ref.shape[1]
    pw = ps_ref.shape[1]
    gw = pw // len(POOL_WINDOWS)
    fine = sub // SUBLANES
    halo = CONV_HALO_VREGS
    b = pl.program_id(0)
    i = pl.program_id(1)

    @pl.when(i == 0)
    def _():
        vprev[...] = jnp.zeros(vprev.shape, F32)
        phalo[...] = jnp.zeros(phalo.shape, F32)

    @pl.when(jnp.logical_and(b == 0, i == 0))
    def _():
        carry[...] = jnp.zeros(carry.shape, F32)

    mod = mod_ref[0]
    sh1, sc1, g1 = mod[0:1], mod[1:2], mod[2:3]
    sh2, sc2 = mod[3:4], mod[4:5]

    def stage1(s):
        r0 = s * sub
        if fused:
            x = _combine_rows(alpha, xp_ref[0, r0:r0 + sub, :], y0_ref[r0:r0 + sub, :], y1_ref[r0:r0 + sub, :],
                              wmp_ref[r0:r0 + sub, :], modp_ref[0][5:6], l2g_ref[...], l2b_ref[...])
            xbuf[r0:r0 + sub, :] = x
        else:
            x = x_ref[0, r0:r0 + sub, :]
        u = (x * (1.0 + sc1) + sh1).astype(BF16)
        h = _dot(u, win_ref[...]) + bin_ref[...]
        a = h[:, :cw]
        gate = h[:, cw:2 * cw]
        p = h[:, 2 * cw:]

        v = (a * jax.nn.sigmoid(gate)).astype(BF16)
        vb = _dot(perm_ref[...], v)
        ebuf[s, halo * SUBLANES:, :] = vb
        sl = lax.broadcasted_iota(I32, (SUBLANES, cw), 0)
        for mm in range(halo):
            src = (fine - halo + mm) * SUBLANES
            cur = vb[src:src + SUBLANES, :]
            prev = vprev[mm * SUBLANES:(mm + 1) * SUBLANES, :]
            ebuf[s, mm * SUBLANES:(mm + 1) * SUBLANES, :] = pltpu.roll(
                jnp.where(sl == SUBLANES - 1, prev, cur), 1, 0)
        vprev[...] = vb[(fine - halo) * SUBLANES:, :]
        pbuf[s, 0:POOL_HALO, :] = phalo[...]
        pbuf[s, POOL_HALO:, :] = p
        phalo[...] = p[sub - POOL_HALO:, :]

    def stage2(s):
        r0 = s * sub
        x = xbuf[r0:r0 + sub, :] if fused else x_ref[0, r0:r0 + sub, :]
        chunk = 8 * SUBLANES
        first = halo - (CONV_KERNEL - 1)
        for q0 in range(0, sub, chunk):
            for c0 in range(0, cw, LANES):
                acc = jnp.broadcast_to(cb_ref[:, c0:c0 + LANES], (chunk, LANES))
                for k in range(CONV_KERNEL):
                    off = q0 + (first + k) * SUBLANES
                    acc = acc + cw_ref[k:k + 1, c0:c0 + LANES] * ebuf[s, off:off + chunk, c0:c0 + LANES]
                cbuf[s, q0:q0 + chunk, c0:c0 + LANES] = acc
        yc = _layer_norm_rows(cbuf[s], cg_ref[...], cbeta_ref[...])
        ya_b = (yc * jax.nn.sigmoid(yc)).astype(BF16)
        ya = _dot(permt_ref[...], ya_b).astype(BF16)

        ext = pbuf[s]
        p = ext[POOL_HALO:, :]
        w_sum = ext + pltpu.roll(ext, 1, 0)
        parts = [w_sum[:, :gw]]
        shift = 2
        for _ in range(len(POOL_WINDOWS) - 1):
            w_sum = w_sum[:, gw:]
            w_sum = w_sum + pltpu.roll(w_sum, shift, 0)
            parts.append(w_sum[:, :gw])
            shift *= 2
        wsum = jnp.concatenate(parts, axis=1)[POOL_HALO:, :]
        pos = (lax.broadcasted_iota(I32, (sub, gw), 0) + (i * ts + r0 + 1)).astype(F32)
        cnt = jnp.concatenate([jnp.minimum(pos, float(wlen)) for wlen in POOL_WINDOWS], axis=1)
        dpool = (wsum / cnt - p).astype(BF16)
        yb_parts = [_dot(dpool[:, gi * gw:(gi + 1) * gw], pw_ref[gi]) for gi in range(len(POOL_WINDOWS))]
        yb = (jnp.concatenate(yb_parts, axis=1) * ps_ref[...]).astype(BF16)

        mix = _dot(ya, wout_ref[0:cw, :]) + _dot(yb, wout_ref[cw:, :]) + bout_ref[...]
        x1 = _layer_norm_rows(alpha * x + (1.0 + g1) * mix, l1g_ref[...], l1b_ref[...])
        x1_ref[0, r0:r0 + sub, :] = x1
        u2 = x1 * (1.0 + sc2) + sh2
        u2_ref[0, r0:r0 + sub, :] = _pack_halves(u2)

        logits = _dot(u2.astype(BF16), wr_ref[...]).T[0:n_experts, :]
        scores = jax.nn.sigmoid(logits)
        sel = scores + rb_ref[...]
        epg = n_experts // N_EXPERT_GROUPS
        io = lax.broadcasted_iota(I32, (epg, sub), 0).astype(F32)
        neg = jnp.float32(-jnp.inf)
        best = None
        for g in range(N_EXPERT_GROUPS):
            sg = sel[g * epg:(g + 1) * epg, :]
            scg = scores[g * epg:(g + 1) * epg, :]
            m1 = jnp.max(sg, axis=0, keepdims=True)
            i1 = jnp.min(jnp.where(sg == m1, io, float(epg)), axis=0, keepdims=True)
            rest = jnp.where(io == i1, neg, sg)
            m2 = jnp.max(rest, axis=0, keepdims=True)
            i2 = jnp.min(jnp.where(jnp.logical_and(rest == m2, io != i1), io, float(epg)), axis=0, keepdims=True)
            s1 = jnp.sum(jnp.where(io == i1, scg, 0.0), axis=0, keepdims=True)
            s2 = jnp.sum(jnp.where(io == i2, scg, 0.0), axis=0, keepdims=True)
            gs = m1 + m2
            cand = (gs, i1 + float(g * epg), i2 + float(g * epg), s1, s2)
            if best is None:
                best = cand
            else:
                upd = gs > best[0]
                best = tuple(jnp.where(upd, cn, bs) for cn, bs in zip(cand, best))
        _, e1, e2, s1, s2 = best
        denom = s1 + s2
        w1 = s1 / denom
        w2 = s2 / denom

        ioe = lax.broadcasted_iota(I32, (n_experts, sub), 0).astype(F32)
        hit1 = ioe == e1
        hit2 = ioe == e2
        onehot = jnp.where(jnp.logical_or(hit1, hit2), 1.0, 0.0)
        before = _dot(onehot.astype(BF16), tri_ref[...]) + carry[:, 0:1]
        r1 = jnp.sum(jnp.where(hit1, before, 0.0), axis=0, keepdims=True)
        r2 = jnp.sum(jnp.where(hit2, before, 0.0), axis=0, keepdims=True)
        carry[...] = carry[...] + jnp.sum(onehot, axis=1, keepdims=True)

        e_ref[0:1, r0:r0 + sub] = e1.astype(I32)
        e_ref[1:2, r0:r0 + sub] = e2.astype(I32)
        rank_ref[0:1, r0:r0 + sub] = r1.astype(I32)
        rank_ref[1:2, r0:r0 + sub] = r2.astype(I32)
        rowi = lax.broadcasted_iota(I32, (LANES, sub), 0)
        wpad = jnp.where(rowi == 0, w1, jnp.where(rowi == 1, w2, 0.0))
        wm_ref[r0:r0 + sub, :] = wpad.T

    nsub = ts // sub
    for s in range(min(STAGE_LEAD, nsub)):
        stage1(s)
    for s in range(nsub):
        if s + STAGE_LEAD < nsub:
            stage1(s + STAGE_LEAD)
        stage2(s)
    cnt_ref[...] = carry[...]


def _row_perm(n):
    r = jnp.arange(n)
    t = (n // SUBLANES) * (r % SUBLANES) + r // SUBLANES
    return (t[:, None] == jnp.arange(n)[None, :]).astype(BF16)


def _mixer(x, mod, lw, wr, rbias, alpha, bsz, x_b0, mod_b0, prev=None):
    _, seq, d = x.shape
    ts = min(SEQ_TILE, seq)
    sub = min(SEQ_SUB_TILE, ts)
    nsub = ts // sub
    ns = seq // ts
    n_tok = bsz * seq
    n_experts = rbias.shape[0]
    cw = lw["conv_w"].shape[1]
    pw = lw["pool_scale"].shape[0]
    perm = _row_perm(sub)
    tri = (jnp.arange(sub)[:, None] < jnp.arange(sub)[None, :]).astype(BF16)
    row = lambda a: a.reshape(1, -1)
    full = lambda shape: pl.BlockSpec(shape, lambda b, i: (0,) * len(shape))
    if prev is None:
        ins = [(x, pl.BlockSpec((1, ts, d), lambda b, i: (b + x_b0, i, 0)))]
    else:
        yy, wmp, modp, l2g, l2b = prev
        nblk = n_tok // ts
        ins = [
            (x, pl.BlockSpec((1, ts, d), lambda b, i: (b + x_b0, i, 0))),
            (yy, pl.BlockSpec((ts, d // 2), lambda b, i: (b * ns + i, 0))),
            (yy, pl.BlockSpec((ts, d // 2), lambda b, i: (nblk + b * ns + i, 0))),
            (wmp, pl.BlockSpec((ts, LANES), lambda b, i: (b * ns + i, 0))),
            (modp, pl.BlockSpec((1, 6, d), lambda b, i: (b + mod_b0, 0, 0))),
            (row(l2g), None), (row(l2b), None),
        ]
    ins += [
        (mod, pl.BlockSpec((1, 6, d), lambda b, i: (b + mod_b0, 0, 0))),
        (lw["w_in"], None), (row(lw["b_in"]), None),
        (jnp.pad(lw["conv_w"], ((0, 1), (0, 0))), None), (row(lw["conv_b"]), None),
        (row(lw["conv_ln_g"]), None), (row(lw["conv_ln_b"]), None),
        (lw["pool_w"], None), (row(lw["pool_scale"]), None),
        (lw["w_out"], None), (row(lw["b_out"]), None),
        (row(lw["ln1_g"]), None), (row(lw["ln1_b"]), None),
        (wr, None), (rbias.reshape(-1, 1), None),
        (perm, None), (perm.T, None), (tri, None),
    ]
    args = [a for a, _ in ins]
    specs = [s if s is not None else full(a.shape) for a, s in ins]
    out_shape = (
        jax.ShapeDtypeStruct((bsz, seq, d), F32),
        jax.ShapeDtypeStruct((bsz, seq, d // 2), U32),
        jax.ShapeDtypeStruct((2, n_tok), I32),
        jax.ShapeDtypeStruct((2, n_tok), I32),
        jax.ShapeDtypeStruct((n_tok, LANES), F32),
        jax.ShapeDtypeStruct((n_experts, LANES), F32),
    )
    out_specs = (
        pl.BlockSpec((1, ts, d), lambda b, i: (b, i, 0)),
        pl.BlockSpec((1, ts, d // 2), lambda b, i: (b, i, 0)),
        pl.BlockSpec((2, ts), lambda b, i: (0, b * ns + i)),
        pl.BlockSpec((2, ts), lambda b, i: (0, b * ns + i)),
        pl.BlockSpec((ts, LANES), lambda b, i: (b * ns + i, 0)),
        pl.BlockSpec((n_experts, LANES), lambda b, i: (0, 0)),
    )
    return pl.pallas_call(
        functools.partial(_mixer_body, alpha, n_experts, sub, prev is not None),
        grid=(bsz, ns),
        in_specs=specs,
        out_specs=out_specs,
        out_shape=out_shape,
        scratch_shapes=[
            pltpu.VMEM((nsub, CONV_HALO_VREGS * SUBLANES + sub, cw), F32),
            pltpu.VMEM((nsub, sub, cw), F32),
            pltpu.VMEM((nsub, POOL_HALO + sub, pw), F32),
            pltpu.VMEM((CONV_HALO_VREGS * SUBLANES, cw), F32),
            pltpu.VMEM((POOL_HALO, pw), F32),
            pltpu.VMEM((n_experts, LANES), F32),
            pltpu.VMEM((ts, d) if prev is not None else (SUBLANES, LANES), F32),
        ],
        compiler_params=pltpu.CompilerParams(dimension_semantics=("arbitrary", "arbitrary"), vmem_limit_bytes=VMEM_LIMIT),
        name="mixer_router",
    )(*args)


def _dest_body(n_experts, pstart_ref, e_ref, rank_ref, dest_ref):
    e = e_ref[...]
    dest = rank_ref[...]
    for ex in range(n_experts):
        dest = dest + jnp.where(e == ex, pstart_ref[ex], 0)
    dest_ref[...] = dest


def _dest_rows(pstart, eidx, rank):
    n_tok = eidx.shape[1]
    tn = min(8192, n_tok)
    grid_spec = pltpu.PrefetchScalarGridSpec(
        num_scalar_prefetch=1,
        grid=(n_tok // tn,),
        in_specs=[pl.BlockSpec((2, tn), lambda i, ps: (0, i)), pl.BlockSpec((2, tn), lambda i, ps: (0, i))],
        out_specs=pl.BlockSpec((2, tn), lambda i, ps: (0, i)),
    )
    return pl.pallas_call(
        functools.partial(_dest_body, pstart.shape[0]),
        grid_spec=grid_spec,
        out_shape=jax.ShapeDtypeStruct((2, n_tok), I32),
        compiler_params=pltpu.CompilerParams(dimension_semantics=("arbitrary",)),
        name="dest_rows",
    )(pstart, eidx, rank)


def _sc_mesh():
    return plsc.VectorSubcoreMesh(core_axis_name="c", subcore_axis_name="s",
                                  num_cores=SC_CORES, num_subcores=SC_SUBCORES)


def _dispatch(u2, d0, d1, n_rows):
    n_tok, d = u2.shape
    workers = SC_CORES * SC_SUBCORES
    per_w = n_tok // workers
    ch = SC_ROWS_PER_STEP
    nch = per_w // ch

    @functools.partial(
        pl.kernel, mesh=_sc_mesh(),
        out_type=jax.ShapeDtypeStruct((n_rows, d), u2.dtype),
        scratch_types=[pltpu.VMEM((nch, ch), I32), pltpu.VMEM((nch, ch), I32), pltpu.VMEM((2, ch, d), u2.dtype),
                       pltpu.SemaphoreType.DMA((2,)), pltpu.SemaphoreType.DMA((2,))],
        name="sc_dispatch",
    )
    def run(u_hbm, d0_hbm, d1_hbm, xs_hbm, i0, i1, rows, rsem, ssem):
        wid = lax.axis_index("s") * SC_CORES + lax.axis_index("c")
        base = wid * per_w
        pltpu.sync_copy(d0_hbm.at[pl.ds(wid * nch, nch)], i0)
        pltpu.sync_copy(d1_hbm.at[pl.ds(wid * nch, nch)], i1)

        def read(j, b):
            return pltpu.make_async_copy(u_hbm.at[pl.ds(base + j * ch, ch)], rows.at[b], rsem.at[b])

        def scatter(idx, j, b):
            return pltpu.make_async_copy(rows.at[b], xs_hbm.at[idx.at[j]], ssem.at[b])

        read(0, 0).start()

        @pl.loop(0, nch, step=2)
        def _(j0):
            for b in range(2):
                j = j0 + b
                read(j, b).wait()
                scatter(i0, j, b).start()
                scatter(i1, j, b).start()

                @pl.when(j >= 1)
                def _():
                    scatter(i0, j - 1, 1 - b).wait()
                    scatter(i1, j - 1, 1 - b).wait()

                @pl.when(j + 1 < nch)
                def _():
                    read(j + 1, 1 - b).start()

        scatter(i0, nch - 1, 1).wait()
        scatter(i1, nch - 1, 1).wait()

    return run(u2, d0.reshape(n_tok // ch, ch), d1.reshape(n_tok // ch, ch))


def _gather_rows(ys, dd):
    n_out = dd.shape[0]
    d = ys.shape[1]
    workers = SC_CORES * SC_SUBCORES
    per_w = n_out // workers
    ch = SC_ROWS_PER_STEP
    nch = per_w // ch

    @functools.partial(
        pl.kernel, mesh=_sc_mesh(),
        out_type=jax.ShapeDtypeStruct((n_out, d), ys.dtype),
        scratch_types=[pltpu.VMEM((nch, ch), I32), pltpu.VMEM((2, ch, d), ys.dtype),
                       pltpu.SemaphoreType.DMA((2,)), pltpu.SemaphoreType.DMA((2,))],
        name="sc_gather",
    )
    def run(ys_hbm, dd_hbm, out_hbm, idx, rows, gsem, wsem):
        wid = lax.axis_index("s") * SC_CORES + lax.axis_index("c")
        base = wid * per_w
        pltpu.sync_copy(dd_hbm.at[pl.ds(wid * nch, nch)], idx)

        def gather(j, b):
            return pltpu.make_async_copy(ys_hbm.at[idx.at[j]], rows.at[b], gsem.at[b])

        def write(j, b):
            return pltpu.make_async_copy(rows.at[b], out_hbm.at[pl.ds(base + j * ch, ch)], wsem.at[b])

        gather(0, 0).start()

        @pl.loop(0, nch, step=2)
        def _(j0):
            for b in range(2):
                j = j0 + b
                gather(j, b).wait()
                write(j, b).start()

                @pl.when(j >= 1)
                def _():
                    write(j - 1, 1 - b).wait()

                @pl.when(j + 1 < nch)
                def _():
                    gather(j + 1, 1 - b).start()

        write(nch - 1, 1).wait()

    return run(ys, dd.reshape(n_out // ch, ch))


def _expert_body(blk_ref, nused_ref, xs_ref, wg_ref, wu_ref, wd_ref, ys_ref, wgb, wub, wdb):
    j = pl.program_id(0)
    used = j < nused_ref[0]
    new_expert = jnp.logical_or(j == 0, blk_ref[j] != blk_ref[jnp.maximum(j - 1, 0)])

    @pl.when(jnp.logical_and(used, new_expert))
    def _():
        wgb[...] = wg_ref[0, 0].astype(BF16)
        wub[...] = wu_ref[0, 0].astype(BF16)
        wdb[...] = wd_ref[0, 0].astype(BF16)

    @pl.when(used)
    def _():
        lo, hi = _unpack_halves(xs_ref[...])
        lo = lo.astype(BF16)
        hi = hi.astype(BF16)
        half = lo.shape[1]
        fw = wgb.shape[1] // EXPERT_F_SPLIT
        acc = None
        for c in range(EXPERT_F_SPLIT):
            cs = slice(c * fw, (c + 1) * fw)
            g = _dot(lo, wgb[0:half, cs]) + _dot(hi, wgb[half:, cs])
            up = _dot(lo, wub[0:half, cs]) + _dot(hi, wub[half:, cs])
            hid = (g * jax.nn.sigmoid(g) * up).astype(BF16)
            part = _dot(hid, wdb[cs, :])
            acc = part if acc is None else acc + part
        ys_ref[...] = _pack_halves(acc)

    @pl.when(j >= nused_ref[0])
    def _():
        ys_ref[...] = jnp.zeros(ys_ref.shape, U32)


def _experts(blk_e, nused, xs, wg, wu, wd, layer):
    n_rows = xs.shape[0]
    d = wg.shape[2]
    f = wg.shape[3]
    tm = EXPERT_TILE
    grid_spec = pltpu.PrefetchScalarGridSpec(
        num_scalar_prefetch=2,
        grid=(n_rows // tm,),
        in_specs=[
            pl.BlockSpec((tm, d // 2), lambda j, be, nu: (jnp.minimum(j, nu[0] - 1), 0)),
            pl.BlockSpec((1, 1, d, f), lambda j, be, nu: (layer, be[j], 0, 0)),
            pl.BlockSpec((1, 1, d, f), lambda j, be, nu: (layer, be[j], 0, 0)),
            pl.BlockSpec((1, 1, f, d), lambda j, be, nu: (layer, be[j], 0, 0)),
        ],
        out_specs=pl.BlockSpec((tm, d // 2), lambda j, be, nu: (j, 0)),
        scratch_shapes=[pltpu.VMEM((d, f), BF16), pltpu.VMEM((d, f), BF16), pltpu.VMEM((f, d), BF16)],
    )
    return pl.pallas_call(
        _expert_body,
        grid_spec=grid_spec,
        out_shape=jax.ShapeDtypeStruct((n_rows, d // 2), U32),
        compiler_params=pltpu.CompilerParams(dimension_semantics=("arbitrary",), vmem_limit_bytes=VMEM_LIMIT),
        name="expert_ffn",
    )(blk_e, nused, xs, wg, wu, wd)


def _combine_body(alpha, x1_ref, y0_ref, y1_ref, mod_ref, wm_ref, g_ref, b_ref, *rest):
    o_ref = rest[-1]
    o_ref[...] = _combine_rows(alpha, x1_ref[...], y0_ref[...], y1_ref[...], wm_ref[...],
                               mod_ref[0][5:6], g_ref[...], b_ref[...])


def _combine(x1, yy, mod, wm, ln_g, ln_b, seq, alpha, mod_b0, out_rows, out_row0, out_prev):
    n_tok, d = x1.shape
    tn = min(COMBINE_TILE, seq)
    per_seq = seq // tn
    nblk = n_tok // tn
    blk0 = out_row0 // tn
    rows = lambda: pl.BlockSpec((tn, d), lambda i: (i, 0))
    in_specs = [rows(), pl.BlockSpec((tn, d // 2), lambda i: (i, 0)),
                pl.BlockSpec((tn, d // 2), lambda i: (nblk + i, 0)),
                pl.BlockSpec((1, 6, d), lambda i: (i // per_seq + mod_b0, 0, 0)),
                pl.BlockSpec((tn, LANES), lambda i: (i, 0)),
                pl.BlockSpec((1, d), lambda i: (0, 0)),
                pl.BlockSpec((1, d), lambda i: (0, 0))]
    args = [x1, yy, yy, mod, wm, ln_g.reshape(1, d), ln_b.reshape(1, d)]
    aliases = {}
    if out_prev is not None:
        in_specs.append(pl.BlockSpec(memory_space=pl.ANY))
        args.append(out_prev)
        aliases = {len(args) - 1: 0}
    return pl.pallas_call(
        functools.partial(_combine_body, alpha),
        grid=(nblk,),
        in_specs=in_specs,
        out_specs=pl.BlockSpec((tn, d), lambda i: (blk0 + i, 0)),
        out_shape=jax.ShapeDtypeStruct((out_rows, d), F32),
        input_output_aliases=aliases,
        compiler_params=pltpu.CompilerParams(dimension_semantics=("arbitrary",), vmem_limit_bytes=VMEM_LIMIT),
        name="combine_ln",
    )(*args)


def kernel(x, c, w_ada, b_ada, w_in, b_in, conv_w, conv_b, conv_ln_g, conv_ln_b, pool_w, pool_scale, w_out, b_out, ln1_g, ln1_b, w_router, router_bias, w_gate, w_up, w_down, ln2_g, ln2_b):
    bsz, seq, d = x.shape
    depth = w_ada.shape[0]
    n_experts = w_router.shape[1]
    alpha = float((2 * depth) ** 0.25)
    tm = EXPERT_TILE
    n_chunks = BATCH_CHUNKS if bsz % BATCH_CHUNKS == 0 else 1
    bc = bsz // n_chunks
    n_tok = bc * seq
    n_rows = 2 * n_tok + n_experts * tm

    mod_all = _ada_mod(c, w_ada, b_ada).reshape(depth, bsz, 6, d)
    wr = jnp.pad(w_router.astype(BF16), ((0, 0), (0, LANES - n_experts)))

    chunks = [x] * n_chunks
    starts = [ci * bc for ci in range(n_chunks)]
    prevs = [None] * n_chunks
    out = None
    for l in range(depth):
        lw = dict(w_in=w_in[l].astype(BF16), b_in=b_in[l], conv_w=conv_w[l], conv_b=conv_b[l],
                  conv_ln_g=conv_ln_g[l], conv_ln_b=conv_ln_b[l], pool_w=pool_w[l].astype(BF16),
                  pool_scale=pool_scale[l], w_out=w_out[l].astype(BF16), b_out=b_out[l],
                  ln1_g=ln1_g[l], ln1_b=ln1_b[l])
        mod = mod_all[l]
        last = l == depth - 1
        st = []
        for ci in range(n_chunks):
            x1, u2, eidx, rank, wm, cnt = _mixer(chunks[ci], mod, lw, wr, router_bias, alpha,
                                                 bc, starts[ci], ci * bc, prevs[ci])
            counts = cnt[:, 0].astype(I32)
            tiles = (counts + tm - 1) // tm
            tile_end = jnp.cumsum(tiles)
            pstart = ((tile_end - tiles) * tm).astype(I32)
            nused = tile_end[-1:].astype(I32)
            blk_e = jnp.minimum(
                jnp.sum(tile_end[None, :] <= jnp.arange(n_rows // tm, dtype=I32)[:, None], axis=1), n_experts - 1
            ).astype(I32)
            dest = _dest_rows(pstart, eidx, rank)
            xs = _dispatch(u2.reshape(n_tok, d // 2), dest[0], dest[1], n_rows)
            st.append((x1, wm, blk_e, nused, dest, xs))
        ys = [_experts(blk_e, nused, xs, w_gate, w_up, w_down, l) for (_, _, blk_e, nused, _, xs) in st]
        yy = [_gather_rows(ys[ci], st[ci][4].reshape(2 * n_tok)) for ci in range(n_chunks)]
        for ci in range(n_chunks):
            x1, wm = st[ci][0], st[ci][1]
            if last:
                out = _combine(x1.reshape(n_tok, d), yy[ci], mod, wm, ln2_g[l], ln2_b[l], seq, alpha,
                               ci * bc, bsz * seq, ci * n_tok, out)
            else:
                chunks[ci] = x1
                starts[ci] = 0
                prevs[ci] = (yy[ci], wm, mod, ln2_g[l], ln2_b[l])
    return out.reshape(bsz, seq, d)
```

```python
import functools

import jax
import jax.numpy as jnp
from jax import lax
from jax.experimental import pallas as pl
from jax.experimental.pallas import tpu as pltpu
from jax.experimental.pallas import tpu_sc as plsc

F32 = jnp.float32
BF16 = jnp.bfloat16
I32 = jnp.int32
U32 = jnp.uint32

CONV_KERNEL = 31
POOL_WINDOWS = (2, 4, 8, 16)
N_EXPERT_GROUPS = 4
LN_EPS = 1e-5

SUBLANES = 8
LANES = 128

SEQ_TILE = 1024
SEQ_SUB_TILE = 256
STAGE_LEAD = 4
EXPERT_TILE = 1024
EXPERT_F_SPLIT = 2
COMBINE_TILE = 1024
BATCH_CHUNKS = 2
CONV_HALO_VREGS = 32
POOL_HALO = 16
VMEM_LIMIT = 56 * 1024 * 1024
SC_CORES = 2
SC_SUBCORES = 16
SC_ROWS_PER_STEP = 64


def _split_bf16(a):
    hi = a.astype(BF16)
    lo = (a - hi.astype(F32)).astype(BF16)
    return hi, lo


def _dot(a, b):
    return jnp.dot(a, b, preferred_element_type=F32)


def _pack_halves(y):
    h = y.shape[1] // 2
    lo = lax.bitcast_convert_type(y[:, :h].astype(BF16).astype(F32), U32)
    hi = lax.bitcast_convert_type(y[:, h:].astype(BF16).astype(F32), U32)
    return (lo >> 16) | hi


def _unpack_halves(p):
    lo = lax.bitcast_convert_type(p << 16, F32)
    hi = lax.bitcast_convert_type(p & jnp.uint32(0xFFFF0000), F32)
    return lo, hi


def _ada_body(c_ref, w_ref, b_ref, o_ref):
    c = c_ref[...]
    ca = c * jax.nn.sigmoid(c)
    chi, clo = _split_bf16(ca)
    whi, wlo = _split_bf16(w_ref[0])
    o_ref[0] = _dot(chi, whi) + _dot(chi, wlo) + _dot(clo, whi) + b_ref[0]


def _ada_mod(c, w_ada, b_ada):
    n_layers, d, n6 = w_ada.shape
    bsz = c.shape[0]
    tn = n6 // 6
    return pl.pallas_call(
        _ada_body,
        grid=(n_layers, n6 // tn),
        in_specs=[
            pl.BlockSpec((bsz, d), lambda l, j: (0, 0)),
            pl.BlockSpec((1, d, tn), lambda l, j: (l, 0, j)),
            pl.BlockSpec((1, 1, tn), lambda l, j: (l, 0, j)),
        ],
        out_specs=pl.BlockSpec((1, bsz, tn), lambda l, j: (l, 0, j)),
        out_shape=jax.ShapeDtypeStruct((n_layers, bsz, n6), F32),
        compiler_params=pltpu.CompilerParams(dimension_semantics=("arbitrary", "arbitrary"), vmem_limit_bytes=VMEM_LIMIT),
        name="ada_mod",
    )(c, w_ada, b_ada.reshape(n_layers, 1, n6))


def _layer_norm_rows(z, g, b):
    mu = jnp.mean(z, axis=-1, keepdims=True)
    zc = z - mu
    var = jnp.mean(zc * zc, axis=-1, keepdims=True)
    return zc * lax.rsqrt(var + LN_EPS) * g + b


def _combine_rows(alpha, x1, y0, y1, wm, g2, ln_g, ln_b):
    lo0, hi0 = _unpack_halves(y0)
    lo1, hi1 = _unpack_halves(y1)
    w0, w1 = wm[:, 0:1], wm[:, 1:2]
    f = jnp.concatenate([w0 * lo0 + w1 * lo1, w0 * hi0 + w1 * hi1], axis=1)
    return _layer_norm_rows(alpha * x1 + (1.0 + g2) * f, ln_g, ln_b)


def _mixer_body(alpha, n_experts, sub, fused, *refs):
    if fused:
        xp_ref, y0_ref, y1_ref, wmp_ref, modp_ref, l2g_ref, l2b_ref = refs[:7]
        refs = refs[7:]
        x_ref = None
    else:
        x_ref = refs[0]
        refs = refs[1:]
    (mod_ref, win_ref, bin_ref, cw_ref, cb_ref, cg_ref, cbeta_ref, pw_ref, ps_ref,
     wout_ref, bout_ref, l1g_ref, l1b_ref, wr_ref, rb_ref, perm_ref, permt_ref, tri_ref,
     x1_ref, u2_ref, e_ref, rank_ref, wm_ref, cnt_ref,
     ebuf, cbuf, pbuf, vprev, phalo, carry, xbuf) = refs
    ts = x1_ref.shape[1]
    cw = cg_ref.shape[1]
    pw = ps_ref.shape[1]
    gw = pw // len(POOL_WINDOWS)
    fine = sub // SUBLANES
    halo = CONV_HALO_VREGS
    b = pl.program_id(0)
    i = pl.program_id(1)

    @pl.when(i == 0)
    def _():
        vprev[...] = jnp.zeros(vprev.shape, F32)
        phalo[...] = jnp.zeros(phalo.shape, F32)

    @pl.when(jnp.logical_and(b == 0, i == 0))
    def _():
        carry[...] = jnp.zeros(carry.shape, F32)

    mod = mod_ref[0]
    sh1, sc1, g1 = mod[0:1], mod[1:2], mod[2:3]
    sh2, sc2 = mod[3:4], mod[4:5]

    def stage1(s):
        r0 = s * sub
        if fused:
            x = _combine_rows(alpha, xp_ref[0, r0:r0 + sub, :], y0_ref[r0:r0 + sub, :], y1_ref[r0:r0 + sub, :],
                              wmp_ref[r0:r0 + sub, :], modp_ref[0][5:6], l2g_ref[...], l2b_ref[...])
            xbuf[r0:r0 + sub, :] = x
        else:
            x = x_ref[0, r0:r0 + sub, :]
        u = (x * (1.0 + sc1) + sh1).astype(BF16)
        h = _dot(u, win_ref[...]) + bin_ref[...]
        a = h[:, :cw]
        gate = h[:, cw:2 * cw]
        p = h[:, 2 * cw:]

        v = (a * jax.nn.sigmoid(gate)).astype(BF16)
        vb = _dot(perm_ref[...], v)
        ebuf[s, halo * SUBLANES:, :] = vb
        sl = lax.broadcasted_iota(I32, (SUBLANES, cw), 0)
        for mm in range(halo):
            src = (fine - halo + mm) * SUBLANES
            cur = vb[src:src + SUBLANES, :]
            prev = vprev[mm * SUBLANES:(mm + 1) * SUBLANES, :]
            ebuf[s, mm * SUBLANES:(mm + 1) * SUBLANES, :] = pltpu.roll(
                jnp.where(sl == SUBLANES - 1, prev, cur), 1, 0)
        vprev[...] = vb[(fine - halo) * SUBLANES:, :]
        pbuf[s, 0:POOL_HALO, :] = phalo[...]
        pbuf[s, POOL_HALO:, :] = p
        phalo[...] = p[sub - POOL_HALO:, :]

    def stage2(s):
        r0 = s * sub
        x = xbuf[r0:r0 + sub, :] if fused else x_ref[0, r0:r0 + sub, :]
        chunk = 8 * SUBLANES
        first = halo - (CONV_KERNEL - 1)
        for q0 in range(0, sub, chunk):
            for c0 in range(0, cw, LANES):
                acc = jnp.broadcast_to(cb_ref[:, c0:c0 + LANES], (chunk, LANES))
                for k in range(CONV_KERNEL):
                    off = q0 + (first + k) * SUBLANES
                    acc = acc + cw_ref[k:k + 1, c0:c0 + LANES] * ebuf[s, off:off + chunk, c0:c0 + LANES]
                cbuf[s, q0:q0 + chunk, c0:c0 + LANES] = acc
        yc = _layer_norm_rows(cbuf[s], cg_ref[...], cbeta_ref[...])
        ya_b = (yc * jax.nn.sigmoid(yc)).astype(BF16)
        ya = _dot(permt_ref[...], ya_b).astype(BF16)

        ext = pbuf[s]
        p = ext[POOL_HALO:, :]
        w_sum = ext + pltpu.roll(ext, 1, 0)
        parts = [w_sum[:, :gw]]
        shift = 2
        for _ in range(len(POOL_WINDOWS) - 1):
            w_sum = w_sum[:, gw:]
            w_sum = w_sum + pltpu.roll(w_sum, shift, 0)
            parts.append(w_sum[:, :gw])
            shift *= 2
        wsum = jnp.concatenate(parts, axis=1)[POOL_HALO:, :]
        pos = (lax.broadcasted_iota(I32, (sub, gw), 0) + (i * ts + r0 + 1)).astype(F32)
        cnt = jnp.concatenate([jnp.minimum(pos, float(wlen)) for wlen in POOL_WINDOWS], axis=1)
        dpool = (wsum / cnt - p).astype(BF16)
        yb_parts = [_dot(dpool[:, gi * gw:(gi + 1) * gw], pw_ref[gi]) for gi in range(len(POOL_WINDOWS))]
        yb = (jnp.concatenate(yb_parts, axis=1) * ps_ref[...]).astype(BF16)

        mix = _dot(ya, wout_ref[0:cw, :]) + _dot(yb, wout_ref[cw:, :]) + bout_ref[...]
        x1 = _layer_norm_rows(alpha * x + (1.0 + g1) * mix, l1g_ref[...], l1b_ref[...])
        x1_ref[0, r0:r0 + sub, :] = x1
        u2 = x1 * (1.0 + sc2) + sh2
        u2_ref[0, r0:r0 + sub, :] = _pack_halves(u2)

        logits = _dot(u2.astype(BF16), wr_ref[...]).T[0:n_experts, :]
        scores = jax.nn.sigmoid(logits)
        sel = scores + rb_ref[...]
        epg = n_experts // N_EXPERT_GROUPS
        io = lax.broadcasted_iota(I32, (epg, sub), 0).astype(F32)
        neg = jnp.float32(-jnp.inf)
        best = None
        for g in range(N_EXPERT_GROUPS):
            sg = sel[g * epg:(g + 1) * epg, :]
            scg = scores[g * epg:(g + 1) * epg, :]
            m1 = jnp.max(sg, axis=0, keepdims=True)
            i1 = jnp.min(jnp.where(sg == m1, io, float(epg)), axis=0, keepdims=True)
            rest = jnp.where(io == i1, neg, sg)
            m2 = jnp.max(rest, axis=0, keepdims=True)
            i2 = jnp.min(jnp.where(jnp.logical_and(rest == m2, io != i1), io, float(epg)), axis=0, keepdims=True)
            s1 = jnp.sum(jnp.where(io == i1, scg, 0.0), axis=0, keepdims=True)
            s2 = jnp.sum(jnp.where(io == i2, scg, 0.0), axis=0, keepdims=True)
            gs = m1 + m2
            cand = (gs, i1 + float(g * epg), i2 + float(g * epg), s1, s2)
            if best is None:
                best = cand
            else:
                upd = gs > best[0]
                best = tuple(jnp.where(upd, cn, bs) for cn, bs in zip(cand, best))
        _, e1, e2, s1, s2 = best
        denom = s1 + s2
        w1 = s1 / denom
        w2 = s2 / denom

        ioe = lax.broadcasted_iota(I32, (n_experts, sub), 0).astype(F32)
        hit1 = ioe == e1
        hit2 = ioe == e2
        onehot = jnp.where(jnp.logical_or(hit1, hit2), 1.0, 0.0)
        before = _dot(onehot.astype(BF16), tri_ref[...]) + carry[:, 0:1]
        r1 = jnp.sum(jnp.where(hit1, before, 0.0), axis=0, keepdims=True)
        r2 = jnp.sum(jnp.where(hit2, before, 0.0), axis=0, keepdims=True)
        carry[...] = carry[...] + jnp.sum(onehot, axis=1, keepdims=True)

        e_ref[0:1, r0:r0 + sub] = e1.astype(I32)
        e_ref[1:2, r0:r0 + sub] = e2.astype(I32)
        rank_ref[0:1, r0:r0 + sub] = r1.astype(I32)
        rank_ref[1:2, r0:r0 + sub] = r2.astype(I32)
        rowi = lax.broadcasted_iota(I32, (LANES, sub), 0)
        wpad = jnp.where(rowi == 0, w1, jnp.where(rowi == 1, w2, 0.0))
        wm_ref[r0:r0 + sub, :] = wpad.T

    nsub = ts // sub
    for s in range(min(STAGE_LEAD, nsub)):
        stage1(s)
    for s in range(nsub):
        if s + STAGE_LEAD < nsub:
            stage1(s + STAGE_LEAD)
        stage2(s)
    cnt_ref[...] = carry[...]


def _row_perm(n):
    r = jnp.arange(n)
    t = (n // SUBLANES) * (r % SUBLANES) + r // SUBLANES
    return (t[:, None] == jnp.arange(n)[None, :]).astype(BF16)


def _mixer(x, mod, lw, wr, rbias, alpha, bsz, x_b0, mod_b0, prev=None):
    _, seq, d = x.shape
    ts = min(SEQ_TILE, seq)
    sub = min(SEQ_SUB_TILE, ts)
    nsub = ts // sub
    ns = seq // ts
    n_tok = bsz * seq
    n_experts = rbias.shape[0]
    cw = lw["conv_w"].shape[1]
    pw = lw["pool_scale"].shape[0]
    perm = _row_perm(sub)
    tri = (jnp.arange(sub)[:, None] < jnp.arange(sub)[None, :]).astype(BF16)
    row = lambda a: a.reshape(1, -1)
    full = lambda shape: pl.BlockSpec(shape, lambda b, i: (0,) * len(shape))
    if prev is None:
        ins = [(x, pl.BlockSpec((1, ts, d), lambda b, i: (b + x_b0, i, 0)))]
    else:
        yy, wmp, modp, l2g, l2b = prev
        nblk = n_tok // ts
        ins = [
            (x, pl.BlockSpec((1, ts, d), lambda b, i: (b + x_b0, i, 0))),
            (yy, pl.BlockSpec((ts, d // 2), lambda b, i: (b * ns + i, 0))),
            (yy, pl.BlockSpec((ts, d // 2), lambda b, i: (nblk + b * ns + i, 0))),
            (wmp, pl.BlockSpec((ts, LANES), lambda b, i: (b * ns + i, 0))),
            (modp, pl.BlockSpec((1, 6, d), lambda b, i: (b + mod_b0, 0, 0))),
            (row(l2g), None), (row(l2b), None),
        ]
    ins += [
        (mod, pl.BlockSpec((1, 6, d), lambda b, i: (b + mod_b0, 0, 0))),
        (lw["w_in"], None), (row(lw["b_in"]), None),
        (jnp.pad(lw["conv_w"], ((0, 1), (0, 0))), None), (row(lw["conv_b"]), None),
        (row(lw["conv_ln_g"]), None), (row(lw["conv_ln_b"]), None),
        (lw["pool_w"], None), (row(lw["pool_scale"]), None),
        (lw["w_out"], None), (row(lw["b_out"]), None),
        (row(lw["ln1_g"]), None), (row(lw["ln1_b"]), None),
        (wr, None), (rbias.reshape(-1, 1), None),
        (perm, None), (perm.T, None), (tri, None),
    ]
    args = [a for a, _ in ins]
    specs = [s if s is not None else full(a.shape) for a, s in ins]
    out_shape = (
        jax.ShapeDtypeStruct((bsz, seq, d), F32),
        jax.ShapeDtypeStruct((bsz, seq, d // 2), U32),
        jax.ShapeDtypeStruct((2, n_tok), I32),
        jax.ShapeDtypeStruct((2, n_tok), I32),
        jax.ShapeDtypeStruct((n_tok, LANES), F32),
        jax.ShapeDtypeStruct((n_experts, LANES), F32),
    )
    out_specs = (
        pl.BlockSpec((1, ts, d), lambda b, i: (b, i, 0)),
        pl.BlockSpec((1, ts, d // 2), lambda b, i: (b, i, 0)),
        pl.BlockSpec((2, ts), lambda b, i: (0, b * ns + i)),
        pl.BlockSpec((2, ts), lambda b, i: (0, b * ns + i)),
        pl.BlockSpec((ts, LANES), lambda b, i: (b * ns + i, 0)),
        pl.BlockSpec((n_experts, LANES), lambda b, i: (0, 0)),
    )
    return pl.pallas_call(
        functools.partial(_mixer_body, alpha, n_experts, sub, prev is not None),
        grid=(bsz, ns),
        in_specs=specs,
        out_specs=out_specs,
        out_shape=out_shape,
        scratch_shapes=[
            pltpu.VMEM((nsub, CONV_HALO_VREGS * SUBLANES + sub, cw), F32),
            pltpu.VMEM((nsub, sub, cw), F32),
            pltpu.VMEM((nsub, POOL_HALO + sub, pw), F32),
            pltpu.VMEM((CONV_HALO_VREGS * SUBLANES, cw), F32),
            pltpu.VMEM((POOL_HALO, pw), F32),
            pltpu.VMEM((n_experts, LANES), F32),
            pltpu.VMEM((ts, d) if prev is not None else (SUBLANES, LANES), F32),
        ],
        compiler_params=pltpu.CompilerParams(dimension_semantics=("arbitrary", "arbitrary"), vmem_limit_bytes=VMEM_LIMIT),
        name="mixer_router",
    )(*args)


def _dest_body(n_experts, pstart_ref, e_ref, rank_ref, dest_ref):
    e = e_ref[...]
    dest = rank_ref[...]
    for ex in range(n_experts):
        dest = dest + jnp.where(e == ex, pstart_ref[ex], 0)
    dest_ref[...] = dest


def _dest_rows(pstart, eidx, rank):
    n_tok = eidx.shape[1]
    tn = min(8192, n_tok)
    grid_spec = pltpu.PrefetchScalarGridSpec(
        num_scalar_prefetch=1,
        grid=(n_tok // tn,),
        in_specs=[pl.BlockSpec((2, tn), lambda i, ps: (0, i)), pl.BlockSpec((2, tn), lambda i, ps: (0, i))],
        out_specs=pl.BlockSpec((2, tn), lambda i, ps: (0, i)),
    )
    return pl.pallas_call(
        functools.partial(_dest_body, pstart.shape[0]),
        grid_spec=grid_spec,
        out_shape=jax.ShapeDtypeStruct((2, n_tok), I32),
        compiler_params=pltpu.CompilerParams(dimension_semantics=("arbitrary",)),
        name="dest_rows",
    )(pstart, eidx, rank)


def _sc_mesh():
    return plsc.VectorSubcoreMesh(core_axis_name="c", subcore_axis_name="s",
                                  num_cores=SC_CORES, num_subcores=SC_SUBCORES)


def _dispatch(u2, d0, d1, n_rows):
    n_tok, d = u2.shape
    workers = SC_CORES * SC_SUBCORES
    per_w = n_tok // workers
    ch = SC_ROWS_PER_STEP
    nch = per_w // ch

    @functools.partial(
        pl.kernel, mesh=_sc_mesh(),
        out_type=jax.ShapeDtypeStruct((n_rows, d), u2.dtype),
        scratch_types=[pltpu.VMEM((nch, ch), I32), pltpu.VMEM((nch, ch), I32), pltpu.VMEM((2, ch, d), u2.dtype),
                       pltpu.SemaphoreType.DMA((2,)), pltpu.SemaphoreType.DMA((2,))],
        name="sc_dispatch",
    )
    def run(u_hbm, d0_hbm, d1_hbm, xs_hbm, i0, i1, rows, rsem, ssem):
        wid = lax.axis_index("s") * SC_CORES + lax.axis_index("c")
        base = wid * per_w
        pltpu.sync_copy(d0_hbm.at[pl.ds(wid * nch, nch)], i0)
        pltpu.sync_copy(d1_hbm.at[pl.ds(wid * nch, nch)], i1)

        def read(j, b):
            return pltpu.make_async_copy(u_hbm.at[pl.ds(base + j * ch, ch)], rows.at[b], rsem.at[b])

        def scatter(idx, j, b):
            return pltpu.make_async_copy(rows.at[b], xs_hbm.at[idx.at[j]], ssem.at[b])

        read(0, 0).start()

        @pl.loop(0, nch, step=2)
        def _(j0):
            for b in range(2):
                j = j0 + b
                read(j, b).wait()
                scatter(i0, j, b).start()
                scatter(i1, j, b).start()

                @pl.when(j >= 1)
                def _():
                    scatter(i0, j - 1, 1 - b).wait()
                    scatter(i1, j - 1, 1 - b).wait()

                @pl.when(j + 1 < nch)
                def _():
                    read(j + 1, 1 - b).start()

        scatter(i0, nch - 1, 1).wait()
        scatter(i1, nch - 1, 1).wait()

    return run(u2, d0.reshape(n_tok // ch, ch), d1.reshape(n_tok // ch, ch))


def _gather_rows(ys, dd):
    n_out = dd.shape[0]
    d = ys.shape[1]
    workers = SC_CORES * SC_SUBCORES
    per_w = n_out // workers
    ch = SC_ROWS_PER_STEP
    nch = per_w // ch

    @functools.partial(
        pl.kernel, mesh=_sc_mesh(),
        out_type=jax.ShapeDtypeStruct((n_out, d), ys.dtype),
        scratch_types=[pltpu.VMEM((nch, ch), I32), pltpu.VMEM((2, ch, d), ys.dtype),
                       pltpu.SemaphoreType.DMA((2,)), pltpu.SemaphoreType.DMA((2,))],
        name="sc_gather",
    )
    def run(ys_hbm, dd_hbm, out_hbm, idx, rows, gsem, wsem):
        wid = lax.axis_index("s") * SC_CORES + lax.axis_index("c")
        base = wid * per_w
        pltpu.sync_copy(dd_hbm.at[pl.ds(wid * nch, nch)], idx)

        def gather(j, b):
            return pltpu.make_async_copy(ys_hbm.at[idx.at[j]], rows.at[b], gsem.at[b])

        def write(j, b):
            return pltpu.make_async_copy(rows.at[b], out_hbm.at[pl.ds(base + j * ch, ch)], wsem.at[b])

        gather(0, 0).start()

        @pl.loop(0, nch, step=2)
        def _(j0):
            for b in range(2):
                j = j0 + b
                gather(j, b).wait()
                write(j, b).start()

                @pl.when(j >= 1)
                def _():
                    write(j - 1, 1 - b).wait()

                @pl.when(j + 1 < nch)
                def _():
                    gather(j + 1, 1 - b).start()

        write(nch - 1, 1).wait()

    return run(ys, dd.reshape(n_out // ch, ch))


def _swiglu_rows(packed, wgb, wub, wdb):
    lo, hi = _unpack_halves(packed)
    lo = lo.astype(BF16)
    hi = hi.astype(BF16)
    half = lo.shape[1]
    fw = wgb.shape[1] // EXPERT_F_SPLIT
    acc = None
    for c in range(EXPERT_F_SPLIT):
        cs = slice(c * fw, (c + 1) * fw)
        g = _dot(lo, wgb[0:half, cs]) + _dot(hi, wgb[half:, cs])
        up = _dot(lo, wub[0:half, cs]) + _dot(hi, wub[half:, cs])
        hid = (g * jax.nn.sigmoid(g) * up).astype(BF16)
        part = _dot(hid, wdb[cs, :])
        acc = part if acc is None else acc + part
    return _pack_halves(acc)


def _expert_body(blk_ref, nused_ref, xs_ref, wga_ref, wua_ref, wda_ref, wgb_ref, wub_ref, wdb_ref, ys_ref,
                 ag, au, ad, bg, bu, bd):
    j = pl.program_id(0)
    g0 = 2 * j
    used = g0 < nused_ref[0]
    e_a = blk_ref[g0]
    e_b = blk_ref[g0 + 1]
    mixed = jnp.logical_and(g0 + 1 < nused_ref[0], e_a != e_b)
    new_a = jnp.logical_or(j == 0, e_a != blk_ref[jnp.maximum(g0 - 2, 0)])
    gran = xs_ref.shape[0] // 2

    @pl.when(jnp.logical_and(used, new_a))
    def _():
        ag[...] = wga_ref[0, 0].astype(BF16)
        au[...] = wua_ref[0, 0].astype(BF16)
        ad[...] = wda_ref[0, 0].astype(BF16)

    @pl.when(jnp.logical_and(used, jnp.logical_not(mixed)))
    def _():
        ys_ref[...] = _swiglu_rows(xs_ref[...], ag, au, ad)

    @pl.when(jnp.logical_and(used, mixed))
    def _():
        bg[...] = wgb_ref[0, 0].astype(BF16)
        bu[...] = wub_ref[0, 0].astype(BF16)
        bd[...] = wdb_ref[0, 0].astype(BF16)
        ys_ref[0:gran, :] = _swiglu_rows(xs_ref[0:gran, :], ag, au, ad)
        ys_ref[gran:, :] = _swiglu_rows(xs_ref[gran:, :], bg, bu, bd)

    @pl.when(jnp.logical_not(used))
    def _():
        ys_ref[...] = jnp.zeros(ys_ref.shape, U32)


def _experts(blk_e, nused, xs, wg, wu, wd, layer):
    n_rows = xs.shape[0]
    d = wg.shape[2]
    f = wg.shape[3]
    tm = EXPERT_TILE
    last = lambda nu: jnp.maximum((nu[0] + 1) // 2 - 1, 0)
    w_a = lambda shape: pl.BlockSpec(shape, lambda j, be, nu: (layer, be[2 * j], 0, 0))
    w_b = lambda shape: pl.BlockSpec(shape, lambda j, be, nu: (layer, be[2 * j + 1], 0, 0))
    grid_spec = pltpu.PrefetchScalarGridSpec(
        num_scalar_prefetch=2,
        grid=(n_rows // tm,),
        in_specs=[
            pl.BlockSpec((tm, d // 2), lambda j, be, nu: (jnp.minimum(j, last(nu)), 0)),
            w_a((1, 1, d, f)), w_a((1, 1, d, f)), w_a((1, 1, f, d)),
            w_b((1, 1, d, f)), w_b((1, 1, d, f)), w_b((1, 1, f, d)),
        ],
        out_specs=pl.BlockSpec((tm, d // 2), lambda j, be, nu: (j, 0)),
        scratch_shapes=[pltpu.VMEM((d, f), BF16), pltpu.VMEM((d, f), BF16), pltpu.VMEM((f, d), BF16),
                        pltpu.VMEM((d, f), BF16), pltpu.VMEM((d, f), BF16), pltpu.VMEM((f, d), BF16)],
    )
    return pl.pallas_call(
        _expert_body,
        grid_spec=grid_spec,
        out_shape=jax.ShapeDtypeStruct((n_rows, d // 2), U32),
        compiler_params=pltpu.CompilerParams(dimension_semantics=("arbitrary",), vmem_limit_bytes=VMEM_LIMIT),
        name="expert_ffn",
    )(blk_e, nused, xs, wg, wu, wd, wg, wu, wd)


def _combine_body(alpha, x1_ref, y0_ref, y1_ref, mod_ref, wm_ref, g_ref, b_ref, *rest):
    o_ref = rest[-1]
    o_ref[...] = _combine_rows(alpha, x1_ref[...], y0_ref[...], y1_ref[...], wm_ref[...],
                               mod_ref[0][5:6], g_ref[...], b_ref[...])


def _combine(x1, yy, mod, wm, ln_g, ln_b, seq, alpha, mod_b0, out_rows, out_row0, out_prev):
    n_tok, d = x1.shape
    tn = min(COMBINE_TILE, seq)
    per_seq = seq // tn
    nblk = n_tok // tn
    blk0 = out_row0 // tn
    rows = lambda: pl.BlockSpec((tn, d), lambda i: (i, 0))
    in_specs = [rows(), pl.BlockSpec((tn, d // 2), lambda i: (i, 0)),
                pl.BlockSpec((tn, d // 2), lambda i: (nblk + i, 0)),
                pl.BlockSpec((1, 6, d), lambda i: (i // per_seq + mod_b0, 0, 0)),
                pl.BlockSpec((tn, LANES), lambda i: (i, 0)),
                pl.BlockSpec((1, d), lambda i: (0, 0)),
                pl.BlockSpec((1, d), lambda i: (0, 0))]
    args = [x1, yy, yy, mod, wm, ln_g.reshape(1, d), ln_b.reshape(1, d)]
    aliases = {}
    if out_prev is not None:
        in_specs.append(pl.BlockSpec(memory_space=pl.ANY))
        args.append(out_prev)
        aliases = {len(args) - 1: 0}
    return pl.pallas_call(
        functools.partial(_combine_body, alpha),
        grid=(nblk,),
        in_specs=in_specs,
        out_specs=pl.BlockSpec((tn, d), lambda i: (blk0 + i, 0)),
        out_shape=jax.ShapeDtypeStruct((out_rows, d), F32),
        input_output_aliases=aliases,
        compiler_params=pltpu.CompilerParams(dimension_semantics=("arbitrary",), vmem_limit_bytes=VMEM_LIMIT),
        name="combine_ln",
    )(*args)


def kernel(x, c, w_ada, b_ada, w_in, b_in, conv_w, conv_b, conv_ln_g, conv_ln_b, pool_w, pool_scale, w_out, b_out, ln1_g, ln1_b, w_router, router_bias, w_gate, w_up, w_down, ln2_g, ln2_b):
    bsz, seq, d = x.shape
    depth = w_ada.shape[0]
    n_experts = w_router.shape[1]
    alpha = float((2 * depth) ** 0.25)
    gran = EXPERT_TILE // 2
    n_chunks = BATCH_CHUNKS if bsz % BATCH_CHUNKS == 0 else 1
    bc = bsz // n_chunks
    n_tok = bc * seq
    n_rows = -(-(2 * n_tok + n_experts * gran) // EXPERT_TILE) * EXPERT_TILE

    mod_all = _ada_mod(c, w_ada, b_ada).reshape(depth, bsz, 6, d)
    wr = jnp.pad(w_router.astype(BF16), ((0, 0), (0, LANES - n_experts)))

    chunks = [x] * n_chunks
    starts = [ci * bc for ci in range(n_chunks)]
    prevs = [None] * n_chunks
    out = None
    for l in range(depth):
        lw = dict(w_in=w_in[l].astype(BF16), b_in=b_in[l], conv_w=conv_w[l], conv_b=conv_b[l],
                  conv_ln_g=conv_ln_g[l], conv_ln_b=conv_ln_b[l], pool_w=pool_w[l].astype(BF16),
                  pool_scale=pool_scale[l], w_out=w_out[l].astype(BF16), b_out=b_out[l],
                  ln1_g=ln1_g[l], ln1_b=ln1_b[l])
        mod = mod_all[l]
        last = l == depth - 1
        st = []
        for ci in range(n_chunks):
            x1, u2, eidx, rank, wm, cnt = _mixer(chunks[ci], mod, lw, wr, router_bias, alpha,
                                                 bc, starts[ci], ci * bc, prevs[ci])
            counts = cnt[:, 0].astype(I32)
            grans = (counts + gran - 1) // gran
            gran_end = jnp.cumsum(grans)
            pstart = ((gran_end - grans) * gran).astype(I32)
            nused = gran_end[-1:].astype(I32)
            blk_e = jnp.minimum(
                jnp.sum(gran_end[None, :] <= jnp.arange(n_rows // gran, dtype=I32)[:, None], axis=1), n_experts - 1
            ).astype(I32)
            dest = _dest_rows(pstart, eidx, rank)
            xs = _dispatch(u2.reshape(n_tok, d // 2), dest[0], dest[1], n_rows)
            st.append((x1, wm, blk_e, nused, dest, xs))
        ys = [_experts(blk_e, nused, xs, w_gate, w_up, w_down, l) for (_, _, blk_e, nused, _, xs) in st]
        yy = [_gather_rows(ys[ci], st[ci][4].reshape(2 * n_tok)) for ci in range(n_chunks)]
        for ci in range(n_chunks):
            x1, wm = st[ci][0], st[ci][1]
            if last:
                out = _combine(x1.reshape(n_tok, d), yy[ci], mod, wm, ln2_g[l], ln2_b[l], seq, alpha,
                               ci * bc, bsz * seq, ci * n_tok, out)
            else:
                chunks[ci] = x1
                starts[ci] = 0
                prevs[ci] = (yy[ci], wm, mod, ln2_g[l], ln2_b[l])
    return out.reshape(bsz, seq, d)
```

```python
import functools

import jax
import jax.numpy as jnp
from jax import lax
from jax.experimental import pallas as pl
from jax.experimental.pallas import tpu as pltpu
from jax.experimental.pallas import tpu_sc as plsc

F32 = jnp.float32
BF16 = jnp.bfloat16
I32 = jnp.int32
U32 = jnp.uint32

CONV_KERNEL = 31
POOL_WINDOWS = (2, 4, 8, 16)
N_EXPERT_GROUPS = 4
LN_EPS = 1e-5

SUBLANES = 8
LANES = 128

SEQ_TILE = 1024
SEQ_SUB_TILE = 256
STAGE_LEAD = 4
EXPERT_TILE = 1024
EXPERT_F_SPLIT = 2
COMBINE_TILE = 1024
BATCH_CHUNKS = 2
CONV_HALO_VREGS = 32
POOL_HALO = 16
VMEM_LIMIT = 56 * 1024 * 1024
SC_CORES = 2
SC_SUBCORES = 16
SC_ROWS_PER_STEP = 64


def _split_bf16(a):
    hi = a.astype(BF16)
    lo = (a - hi.astype(F32)).astype(BF16)
    return hi, lo


def _dot(a, b):
    return jnp.dot(a, b, preferred_element_type=F32)


def _pack_halves(y):
    h = y.shape[1] // 2
    lo = lax.bitcast_convert_type(y[:, :h].astype(BF16).astype(F32), U32)
    hi = lax.bitcast_convert_type(y[:, h:].astype(BF16).astype(F32), U32)
    return (lo >> 16) | hi


def _unpack_halves(p):
    lo = lax.bitcast_convert_type(p << 16, F32)
    hi = lax.bitcast_convert_type(p & jnp.uint32(0xFFFF0000), F32)
    return lo, hi


def _ada_body(c_ref, w_ref, b_ref, o_ref):
    c = c_ref[...]
    ca = c * jax.nn.sigmoid(c)
    chi, clo = _split_bf16(ca)
    whi, wlo = _split_bf16(w_ref[0])
    o_ref[0] = _dot(chi, whi) + _dot(chi, wlo) + _dot(clo, whi) + b_ref[0]


def _ada_mod(c, w_ada, b_ada):
    n_layers, d, n6 = w_ada.shape
    bsz = c.shape[0]
    tn = n6 // 6
    return pl.pallas_call(
        _ada_body,
        grid=(n_layers, n6 // tn),
        in_specs=[
            pl.BlockSpec((bsz, d), lambda l, j: (0, 0)),
            pl.BlockSpec((1, d, tn), lambda l, j: (l, 0, j)),
            pl.BlockSpec((1, 1, tn), lambda l, j: (l, 0, j)),
        ],
        out_specs=pl.BlockSpec((1, bsz, tn), lambda l, j: (l, 0, j)),
        out_shape=jax.ShapeDtypeStruct((n_layers, bsz, n6), F32),
        compiler_params=pltpu.CompilerParams(dimension_semantics=("arbitrary", "arbitrary"), vmem_limit_bytes=VMEM_LIMIT),
        name="ada_mod",
    )(c, w_ada, b_ada.reshape(n_layers, 1, n6))


def _layer_norm_rows(z, g, b):
    mu = jnp.mean(z, axis=-1, keepdims=True)
    zc = z - mu
    var = jnp.mean(zc * zc, axis=-1, keepdims=True)
    return zc * lax.rsqrt(var + LN_EPS) * g + b


def _combine_rows(alpha, x1, y0, y1, wm, g2, ln_g, ln_b):
    lo0, hi0 = _unpack_halves(y0)
    lo1, hi1 = _unpack_halves(y1)
    w0, w1 = wm[:, 0:1], wm[:, 1:2]
    f = jnp.concatenate([w0 * lo0 + w1 * lo1, w0 * hi0 + w1 * hi1], axis=1)
    return _layer_norm_rows(alpha * x1 + (1.0 + g2) * f, ln_g, ln_b)


def _mixer_body(alpha, n_experts, sub, fused, *refs):
    if fused:
        xp_ref, y0_ref, y1_ref, wmp_ref, modp_ref, l2g_ref, l2b_ref = refs[:7]
        refs = refs[7:]
        x_ref = None
    else:
        x_ref = refs[0]
        refs = refs[1:]
    (mod_ref, win_ref, bin_ref, cw_ref, cb_ref, cg_ref, cbeta_ref, pw_ref, ps_ref,
     wout_ref, bout_ref, l1g_ref, l1b_ref, wr_ref, rb_ref, perm_ref, permt_ref, tri_ref,
     x1_ref, u2_ref, e_ref, rank_ref, wm_ref, cnt_ref,
     ebuf, cbuf, pbuf, vprev, phalo, carry, xbuf) = refs
    ts = x1_ref.shape[1]
    cw = cg_ref.shape[1]
    pw = ps_ref.shape[1]
    gw = pw // len(POOL_WINDOWS)
    fine = sub // SUBLANES
    halo = CONV_HALO_VREGS
    b = pl.program_id(0)
    i = pl.program_id(1)

    @pl.when(i == 0)
    def _():
        vprev[...] = jnp.zeros(vprev.shape, F32)
        phalo[...] = jnp.zeros(phalo.shape, F32)

    @pl.when(jnp.logical_and(b == 0, i == 0))
    def _():
        carry[...] = jnp.zeros(carry.shape, F32)

    mod = mod_ref[0]
    sh1, sc1, g1 = mod[0:1], mod[1:2], mod[2:3]
    sh2, sc2 = mod[3:4], mod[4:5]

    def stage1(s):
        r0 = s * sub
        if fused:
            x = _combine_rows(alpha, xp_ref[0, r0:r0 + sub, :], y0_ref[r0:r0 + sub, :], y1_ref[r0:r0 + sub, :],
                              wmp_ref[r0:r0 + sub, :], modp_ref[0][5:6], l2g_ref[...], l2b_ref[...])
            xbuf[r0:r0 + sub, :] = x
        else:
            x = x_ref[0, r0:r0 + sub, :]
        u = (x * (1.0 + sc1) + sh1).astype(BF16)
        h = _dot(u, win_ref[...]) + bin_ref[...]
        a = h[:, :cw]
        gate = h[:, cw:2 * cw]
        p = h[:, 2 * cw:]

        v = (a * jax.nn.sigmoid(gate)).astype(BF16)
        vb = _dot(perm_ref[...], v)
        ebuf[s, halo * SUBLANES:, :] = vb
        sl = lax.broadcasted_iota(I32, (SUBLANES, cw), 0)
        for mm in range(halo):
            src = (fine - halo + mm) * SUBLANES
            cur = vb[src:src + SUBLANES, :]
            prev = vprev[mm * SUBLANES:(mm + 1) * SUBLANES, :]
            ebuf[s, mm * SUBLANES:(mm + 1) * SUBLANES, :] = pltpu.roll(
                jnp.where(sl == SUBLANES - 1, prev, cur), 1, 0)
        vprev[...] = vb[(fine - halo) * SUBLANES:, :]
        pbuf[s, 0:POOL_HALO, :] = phalo[...]
        pbuf[s, POOL_HALO:, :] = p
        phalo[...] = p[sub - POOL_HALO:, :]

    def stage2(s):
        r0 = s * sub
        x = xbuf[r0:r0 + sub, :] if fused else x_ref[0, r0:r0 + sub, :]
        chunk = 8 * SUBLANES
        first = halo - (CONV_KERNEL - 1)
        for q0 in range(0, sub, chunk):
            for c0 in range(0, cw, LANES):
                acc = jnp.broadcast_to(cb_ref[:, c0:c0 + LANES], (chunk, LANES))
                for k in range(CONV_KERNEL):
                    off = q0 + (first + k) * SUBLANES
                    acc = acc + cw_ref[k:k + 1, c0:c0 + LANES] * ebuf[s, off:off + chunk, c0:c0 + LANES]
                cbuf[s, q0:q0 + chunk, c0:c0 + LANES] = acc
        yc = _layer_norm_rows(cbuf[s], cg_ref[...], cbeta_ref[...])
        ya_b = (yc * jax.nn.sigmoid(yc)).astype(BF16)
        ya = _dot(permt_ref[...], ya_b).astype(BF16)

        ext = pbuf[s]
        p = ext[POOL_HALO:, :]
        w_sum = ext + pltpu.roll(ext, 1, 0)
        parts = [w_sum[:, :gw]]
        shift = 2
        for _ in range(len(POOL_WINDOWS) - 1):
            w_sum = w_sum[:, gw:]
            w_sum = w_sum + pltpu.roll(w_sum, shift, 0)
            parts.append(w_sum[:, :gw])
            shift *= 2
        wsum = jnp.concatenate(parts, axis=1)[POOL_HALO:, :]
        pos = (lax.broadcasted_iota(I32, (sub, gw), 0) + (i * ts + r0 + 1)).astype(F32)
        cnt = jnp.concatenate([jnp.minimum(pos[:POOL_HALO], float(wlen)) for wlen in POOL_WINDOWS], axis=1)
        inv = jnp.concatenate([jnp.full((1, gw), 1.0 / wlen, F32) for wlen in POOL_WINDOWS], axis=1)
        mean = jnp.concatenate([wsum[:POOL_HALO] / cnt, wsum[POOL_HALO:] * inv], axis=0)
        dpool = (mean - p).astype(BF16)
        yb_parts = [_dot(dpool[:, gi * gw:(gi + 1) * gw], pw_ref[gi]) for gi in range(len(POOL_WINDOWS))]
        yb = (jnp.concatenate(yb_parts, axis=1) * ps_ref[...]).astype(BF16)

        mix = _dot(jnp.concatenate([ya, yb], axis=1), wout_ref[...]) + bout_ref[...]
        x1 = _layer_norm_rows(alpha * x + (1.0 + g1) * mix, l1g_ref[...], l1b_ref[...])
        x1_ref[0, r0:r0 + sub, :] = x1
        u2 = x1 * (1.0 + sc2) + sh2
        u2_ref[0, r0:r0 + sub, :] = _pack_halves(u2)

        logits = _dot(u2.astype(BF16), wr_ref[...]).T[0:n_experts, :]
        scores = jax.nn.sigmoid(logits)
        sel = scores + rb_ref[...]
        epg = n_experts // N_EXPERT_GROUPS
        io = lax.broadcasted_iota(I32, (epg, sub), 0).astype(F32)
        neg = jnp.float32(-jnp.inf)
        best = None
        for g in range(N_EXPERT_GROUPS):
            sg = sel[g * epg:(g + 1) * epg, :]
            scg = scores[g * epg:(g + 1) * epg, :]
            m1 = jnp.max(sg, axis=0, keepdims=True)
            i1 = jnp.min(jnp.where(sg == m1, io, float(epg)), axis=0, keepdims=True)
            rest = jnp.where(io == i1, neg, sg)
            m2 = jnp.max(rest, axis=0, keepdims=True)
            i2 = jnp.min(jnp.where(jnp.logical_and(rest == m2, io != i1), io, float(epg)), axis=0, keepdims=True)
            s1 = jnp.sum(jnp.where(io == i1, scg, 0.0), axis=0, keepdims=True)
            s2 = jnp.sum(jnp.where(io == i2, scg, 0.0), axis=0, keepdims=True)
            gs = m1 + m2
            cand = (gs, i1 + float(g * epg), i2 + float(g * epg), s1, s2)
            if best is None:
                best = cand
            else:
                upd = gs > best[0]
                best = tuple(jnp.where(upd, cn, bs) for cn, bs in zip(cand, best))
        _, e1, e2, s1, s2 = best
        denom = s1 + s2
        w1 = s1 / denom
        w2 = s2 / denom

        ioe = lax.broadcasted_iota(I32, (n_experts, sub), 0).astype(F32)
        hit1 = ioe == e1
        hit2 = ioe == e2
        onehot = jnp.where(jnp.logical_or(hit1, hit2), 1.0, 0.0)
        before = _dot(onehot.astype(BF16), tri_ref[...]) + carry[:, 0:1]
        r1 = jnp.sum(jnp.where(hit1, before, 0.0), axis=0, keepdims=True)
        r2 = jnp.sum(jnp.where(hit2, before, 0.0), axis=0, keepdims=True)
        carry[...] = carry[...] + jnp.sum(onehot, axis=1, keepdims=True)

        e_ref[0:1, r0:r0 + sub] = e1.astype(I32)
        e_ref[1:2, r0:r0 + sub] = e2.astype(I32)
        rank_ref[0:1, r0:r0 + sub] = r1.astype(I32)
        rank_ref[1:2, r0:r0 + sub] = r2.astype(I32)
        rowi = lax.broadcasted_iota(I32, (LANES, sub), 0)
        wpad = jnp.where(rowi == 0, w1, jnp.where(rowi == 1, w2, 0.0))
        wm_ref[r0:r0 + sub, :] = wpad.T

    nsub = ts // sub
    for s in range(min(STAGE_LEAD, nsub)):
        stage1(s)
    for s in range(nsub):
        if s + STAGE_LEAD < nsub:
            stage1(s + STAGE_LEAD)
        stage2(s)
    cnt_ref[...] = carry[...]


def _row_perm(n):
    r = jnp.arange(n)
    t = (n // SUBLANES) * (r % SUBLANES) + r // SUBLANES
    return (t[:, None] == jnp.arange(n)[None, :]).astype(BF16)


def _mixer(x, mod, lw, wr, rbias, alpha, bsz, x_b0, mod_b0, prev=None):
    _, seq, d = x.shape
    ts = min(SEQ_TILE, seq)
    sub = min(SEQ_SUB_TILE, ts)
    nsub = ts // sub
    ns = seq // ts
    n_tok = bsz * seq
    n_experts = rbias.shape[0]
    cw = lw["conv_w"].shape[1]
    pw = lw["pool_scale"].shape[0]
    perm = _row_perm(sub)
    tri = (jnp.arange(sub)[:, None] < jnp.arange(sub)[None, :]).astype(BF16)
    row = lambda a: a.reshape(1, -1)
    full = lambda shape: pl.BlockSpec(shape, lambda b, i: (0,) * len(shape))
    if prev is None:
        ins = [(x, pl.BlockSpec((1, ts, d), lambda b, i: (b + x_b0, i, 0)))]
    else:
        yy, wmp, modp, l2g, l2b = prev
        nblk = n_tok // ts
        ins = [
            (x, pl.BlockSpec((1, ts, d), lambda b, i: (b + x_b0, i, 0))),
            (yy, pl.BlockSpec((ts, d // 2), lambda b, i: (b * ns + i, 0))),
            (yy, pl.BlockSpec((ts, d // 2), lambda b, i: (nblk + b * ns + i, 0))),
            (wmp, pl.BlockSpec((ts, LANES), lambda b, i: (b * ns + i, 0))),
            (modp, pl.BlockSpec((1, 6, d), lambda b, i: (b + mod_b0, 0, 0))),
            (row(l2g), None), (row(l2b), None),
        ]
    ins += [
        (mod, pl.BlockSpec((1, 6, d), lambda b, i: (b + mod_b0, 0, 0))),
        (lw["w_in"], None), (row(lw["b_in"]), None),
        (jnp.pad(lw["conv_w"], ((0, 1), (0, 0))), None), (row(lw["conv_b"]), None),
        (row(lw["conv_ln_g"]), None), (row(lw["conv_ln_b"]), None),
        (lw["pool_w"], None), (row(lw["pool_scale"]), None),
        (lw["w_out"], None), (row(lw["b_out"]), None),
        (row(lw["ln1_g"]), None), (row(lw["ln1_b"]), None),
        (wr, None), (rbias.reshape(-1, 1), None),
        (perm, None), (perm.T, None), (tri, None),
    ]
    args = [a for a, _ in ins]
    specs = [s if s is not None else full(a.shape) for a, s in ins]
    out_shape = (
        jax.ShapeDtypeStruct((bsz, seq, d), F32),
        jax.ShapeDtypeStruct((bsz, seq, d // 2), U32),
        jax.ShapeDtypeStruct((2, n_tok), I32),
        jax.ShapeDtypeStruct((2, n_tok), I32),
        jax.ShapeDtypeStruct((n_tok, LANES), F32),
        jax.ShapeDtypeStruct((n_experts, LANES), F32),
    )
    out_specs = (
        pl.BlockSpec((1, ts, d), lambda b, i: (b, i, 0)),
        pl.BlockSpec((1, ts, d // 2), lambda b, i: (b, i, 0)),
        pl.BlockSpec((2, ts), lambda b, i: (0, b * ns + i)),
        pl.BlockSpec((2, ts), lambda b, i: (0, b * ns + i)),
        pl.BlockSpec((ts, LANES), lambda b, i: (b * ns + i, 0)),
        pl.BlockSpec((n_experts, LANES), lambda b, i: (0, 0)),
    )
    return pl.pallas_call(
        functools.partial(_mixer_body, alpha, n_experts, sub, prev is not None),
        grid=(bsz, ns),
        in_specs=specs,
        out_specs=out_specs,
        out_shape=out_shape,
        scratch_shapes=[
            pltpu.VMEM((nsub, CONV_HALO_VREGS * SUBLANES + sub, cw), F32),
            pltpu.VMEM((nsub, sub, cw), F32),
            pltpu.VMEM((nsub, POOL_HALO + sub, pw), F32),
            pltpu.VMEM((CONV_HALO_VREGS * SUBLANES, cw), F32),
            pltpu.VMEM((POOL_HALO, pw), F32),
            pltpu.VMEM((n_experts, LANES), F32),
            pltpu.VMEM((ts, d) if prev is not None else (SUBLANES, LANES), F32),
        ],
        compiler_params=pltpu.CompilerParams(dimension_semantics=("arbitrary", "arbitrary"), vmem_limit_bytes=VMEM_LIMIT),
        name="mixer_router",
    )(*args)


def _dest_body(n_experts, pstart_ref, e_ref, rank_ref, dest_ref):
    e = e_ref[...]
    dest = rank_ref[...]
    for ex in range(n_experts):
        dest = dest + jnp.where(e == ex, pstart_ref[ex], 0)
    dest_ref[...] = dest


def _dest_rows(pstart, eidx, rank):
    n_tok = eidx.shape[1]
    tn = min(8192, n_tok)
    grid_spec = pltpu.PrefetchScalarGridSpec(
        num_scalar_prefetch=1,
        grid=(n_tok // tn,),
        in_specs=[pl.BlockSpec((2, tn), lambda i, ps: (0, i)), pl.BlockSpec((2, tn), lambda i, ps: (0, i))],
        out_specs=pl.BlockSpec((2, tn), lambda i, ps: (0, i)),
    )
    return pl.pallas_call(
        functools.partial(_dest_body, pstart.shape[0]),
        grid_spec=grid_spec,
        out_shape=jax.ShapeDtypeStruct((2, n_tok), I32),
        compiler_params=pltpu.CompilerParams(dimension_semantics=("arbitrary",)),
        name="dest_rows",
    )(pstart, eidx, rank)


def _sc_mesh():
    return plsc.VectorSubcoreMesh(core_axis_name="c", subcore_axis_name="s",
                                  num_cores=SC_CORES, num_subcores=SC_SUBCORES)


def _dispatch(u2, d0, d1, n_rows):
    n_tok, d = u2.shape
    workers = SC_CORES * SC_SUBCORES
    per_w = n_tok // workers
    ch = SC_ROWS_PER_STEP
    nch = per_w // ch

    @functools.partial(
        pl.kernel, mesh=_sc_mesh(),
        out_type=jax.ShapeDtypeStruct((n_rows, d), u2.dtype),
        scratch_types=[pltpu.VMEM((nch, ch), I32), pltpu.VMEM((nch, ch), I32), pltpu.VMEM((2, ch, d), u2.dtype),
                       pltpu.SemaphoreType.DMA((2,)), pltpu.SemaphoreType.DMA((2,))],
        name="sc_dispatch",
    )
    def run(u_hbm, d0_hbm, d1_hbm, xs_hbm, i0, i1, rows, rsem, ssem):
        wid = lax.axis_index("s") * SC_CORES + lax.axis_index("c")
        base = wid * per_w
        pltpu.sync_copy(d0_hbm.at[pl.ds(wid * nch, nch)], i0)
        pltpu.sync_copy(d1_hbm.at[pl.ds(wid * nch, nch)], i1)

        def read(j, b):
            return pltpu.make_async_copy(u_hbm.at[pl.ds(base + j * ch, ch)], rows.at[b], rsem.at[b])

        def scatter(idx, j, b):
            return pltpu.make_async_copy(rows.at[b], xs_hbm.at[idx.at[j]], ssem.at[b])

        read(0, 0).start()

        @pl.loop(0, nch, step=2)
        def _(j0):
            for b in range(2):
                j = j0 + b
                read(j, b).wait()
                scatter(i0, j, b).start()
                scatter(i1, j, b).start()

                @pl.when(j >= 1)
                def _():
                    scatter(i0, j - 1, 1 - b).wait()
                    scatter(i1, j - 1, 1 - b).wait()

                @pl.when(j + 1 < nch)
                def _():
                    read(j + 1, 1 - b).start()

        scatter(i0, nch - 1, 1).wait()
        scatter(i1, nch - 1, 1).wait()

    return run(u2, d0.reshape(n_tok // ch, ch), d1.reshape(n_tok // ch, ch))


def _gather_rows(ys, dd):
    n_out = dd.shape[0]
    d = ys.shape[1]
    workers = SC_CORES * SC_SUBCORES
    per_w = n_out // workers
    ch = SC_ROWS_PER_STEP
    nch = per_w // ch

    @functools.partial(
        pl.kernel, mesh=_sc_mesh(),
        out_type=jax.ShapeDtypeStruct((n_out, d), ys.dtype),
        scratch_types=[pltpu.VMEM((nch, ch), I32), pltpu.VMEM((2, ch, d), ys.dtype),
                       pltpu.SemaphoreType.DMA((2,)), pltpu.SemaphoreType.DMA((2,))],
        name="sc_gather",
    )
    def run(ys_hbm, dd_hbm, out_hbm, idx, rows, gsem, wsem):
        wid = lax.axis_index("s") * SC_CORES + lax.axis_index("c")
        base = wid * per_w
        pltpu.sync_copy(dd_hbm.at[pl.ds(wid * nch, nch)], idx)

        def gather(j, b):
            return pltpu.make_async_copy(ys_hbm.at[idx.at[j]], rows.at[b], gsem.at[b])

        def write(j, b):
            return pltpu.make_async_copy(rows.at[b], out_hbm.at[pl.ds(base + j * ch, ch)], wsem.at[b])

        gather(0, 0).start()

        @pl.loop(0, nch, step=2)
        def _(j0):
            for b in range(2):
                j = j0 + b
                gather(j, b).wait()
                write(j, b).start()

                @pl.when(j >= 1)
                def _():
                    write(j - 1, 1 - b).wait()

                @pl.when(j + 1 < nch)
                def _():
                    gather(j + 1, 1 - b).start()

        write(nch - 1, 1).wait()

    return run(ys, dd.reshape(n_out // ch, ch))


def _expert_body(blk_ref, nused_ref, xs_ref, wg_ref, wu_ref, wd_ref, ys_ref, wgb, wub, wdb):
    j = pl.program_id(0)
    used = j < nused_ref[0]
    new_expert = jnp.logical_or(j == 0, blk_ref[j] != blk_ref[jnp.maximum(j - 1, 0)])

    @pl.when(jnp.logical_and(used, new_expert))
    def _():
        wgb[...] = wg_ref[0, 0].astype(BF16)
        wub[...] = wu_ref[0, 0].astype(BF16)
        wdb[...] = wd_ref[0, 0].astype(BF16)

    @pl.when(used)
    def _():
        lo, hi = _unpack_halves(xs_ref[...])
        lo = lo.astype(BF16)
        hi = hi.astype(BF16)
        half = lo.shape[1]
        fw = wgb.shape[1] // EXPERT_F_SPLIT
        acc = None
        for c in range(EXPERT_F_SPLIT):
            cs = slice(c * fw, (c + 1) * fw)
            g = _dot(lo, wgb[0:half, cs]) + _dot(hi, wgb[half:, cs])
            up = _dot(lo, wub[0:half, cs]) + _dot(hi, wub[half:, cs])
            hid = (g * jax.nn.sigmoid(g) * up).astype(BF16)
            part = _dot(hid, wdb[cs, :])
            acc = part if acc is None else acc + part
        ys_ref[...] = _pack_halves(acc)

    @pl.when(j >= nused_ref[0])
    def _():
        ys_ref[...] = jnp.zeros(ys_ref.shape, U32)


def _experts(blk_e, nused, xs, wg, wu, wd, layer):
    n_rows = xs.shape[0]
    d = wg.shape[2]
    f = wg.shape[3]
    tm = EXPERT_TILE
    grid_spec = pltpu.PrefetchScalarGridSpec(
        num_scalar_prefetch=2,
        grid=(n_rows // tm,),
        in_specs=[
            pl.BlockSpec((tm, d // 2), lambda j, be, nu: (jnp.minimum(j, nu[0] - 1), 0)),
            pl.BlockSpec((1, 1, d, f), lambda j, be, nu: (layer, be[j], 0, 0)),
            pl.BlockSpec((1, 1, d, f), lambda j, be, nu: (layer, be[j], 0, 0)),
            pl.BlockSpec((1, 1, f, d), lambda j, be, nu: (layer, be[j], 0, 0)),
        ],
        out_specs=pl.BlockSpec((tm, d // 2), lambda j, be, nu: (j, 0)),
        scratch_shapes=[pltpu.VMEM((d, f), BF16), pltpu.VMEM((d, f), BF16), pltpu.VMEM((f, d), BF16)],
    )
    return pl.pallas_call(
        _expert_body,
        grid_spec=grid_spec,
        out_shape=jax.ShapeDtypeStruct((n_rows, d // 2), U32),
        compiler_params=pltpu.CompilerParams(dimension_semantics=("arbitrary",), vmem_limit_bytes=VMEM_LIMIT),
        name="expert_ffn",
    )(blk_e, nused, xs, wg, wu, wd)


def _combine_body(alpha, x1_ref, y0_ref, y1_ref, mod_ref, wm_ref, g_ref, b_ref, *rest):
    o_ref = rest[-1]
    o_ref[...] = _combine_rows(alpha, x1_ref[...], y0_ref[...], y1_ref[...], wm_ref[...],
                               mod_ref[0][5:6], g_ref[...], b_ref[...])


def _combine(x1, yy, mod, wm, ln_g, ln_b, seq, alpha, mod_b0, out_rows, out_row0, out_prev):
    n_tok, d = x1.shape
    tn = min(COMBINE_TILE, seq)
    per_seq = seq // tn
    nblk = n_tok // tn
    blk0 = out_row0 // tn
    rows = lambda: pl.BlockSpec((tn, d), lambda i: (i, 0))
    in_specs = [rows(), pl.BlockSpec((tn, d // 2), lambda i: (i, 0)),
                pl.BlockSpec((tn, d // 2), lambda i: (nblk + i, 0)),
                pl.BlockSpec((1, 6, d), lambda i: (i // per_seq + mod_b0, 0, 0)),
                pl.BlockSpec((tn, LANES), lambda i: (i, 0)),
                pl.BlockSpec((1, d), lambda i: (0, 0)),
                pl.BlockSpec((1, d), lambda i: (0, 0))]
    args = [x1, yy, yy, mod, wm, ln_g.reshape(1, d), ln_b.reshape(1, d)]
    aliases = {}
    if out_prev is not None:
        in_specs.append(pl.BlockSpec(memory_space=pl.ANY))
        args.append(out_prev)
        aliases = {len(args) - 1: 0}
    return pl.pallas_call(
        functools.partial(_combine_body, alpha),
        grid=(nblk,),
        in_specs=in_specs,
        out_specs=pl.BlockSpec((tn, d), lambda i: (blk0 + i, 0)),
        out_shape=jax.ShapeDtypeStruct((out_rows, d), F32),
        input_output_aliases=aliases,
        compiler_params=pltpu.CompilerParams(dimension_semantics=("arbitrary",), vmem_limit_bytes=VMEM_LIMIT),
        name="combine_ln",
    )(*args)


def kernel(x, c, w_ada, b_ada, w_in, b_in, conv_w, conv_b, conv_ln_g, conv_ln_b, pool_w, pool_scale, w_out, b_out, ln1_g, ln1_b, w_router, router_bias, w_gate, w_up, w_down, ln2_g, ln2_b):
    bsz, seq, d = x.shape
    depth = w_ada.shape[0]
    n_experts = w_router.shape[1]
    alpha = float((2 * depth) ** 0.25)
    tm = EXPERT_TILE
    n_chunks = BATCH_CHUNKS if bsz % BATCH_CHUNKS == 0 else 1
    bc = bsz // n_chunks
    n_tok = bc * seq
    n_rows = 2 * n_tok + n_experts * tm

    mod_all = _ada_mod(c, w_ada, b_ada).reshape(depth, bsz, 6, d)
    wr = jnp.pad(w_router.astype(BF16), ((0, 0), (0, LANES - n_experts)))

    chunks = [x] * n_chunks
    starts = [ci * bc for ci in range(n_chunks)]
    prevs = [None] * n_chunks
    out = None
    for l in range(depth):
        lw = dict(w_in=w_in[l].astype(BF16), b_in=b_in[l], conv_w=conv_w[l], conv_b=conv_b[l],
                  conv_ln_g=conv_ln_g[l], conv_ln_b=conv_ln_b[l], pool_w=pool_w[l].astype(BF16),
                  pool_scale=pool_scale[l], w_out=w_out[l].astype(BF16), b_out=b_out[l],
                  ln1_g=ln1_g[l], ln1_b=ln1_b[l])
        mod = mod_all[l]
        last = l == depth - 1
        st = []
        for ci in range(n_chunks):
            x1, u2, eidx, rank, wm, cnt = _mixer(chunks[ci], mod, lw, wr, router_bias, alpha,
                                                 bc, starts[ci], ci * bc, prevs[ci])
            counts = cnt[:, 0].astype(I32)
            tiles = (counts + tm - 1) // tm
            tile_end = jnp.cumsum(tiles)
            pstart = ((tile_end - tiles) * tm).astype(I32)
            nused = tile_end[-1:].astype(I32)
            blk_e = jnp.minimum(
                jnp.sum(tile_end[None, :] <= jnp.arange(n_rows // tm, dtype=I32)[:, None], axis=1), n_experts - 1
            ).astype(I32)
            dest = _dest_rows(pstart, eidx, rank)
            xs = _dispatch(u2.reshape(n_tok, d // 2), dest[0], dest[1], n_rows)
            st.append((x1, wm, blk_e, nused, dest, xs))
        ys = [_experts(blk_e, nused, xs, w_gate, w_up, w_down, l) for (_, _, blk_e, nused, _, xs) in st]
        yy = [_gather_rows(ys[ci], st[ci][4].reshape(2 * n_tok)) for ci in range(n_chunks)]
        for ci in range(n_chunks):
            x1, wm = st[ci][0], st[ci][1]
            if last:
                out = _combine(x1.reshape(n_tok, d), yy[ci], mod, wm, ln2_g[l], ln2_b[l], seq, alpha,
                               ci * bc, bsz * seq, ci * n_tok, out)
            else:
                chunks[ci] = x1
                starts[ci] = 0
                prevs[ci] = (yy[ci], wm, mod, ln2_g[l], ln2_b[l])
    return out.reshape(bsz, seq, d)
```

```python
import functools

import jax
import jax.numpy as jnp
from jax import lax
from jax.experimental import pallas as pl
from jax.experimental.pallas import tpu as pltpu
from jax.experimental.pallas import tpu_sc as plsc

F32 = jnp.float32
BF16 = jnp.bfloat16
I32 = jnp.int32
U32 = jnp.uint32

CONV_KERNEL = 31
POOL_WINDOWS = (2, 4, 8, 16)
N_EXPERT_GROUPS = 4
LN_EPS = 1e-5

SUBLANES = 8
LANES = 128

SEQ_TILE = 1024
SEQ_SUB_TILE = 256
STAGE_LEAD = 4
EXPERT_TILE = 1024
EXPERT_F_SPLIT = 2
COMBINE_TILE = 1024
BATCH_CHUNKS = 2
CONV_HALO_VREGS = 32
POOL_HALO = 16
VMEM_LIMIT = 56 * 1024 * 1024
SC_CORES = 2
SC_SUBCORES = 16
SC_ROWS_PER_STEP = 64


def _split_bf16(a):
    hi = a.astype(BF16)
    lo = (a - hi.astype(F32)).astype(BF16)
    return hi, lo


def _dot(a, b):
    return jnp.dot(a, b, preferred_element_type=F32)


def _pack_halves(y):
    h = y.shape[1] // 2
    lo = lax.bitcast_convert_type(y[:, :h].astype(BF16).astype(F32), U32)
    hi = lax.bitcast_convert_type(y[:, h:].astype(BF16).astype(F32), U32)
    return (lo >> 16) | hi


def _unpack_halves(p):
    lo = lax.bitcast_convert_type(p << 16, F32)
    hi = lax.bitcast_convert_type(p & jnp.uint32(0xFFFF0000), F32)
    return lo, hi


def _ada_body(c_ref, w_ref, b_ref, o_ref):
    c = c_ref[...]
    ca = c * jax.nn.sigmoid(c)
    chi, clo = _split_bf16(ca)
    whi, wlo = _split_bf16(w_ref[0])
    o_ref[0] = _dot(chi, whi) + _dot(chi, wlo) + _dot(clo, whi) + b_ref[0]


def _ada_mod(c, w_ada, b_ada):
    n_layers, d, n6 = w_ada.shape
    bsz = c.shape[0]
    tn = n6 // 6
    return pl.pallas_call(
        _ada_body,
        grid=(n_layers, n6 // tn),
        in_specs=[
            pl.BlockSpec((bsz, d), lambda l, j: (0, 0)),
            pl.BlockSpec((1, d, tn), lambda l, j: (l, 0, j)),
            pl.BlockSpec((1, 1, tn), lambda l, j: (l, 0, j)),
        ],
        out_specs=pl.BlockSpec((1, bsz, tn), lambda l, j: (l, 0, j)),
        out_shape=jax.ShapeDtypeStruct((n_layers, bsz, n6), F32),
        compiler_params=pltpu.CompilerParams(dimension_semantics=("arbitrary", "arbitrary"), vmem_limit_bytes=VMEM_LIMIT),
        name="ada_mod",
    )(c, w_ada, b_ada.reshape(n_layers, 1, n6))


def _layer_norm_rows(z, g, b, eps=LN_EPS):
    mu = jnp.mean(z, axis=-1, keepdims=True)
    zc = z - mu
    var = jnp.mean(zc * zc, axis=-1, keepdims=True)
    return zc * lax.rsqrt(var + eps) * g + b


def _deepnorm_rows(alpha, x, gate, branch, g, b):
    return _layer_norm_rows(x + (gate * (1.0 / alpha)) * branch, g, b, LN_EPS / (alpha * alpha))


def _combine_rows(alpha, x1, y0, y1, wm, g2, ln_g, ln_b):
    lo0, hi0 = _unpack_halves(y0)
    lo1, hi1 = _unpack_halves(y1)
    w0, w1 = wm[:, 0:1], wm[:, 1:2]
    f = jnp.concatenate([w0 * lo0 + w1 * lo1, w0 * hi0 + w1 * hi1], axis=1)
    return _deepnorm_rows(alpha, x1, 1.0 + g2, f, ln_g, ln_b)


def _mixer_body(alpha, n_experts, sub, fused, *refs):
    if fused:
        xp_ref, y0_ref, y1_ref, wmp_ref, modp_ref, l2g_ref, l2b_ref = refs[:7]
        refs = refs[7:]
        x_ref = None
    else:
        x_ref = refs[0]
        refs = refs[1:]
    (mod_ref, win_ref, bin_ref, cw_ref, cb_ref, cg_ref, cbeta_ref, pw_ref, ps_ref,
     wout_ref, bout_ref, l1g_ref, l1b_ref, wr_ref, rb_ref, perm_ref, permt_ref, tri_ref,
     x1_ref, u2_ref, e_ref, rank_ref, wm_ref, cnt_ref,
     ebuf, cbuf, pbuf, vprev, phalo, carry, xbuf) = refs
    ts = x1_ref.shape[1]
    cw = cg_ref.shape[1]
    pw = ps_ref.shape[1]
    gw = pw // len(POOL_WINDOWS)
    fine = sub // SUBLANES
    halo = CONV_HALO_VREGS
    b = pl.program_id(0)
    i = pl.program_id(1)

    @pl.when(i == 0)
    def _():
        vprev[...] = jnp.zeros(vprev.shape, F32)
        phalo[...] = jnp.zeros(phalo.shape, F32)

    @pl.when(jnp.logical_and(b == 0, i == 0))
    def _():
        carry[...] = jnp.zeros(carry.shape, F32)

    mod = mod_ref[0]
    sh1, sc1, g1 = mod[0:1], mod[1:2], mod[2:3]
    sh2, sc2 = mod[3:4], mod[4:5]

    def stage1(s):
        r0 = s * sub
        if fused:
            x = _combine_rows(alpha, xp_ref[0, r0:r0 + sub, :], y0_ref[r0:r0 + sub, :], y1_ref[r0:r0 + sub, :],
                              wmp_ref[r0:r0 + sub, :], modp_ref[0][5:6], l2g_ref[...], l2b_ref[...])
            xbuf[r0:r0 + sub, :] = x
        else:
            x = x_ref[0, r0:r0 + sub, :]
        u = (x * (1.0 + sc1) + sh1).astype(BF16)
        h = _dot(u, win_ref[...]) + bin_ref[...]
        a = h[:, :cw]
        gate = h[:, cw:2 * cw]
        p = h[:, 2 * cw:]

        v = (a * jax.nn.sigmoid(gate)).astype(BF16)
        vb = _dot(perm_ref[...], v)
        ebuf[s, halo * SUBLANES:, :] = vb
        sl = lax.broadcasted_iota(I32, (SUBLANES, cw), 0)
        for mm in range(halo):
            src = (fine - halo + mm) * SUBLANES
            cur = vb[src:src + SUBLANES, :]
            prev = vprev[mm * SUBLANES:(mm + 1) * SUBLANES, :]
            ebuf[s, mm * SUBLANES:(mm + 1) * SUBLANES, :] = pltpu.roll(
                jnp.where(sl == SUBLANES - 1, prev, cur), 1, 0)
        vprev[...] = vb[(fine - halo) * SUBLANES:, :]
        pbuf[s, 0:POOL_HALO, :] = phalo[...]
        pbuf[s, POOL_HALO:, :] = p
        phalo[...] = p[sub - POOL_HALO:, :]

    def stage2(s):
        r0 = s * sub
        x = xbuf[r0:r0 + sub, :] if fused else x_ref[0, r0:r0 + sub, :]
        chunk = 8 * SUBLANES
        first = halo - (CONV_KERNEL - 1)
        for q0 in range(0, sub, chunk):
            for c0 in range(0, cw, LANES):
                acc = jnp.broadcast_to(cb_ref[:, c0:c0 + LANES], (chunk, LANES))
                for k in range(CONV_KERNEL):
                    off = q0 + (first + k) * SUBLANES
                    acc = acc + cw_ref[k:k + 1, c0:c0 + LANES] * ebuf[s, off:off + chunk, c0:c0 + LANES]
                cbuf[s, q0:q0 + chunk, c0:c0 + LANES] = acc
        yc = _layer_norm_rows(cbuf[s], cg_ref[...], cbeta_ref[...])
        ya_b = (yc * jax.nn.sigmoid(yc)).astype(BF16)
        ya = _dot(permt_ref[...], ya_b).astype(BF16)

        ext = pbuf[s]
        p = ext[POOL_HALO:, :]
        w_sum = ext + pltpu.roll(ext, 1, 0)
        parts = [w_sum[:, :gw]]
        shift = 2
        for _ in range(len(POOL_WINDOWS) - 1):
            w_sum = w_sum[:, gw:]
            w_sum = w_sum + pltpu.roll(w_sum, shift, 0)
            parts.append(w_sum[:, :gw])
            shift *= 2
        wsum = jnp.concatenate(parts, axis=1)[POOL_HALO:, :]
        pos = (lax.broadcasted_iota(I32, (sub, gw), 0) + (i * ts + r0 + 1)).astype(F32)
        cnt = jnp.concatenate([jnp.minimum(pos[:POOL_HALO], float(wlen)) for wlen in POOL_WINDOWS], axis=1)
        inv = jnp.concatenate([jnp.full((1, gw), 1.0 / wlen, F32) for wlen in POOL_WINDOWS], axis=1)
        mean = jnp.concatenate([wsum[:POOL_HALO] / cnt, wsum[POOL_HALO:] * inv], axis=0)
        dpool = (mean - p).astype(BF16)
        yb_parts = [_dot(dpool[:, gi * gw:(gi + 1) * gw], pw_ref[gi]) for gi in range(len(POOL_WINDOWS))]
        yb = (jnp.concatenate(yb_parts, axis=1) * ps_ref[...]).astype(BF16)

        mix = _dot(jnp.concatenate([ya, yb], axis=1), wout_ref[...]) + bout_ref[...]
        x1 = _deepnorm_rows(alpha, x, 1.0 + g1, mix, l1g_ref[...], l1b_ref[...])
        x1_ref[0, r0:r0 + sub, :] = x1
        u2 = x1 * (1.0 + sc2) + sh2
        u2_ref[0, r0:r0 + sub, :] = _pack_halves(u2)

        logits = _dot(u2.astype(BF16), wr_ref[...]).T[0:n_experts, :]
        scores = jax.nn.sigmoid(logits)
        sel = scores + rb_ref[...]
        epg = n_experts // N_EXPERT_GROUPS
        io = lax.broadcasted_iota(I32, (epg, sub), 0).astype(F32)
        neg = jnp.float32(-jnp.inf)
        best = None
        for g in range(N_EXPERT_GROUPS):
            sg = sel[g * epg:(g + 1) * epg, :]
            scg = scores[g * epg:(g + 1) * epg, :]
            m1 = jnp.max(sg, axis=0, keepdims=True)
            i1 = jnp.min(jnp.where(sg == m1, io, float(epg)), axis=0, keepdims=True)
            rest = jnp.where(io == i1, neg, sg)
            m2 = jnp.max(rest, axis=0, keepdims=True)
            i2 = jnp.min(jnp.where(jnp.logical_and(rest == m2, io != i1), io, float(epg)), axis=0, keepdims=True)
            s1 = jnp.sum(jnp.where(io == i1, scg, 0.0), axis=0, keepdims=True)
            s2 = jnp.sum(jnp.where(io == i2, scg, 0.0), axis=0, keepdims=True)
            gs = m1 + m2
            cand = (gs, i1 + float(g * epg), i2 + float(g * epg), s1, s2)
            if best is None:
                best = cand
            else:
                upd = gs > best[0]
                best = tuple(jnp.where(upd, cn, bs) for cn, bs in zip(cand, best))
        _, e1, e2, s1, s2 = best
        denom = s1 + s2
        w1 = s1 / denom
        w2 = s2 / denom

        ioe = lax.broadcasted_iota(I32, (n_experts, sub), 0).astype(F32)
        hit1 = ioe == e1
        hit2 = ioe == e2
        onehot = jnp.where(jnp.logical_or(hit1, hit2), 1.0, 0.0)
        before = _dot(onehot.astype(BF16), tri_ref[...]) + carry[:, 0:1]
        r1 = jnp.sum(jnp.where(hit1, before, 0.0), axis=0, keepdims=True)
        r2 = jnp.sum(jnp.where(hit2, before, 0.0), axis=0, keepdims=True)
        carry[...] = carry[...] + jnp.sum(onehot, axis=1, keepdims=True)

        e_ref[0:1, r0:r0 + sub] = e1.astype(I32)
        e_ref[1:2, r0:r0 + sub] = e2.astype(I32)
        rank_ref[0:1, r0:r0 + sub] = r1.astype(I32)
        rank_ref[1:2, r0:r0 + sub] = r2.astype(I32)
        rowi = lax.broadcasted_iota(I32, (LANES, sub), 0)
        wpad = jnp.where(rowi == 0, w1, jnp.where(rowi == 1, w2, 0.0))
        wm_ref[r0:r0 + sub, :] = wpad.T

    nsub = ts // sub
    for s in range(min(STAGE_LEAD, nsub)):
        stage1(s)
    for s in range(nsub):
        if s + STAGE_LEAD < nsub:
            stage1(s + STAGE_LEAD)
        stage2(s)
    cnt_ref[...] = carry[...]


def _row_perm(n):
    r = jnp.arange(n)
    t = (n // SUBLANES) * (r % SUBLANES) + r // SUBLANES
    return (t[:, None] == jnp.arange(n)[None, :]).astype(BF16)


def _mixer(x, mod, lw, wr, rbias, alpha, bsz, x_b0, mod_b0, prev=None):
    _, seq, d = x.shape
    ts = min(SEQ_TILE, seq)
    sub = min(SEQ_SUB_TILE, ts)
    nsub = ts // sub
    ns = seq // ts
    n_tok = bsz * seq
    n_experts = rbias.shape[0]
    cw = lw["conv_w"].shape[1]
    pw = lw["pool_scale"].shape[0]
    perm = _row_perm(sub)
    tri = (jnp.arange(sub)[:, None] < jnp.arange(sub)[None, :]).astype(BF16)
    row = lambda a: a.reshape(1, -1)
    full = lambda shape: pl.BlockSpec(shape, lambda b, i: (0,) * len(shape))
    if prev is None:
        ins = [(x, pl.BlockSpec((1, ts, d), lambda b, i: (b + x_b0, i, 0)))]
    else:
        yy, wmp, modp, l2g, l2b = prev
        nblk = n_tok // ts
        ins = [
            (x, pl.BlockSpec((1, ts, d), lambda b, i: (b + x_b0, i, 0))),
            (yy, pl.BlockSpec((ts, d // 2), lambda b, i: (b * ns + i, 0))),
            (yy, pl.BlockSpec((ts, d // 2), lambda b, i: (nblk + b * ns + i, 0))),
            (wmp, pl.BlockSpec((ts, LANES), lambda b, i: (b * ns + i, 0))),
            (modp, pl.BlockSpec((1, 6, d), lambda b, i: (b + mod_b0, 0, 0))),
            (row(l2g), None), (row(l2b), None),
        ]
    ins += [
        (mod, pl.BlockSpec((1, 6, d), lambda b, i: (b + mod_b0, 0, 0))),
        (lw["w_in"], None), (row(lw["b_in"]), None),
        (jnp.pad(lw["conv_w"], ((0, 1), (0, 0))), None), (row(lw["conv_b"]), None),
        (row(lw["conv_ln_g"]), None), (row(lw["conv_ln_b"]), None),
        (lw["pool_w"], None), (row(lw["pool_scale"]), None),
        (lw["w_out"], None), (row(lw["b_out"]), None),
        (row(lw["ln1_g"]), None), (row(lw["ln1_b"]), None),
        (wr, None), (rbias.reshape(-1, 1), None),
        (perm, None), (perm.T, None), (tri, None),
    ]
    args = [a for a, _ in ins]
    specs = [s if s is not None else full(a.shape) for a, s in ins]
    out_shape = (
        jax.ShapeDtypeStruct((bsz, seq, d), F32),
        jax.ShapeDtypeStruct((bsz, seq, d // 2), U32),
        jax.ShapeDtypeStruct((2, n_tok), I32),
        jax.ShapeDtypeStruct((2, n_tok), I32),
        jax.ShapeDtypeStruct((n_tok, LANES), F32),
        jax.ShapeDtypeStruct((n_experts, LANES), F32),
    )
    out_specs = (
        pl.BlockSpec((1, ts, d), lambda b, i: (b, i, 0)),
        pl.BlockSpec((1, ts, d // 2), lambda b, i: (b, i, 0)),
        pl.BlockSpec((2, ts), lambda b, i: (0, b * ns + i)),
        pl.BlockSpec((2, ts), lambda b, i: (0, b * ns + i)),
        pl.BlockSpec((ts, LANES), lambda b, i: (b * ns + i, 0)),
        pl.BlockSpec((n_experts, LANES), lambda b, i: (0, 0)),
    )
    return pl.pallas_call(
        functools.partial(_mixer_body, alpha, n_experts, sub, prev is not None),
        grid=(bsz, ns),
        in_specs=specs,
        out_specs=out_specs,
        out_shape=out_shape,
        scratch_shapes=[
            pltpu.VMEM((nsub, CONV_HALO_VREGS * SUBLANES + sub, cw), F32),
            pltpu.VMEM((nsub, sub, cw), F32),
            pltpu.VMEM((nsub, POOL_HALO + sub, pw), F32),
            pltpu.VMEM((CONV_HALO_VREGS * SUBLANES, cw), F32),
            pltpu.VMEM((POOL_HALO, pw), F32),
            pltpu.VMEM((n_experts, LANES), F32),
            pltpu.VMEM((ts, d) if prev is not None else (SUBLANES, LANES), F32),
        ],
        compiler_params=pltpu.CompilerParams(dimension_semantics=("arbitrary", "arbitrary"), vmem_limit_bytes=VMEM_LIMIT),
        name="mixer_router",
    )(*args)


def _dest_body(n_experts, pstart_ref, e_ref, rank_ref, dest_ref):
    e = e_ref[...]
    dest = rank_ref[...]
    for ex in range(n_experts):
        dest = dest + jnp.where(e == ex, pstart_ref[ex], 0)
    dest_ref[...] = dest


def _dest_rows(pstart, eidx, rank):
    n_tok = eidx.shape[1]
    tn = min(8192, n_tok)
    grid_spec = pltpu.PrefetchScalarGridSpec(
        num_scalar_prefetch=1,
        grid=(n_tok // tn,),
        in_specs=[pl.BlockSpec((2, tn), lambda i, ps: (0, i)), pl.BlockSpec((2, tn), lambda i, ps: (0, i))],
        out_specs=pl.BlockSpec((2, tn), lambda i, ps: (0, i)),
    )
    return pl.pallas_call(
        functools.partial(_dest_body, pstart.shape[0]),
        grid_spec=grid_spec,
        out_shape=jax.ShapeDtypeStruct((2, n_tok), I32),
        compiler_params=pltpu.CompilerParams(dimension_semantics=("arbitrary",)),
        name="dest_rows",
    )(pstart, eidx, rank)


def _sc_mesh():
    return plsc.VectorSubcoreMesh(core_axis_name="c", subcore_axis_name="s",
                                  num_cores=SC_CORES, num_subcores=SC_SUBCORES)


def _dispatch(u2, d0, d1, n_rows):
    n_tok, d = u2.shape
    workers = SC_CORES * SC_SUBCORES
    per_w = n_tok // workers
    ch = SC_ROWS_PER_STEP
    nch = per_w // ch

    @functools.partial(
        pl.kernel, mesh=_sc_mesh(),
        out_type=jax.ShapeDtypeStruct((n_rows, d), u2.dtype),
        scratch_types=[pltpu.VMEM((nch, ch), I32), pltpu.VMEM((nch, ch), I32), pltpu.VMEM((2, ch, d), u2.dtype),
                       pltpu.SemaphoreType.DMA((2,)), pltpu.SemaphoreType.DMA((2,))],
        name="sc_dispatch",
    )
    def run(u_hbm, d0_hbm, d1_hbm, xs_hbm, i0, i1, rows, rsem, ssem):
        wid = lax.axis_index("s") * SC_CORES + lax.axis_index("c")
        base = wid * per_w
        pltpu.sync_copy(d0_hbm.at[pl.ds(wid * nch, nch)], i0)
        pltpu.sync_copy(d1_hbm.at[pl.ds(wid * nch, nch)], i1)

        def read(j, b):
            return pltpu.make_async_copy(u_hbm.at[pl.ds(base + j * ch, ch)], rows.at[b], rsem.at[b])

        def scatter(idx, j, b):
            return pltpu.make_async_copy(rows.at[b], xs_hbm.at[idx.at[j]], ssem.at[b])

        read(0, 0).start()

        @pl.loop(0, nch, step=2)
        def _(j0):
            for b in range(2):
                j = j0 + b
                read(j, b).wait()
                scatter(i0, j, b).start()
                scatter(i1, j, b).start()

                @pl.when(j >= 1)
                def _():
                    scatter(i0, j - 1, 1 - b).wait()
                    scatter(i1, j - 1, 1 - b).wait()

                @pl.when(j + 1 < nch)
                def _():
                    read(j + 1, 1 - b).start()

        scatter(i0, nch - 1, 1).wait()
        scatter(i1, nch - 1, 1).wait()

    return run(u2, d0.reshape(n_tok // ch, ch), d1.reshape(n_tok // ch, ch))


def _gather_rows(ys, dd):
    n_out = dd.shape[0]
    d = ys.shape[1]
    workers = SC_CORES * SC_SUBCORES
    per_w = n_out // workers
    ch = SC_ROWS_PER_STEP
    nch = per_w // ch

    @functools.partial(
        pl.kernel, mesh=_sc_mesh(),
        out_type=jax.ShapeDtypeStruct((n_out, d), ys.dtype),
        scratch_types=[pltpu.VMEM((nch, ch), I32), pltpu.VMEM((2, ch, d), ys.dtype),
                       pltpu.SemaphoreType.DMA((2,)), pltpu.SemaphoreType.DMA((2,))],
        name="sc_gather",
    )
    def run(ys_hbm, dd_hbm, out_hbm, idx, rows, gsem, wsem):
        wid = lax.axis_index("s") * SC_CORES + lax.axis_index("c")
        base = wid * per_w
        pltpu.sync_copy(dd_hbm.at[pl.ds(wid * nch, nch)], idx)

        def gather(j, b):
            return pltpu.make_async_copy(ys_hbm.at[idx.at[j]], rows.at[b], gsem.at[b])

        def write(j, b):
            return pltpu.make_async_copy(rows.at[b], out_hbm.at[pl.ds(base + j * ch, ch)], wsem.at[b])

        gather(0, 0).start()

        @pl.loop(0, nch, step=2)
        def _(j0):
            for b in range(2):
                j = j0 + b
                gather(j, b).wait()
                write(j, b).start()

                @pl.when(j >= 1)
                def _():
                    write(j - 1, 1 - b).wait()

                @pl.when(j + 1 < nch)
                def _():
                    gather(j + 1, 1 - b).start()

        write(nch - 1, 1).wait()

    return run(ys, dd.reshape(n_out // ch, ch))


def _expert_body(blk_ref, nused_ref, xs_ref, wg_ref, wu_ref, wd_ref, ys_ref, wgb, wub, wdb):
    j = pl.program_id(0)
    used = j < nused_ref[0]
    new_expert = jnp.logical_or(j == 0, blk_ref[j] != blk_ref[jnp.maximum(j - 1, 0)])

    @pl.when(jnp.logical_and(used, new_expert))
    def _():
        wgb[...] = wg_ref[0, 0].astype(BF16)
        wub[...] = wu_ref[0, 0].astype(BF16)
        wdb[...] = wd_ref[0, 0].astype(BF16)

    @pl.when(used)
    def _():
        lo, hi = _unpack_halves(xs_ref[...])
        lo = lo.astype(BF16)
        hi = hi.astype(BF16)
        half = lo.shape[1]
        fw = wgb.shape[1] // EXPERT_F_SPLIT
        acc = None
        for c in range(EXPERT_F_SPLIT):
            cs = slice(c * fw, (c + 1) * fw)
            g = _dot(lo, wgb[0:half, cs]) + _dot(hi, wgb[half:, cs])
            up = _dot(lo, wub[0:half, cs]) + _dot(hi, wub[half:, cs])
            hid = (g * jax.nn.sigmoid(g) * up).astype(BF16)
            part = _dot(hid, wdb[cs, :])
            acc = part if acc is None else acc + part
        ys_ref[...] = _pack_halves(acc)

    @pl.when(j >= nused_ref[0])
    def _():
        ys_ref[...] = jnp.zeros(ys_ref.shape, U32)


def _experts(blk_e, nused, xs, wg, wu, wd, layer):
    n_rows = xs.shape[0]
    d = wg.shape[2]
    f = wg.shape[3]
    tm = EXPERT_TILE
    grid_spec = pltpu.PrefetchScalarGridSpec(
        num_scalar_prefetch=2,
        grid=(n_rows // tm,),
        in_specs=[
            pl.BlockSpec((tm, d // 2), lambda j, be, nu: (jnp.minimum(j, nu[0] - 1), 0)),
            pl.BlockSpec((1, 1, d, f), lambda j, be, nu: (layer, be[j], 0, 0)),
            pl.BlockSpec((1, 1, d, f), lambda j, be, nu: (layer, be[j], 0, 0)),
            pl.BlockSpec((1, 1, f, d), lambda j, be, nu: (layer, be[j], 0, 0)),
        ],
        out_specs=pl.BlockSpec((tm, d // 2), lambda j, be, nu: (j, 0)),
        scratch_shapes=[pltpu.VMEM((d, f), BF16), pltpu.VMEM((d, f), BF16), pltpu.VMEM((f, d), BF16)],
    )
    return pl.pallas_call(
        _expert_body,
        grid_spec=grid_spec,
        out_shape=jax.ShapeDtypeStruct((n_rows, d // 2), U32),
        compiler_params=pltpu.CompilerParams(dimension_semantics=("arbitrary",), vmem_limit_bytes=VMEM_LIMIT),
        name="expert_ffn",
    )(blk_e, nused, xs, wg, wu, wd)


def _combine_body(alpha, x1_ref, y0_ref, y1_ref, mod_ref, wm_ref, g_ref, b_ref, *rest):
    o_ref = rest[-1]
    o_ref[...] = _combine_rows(alpha, x1_ref[...], y0_ref[...], y1_ref[...], wm_ref[...],
                               mod_ref[0][5:6], g_ref[...], b_ref[...])


def _combine(x1, yy, mod, wm, ln_g, ln_b, seq, alpha, mod_b0, out_rows, out_row0, out_prev):
    n_tok, d = x1.shape
    tn = min(COMBINE_TILE, seq)
    per_seq = seq // tn
    nblk = n_tok // tn
    blk0 = out_row0 // tn
    rows = lambda: pl.BlockSpec((tn, d), lambda i: (i, 0))
    in_specs = [rows(), pl.BlockSpec((tn, d // 2), lambda i: (i, 0)),
                pl.BlockSpec((tn, d // 2), lambda i: (nblk + i, 0)),
                pl.BlockSpec((1, 6, d), lambda i: (i // per_seq + mod_b0, 0, 0)),
                pl.BlockSpec((tn, LANES), lambda i: (i, 0)),
                pl.BlockSpec((1, d), lambda i: (0, 0)),
                pl.BlockSpec((1, d), lambda i: (0, 0))]
    args = [x1, yy, yy, mod, wm, ln_g.reshape(1, d), ln_b.reshape(1, d)]
    aliases = {}
    if out_prev is not None:
        in_specs.append(pl.BlockSpec(memory_space=pl.ANY))
        args.append(out_prev)
        aliases = {len(args) - 1: 0}
    return pl.pallas_call(
        functools.partial(_combine_body, alpha),
        grid=(nblk,),
        in_specs=in_specs,
        out_specs=pl.BlockSpec((tn, d), lambda i: (blk0 + i, 0)),
        out_shape=jax.ShapeDtypeStruct((out_rows, d), F32),
        input_output_aliases=aliases,
        compiler_params=pltpu.CompilerParams(dimension_semantics=("arbitrary",), vmem_limit_bytes=VMEM_LIMIT),
        name="combine_ln",
    )(*args)


def kernel(x, c, w_ada, b_ada, w_in, b_in, conv_w, conv_b, conv_ln_g, conv_ln_b, pool_w, pool_scale, w_out, b_out, ln1_g, ln1_b, w_router, router_bias, w_gate, w_up, w_down, ln2_g, ln2_b):
    bsz, seq, d = x.shape
    depth = w_ada.shape[0]
    n_experts = w_router.shape[1]
    alpha = float((2 * depth) ** 0.25)
    tm = EXPERT_TILE
    n_chunks = BATCH_CHUNKS if bsz % BATCH_CHUNKS == 0 else 1
    bc = bsz // n_chunks
    n_tok = bc * seq
    n_rows = 2 * n_tok + n_experts * tm

    mod_all = _ada_mod(c, w_ada, b_ada).reshape(depth, bsz, 6, d)
    wr = jnp.pad(w_router.astype(BF16), ((0, 0), (0, LANES - n_experts)))

    chunks = [x] * n_chunks
    starts = [ci * bc for ci in range(n_chunks)]
    prevs = [None] * n_chunks
    out = None
    for l in range(depth):
        lw = dict(w_in=w_in[l].astype(BF16), b_in=b_in[l], conv_w=conv_w[l], conv_b=conv_b[l],
                  conv_ln_g=conv_ln_g[l], conv_ln_b=conv_ln_b[l], pool_w=pool_w[l].astype(BF16),
                  pool_scale=pool_scale[l], w_out=w_out[l].astype(BF16), b_out=b_out[l],
                  ln1_g=ln1_g[l], ln1_b=ln1_b[l])
        mod = mod_all[l]
        last = l == depth - 1
        st = []
        for ci in range(n_chunks):
            x1, u2, eidx, rank, wm, cnt = _mixer(chunks[ci], mod, lw, wr, router_bias, alpha,
                                                 bc, starts[ci], ci * bc, prevs[ci])
            counts = cnt[:, 0].astype(I32)
            tiles = (counts + tm - 1) // tm
            tile_end = jnp.cumsum(tiles)
            pstart = ((tile_end - tiles) * tm).astype(I32)
            nused = tile_end[-1:].astype(I32)
            blk_e = jnp.minimum(
                jnp.sum(tile_end[None, :] <= jnp.arange(n_rows // tm, dtype=I32)[:, None], axis=1), n_experts - 1
            ).astype(I32)
            dest = _dest_rows(pstart, eidx, rank)
            xs = _dispatch(u2.reshape(n_tok, d // 2), dest[0], dest[1], n_rows)
            st.append((x1, wm, blk_e, nused, dest, xs))
        ys = [_experts(blk_e, nused, xs, w_gate, w_up, w_down, l) for (_, _, blk_e, nused, _, xs) in st]
        yy = [_gather_rows(ys[ci], st[ci][4].reshape(2 * n_tok)) for ci in range(n_chunks)]
        for ci in range(n_chunks):
            x1, wm = st[ci][0], st[ci][1]
            if last:
                out = _combine(x1.reshape(n_tok, d), yy[ci], mod, wm, ln2_g[l], ln2_b[l], seq, alpha,
                               ci * bc, bsz * seq, ci * n_tok, out)
            else:
                chunks[ci] = x1
                starts[ci] = 0
                prevs[ci] = (yy[ci], wm, mod, ln2_g[l], ln2_b[l])
    return out.reshape(bsz, seq, d)
```

```python
import functools

import jax
import jax.numpy as jnp
from jax import lax
from jax.experimental import pallas as pl
from jax.experimental.pallas import tpu as pltpu
from jax.experimental.pallas import tpu_sc as plsc

F32 = jnp.float32
BF16 = jnp.bfloat16
I32 = jnp.int32
U32 = jnp.uint32

CONV_KERNEL = 31
POOL_WINDOWS = (2, 4, 8, 16)
N_EXPERT_GROUPS = 4
LN_EPS = 1e-5

SUBLANES = 8
LANES = 128

SEQ_TILE = 1024
SEQ_SUB_TILE = 256
STAGE_LEAD = 4
EXPERT_TILE = 1024
EXPERT_F_SPLIT = 2
COMBINE_TILE = 1024
BATCH_CHUNKS = 2
CONV_HALO_VREGS = 32
POOL_HALO = 16
VMEM_LIMIT = 56 * 1024 * 1024
SC_CORES = 1
SC_SUBCORES = 16
SC_ROWS_PER_STEP = 64


def _split_bf16(a):
    hi = a.astype(BF16)
    lo = (a - hi.astype(F32)).astype(BF16)
    return hi, lo


def _dot(a, b):
    return jnp.dot(a, b, preferred_element_type=F32)


def _pack_halves(y):
    h = y.shape[1] // 2
    lo = lax.bitcast_convert_type(y[:, :h].astype(BF16).astype(F32), U32)
    hi = lax.bitcast_convert_type(y[:, h:].astype(BF16).astype(F32), U32)
    return (lo >> 16) | hi


def _unpack_halves(p):
    lo = lax.bitcast_convert_type(p << 16, F32)
    hi = lax.bitcast_convert_type(p & jnp.uint32(0xFFFF0000), F32)
    return lo, hi


def _ada_body(c_ref, w_ref, b_ref, o_ref):
    c = c_ref[...]
    ca = c * jax.nn.sigmoid(c)
    chi, clo = _split_bf16(ca)
    whi, wlo = _split_bf16(w_ref[0])
    o_ref[0] = _dot(chi, whi) + _dot(chi, wlo) + _dot(clo, whi) + b_ref[0]


def _ada_mod(c, w_ada, b_ada):
    n_layers, d, n6 = w_ada.shape
    bsz = c.shape[0]
    tn = n6 // 6
    return pl.pallas_call(
        _ada_body,
        grid=(n_layers, n6 // tn),
        in_specs=[
            pl.BlockSpec((bsz, d), lambda l, j: (0, 0)),
            pl.BlockSpec((1, d, tn), lambda l, j: (l, 0, j)),
            pl.BlockSpec((1, 1, tn), lambda l, j: (l, 0, j)),
        ],
        out_specs=pl.BlockSpec((1, bsz, tn), lambda l, j: (l, 0, j)),
        out_shape=jax.ShapeDtypeStruct((n_layers, bsz, n6), F32),
        compiler_params=pltpu.CompilerParams(dimension_semantics=("arbitrary", "arbitrary"), vmem_limit_bytes=VMEM_LIMIT),
        name="ada_mod",
    )(c, w_ada, b_ada.reshape(n_layers, 1, n6))


def _layer_norm_rows(z, g, b, eps=LN_EPS):
    mu = jnp.mean(z, axis=-1, keepdims=True)
    zc = z - mu
    var = jnp.mean(zc * zc, axis=-1, keepdims=True)
    return zc * lax.rsqrt(var + eps) * g + b


def _deepnorm_rows(alpha, x, gate, branch, g, b):
    return _layer_norm_rows(x + (gate * (1.0 / alpha)) * branch, g, b, LN_EPS / (alpha * alpha))


def _combine_rows(alpha, x1, y0, y1, wm, g2, ln_g, ln_b):
    lo0, hi0 = _unpack_halves(y0)
    lo1, hi1 = _unpack_halves(y1)
    w0, w1 = wm[:, 0:1], wm[:, 1:2]
    f = jnp.concatenate([w0 * lo0 + w1 * lo1, w0 * hi0 + w1 * hi1], axis=1)
    return _deepnorm_rows(alpha, x1, 1.0 + g2, f, ln_g, ln_b)


def _mixer_body(alpha, n_experts, sub, fused, *refs):
    if fused:
        xp_ref, y0_ref, y1_ref, wmp_ref, modp_ref, l2g_ref, l2b_ref = refs[:7]
        refs = refs[7:]
        x_ref = None
    else:
        x_ref = refs[0]
        refs = refs[1:]
    (mod_ref, win_ref, bin_ref, cw_ref, cb_ref, cg_ref, cbeta_ref, pw_ref, ps_ref,
     wout_ref, bout_ref, l1g_ref, l1b_ref, wr_ref, rb_ref, perm_ref, permt_ref, tri_ref,
     x1_ref, u2_ref, e_ref, rank_ref, wm_ref, cnt_ref,
     ebuf, cbuf, pbuf, vprev, phalo, carry, xbuf) = refs
    ts = x1_ref.shape[1]
    cw = cg_ref.shape[1]
    pw = ps_ref.shape[1]
    gw = pw // len(POOL_WINDOWS)
    fine = sub // SUBLANES
    halo = CONV_HALO_VREGS
    b = pl.program_id(0)
    i = pl.program_id(1)

    @pl.when(i == 0)
    def _():
        vprev[...] = jnp.zeros(vprev.shape, F32)
        phalo[...] = jnp.zeros(phalo.shape, F32)

    @pl.when(jnp.logical_and(b == 0, i == 0))
    def _():
        carry[...] = jnp.zeros(carry.shape, F32)

    mod = mod_ref[0]
    sh1, sc1, g1 = mod[0:1], mod[1:2], mod[2:3]
    sh2, sc2 = mod[3:4], mod[4:5]

    def stage1(s):
        r0 = s * sub
        if fused:
            x = _combine_rows(alpha, xp_ref[0, r0:r0 + sub, :], y0_ref[r0:r0 + sub, :], y1_ref[r0:r0 + sub, :],
                              wmp_ref[r0:r0 + sub, :], modp_ref[0][5:6], l2g_ref[...], l2b_ref[...])
            xbuf[r0:r0 + sub, :] = x
        else:
            x = x_ref[0, r0:r0 + sub, :]
        u = (x * (1.0 + sc1) + sh1).astype(BF16)
        h = _dot(u, win_ref[...]) + bin_ref[...]
        a = h[:, :cw]
        gate = h[:, cw:2 * cw]
        p = h[:, 2 * cw:]

        v = (a * jax.nn.sigmoid(gate)).astype(BF16)
        vb = _dot(perm_ref[...], v)
        ebuf[s, halo * SUBLANES:, :] = vb
        sl = lax.broadcasted_iota(I32, (SUBLANES, cw), 0)
        for mm in range(halo):
            src = (fine - halo + mm) * SUBLANES
            cur = vb[src:src + SUBLANES, :]
            prev = vprev[mm * SUBLANES:(mm + 1) * SUBLANES, :]
            ebuf[s, mm * SUBLANES:(mm + 1) * SUBLANES, :] = pltpu.roll(
                jnp.where(sl == SUBLANES - 1, prev, cur), 1, 0)
        vprev[...] = vb[(fine - halo) * SUBLANES:, :]
        pbuf[s, 0:POOL_HALO, :] = phalo[...]
        pbuf[s, POOL_HALO:, :] = p
        phalo[...] = p[sub - POOL_HALO:, :]

    def stage2(s):
        r0 = s * sub
        x = xbuf[r0:r0 + sub, :] if fused else x_ref[0, r0:r0 + sub, :]
        chunk = 8 * SUBLANES
        first = halo - (CONV_KERNEL - 1)
        for q0 in range(0, sub, chunk):
            for c0 in range(0, cw, LANES):
                acc = jnp.broadcast_to(cb_ref[:, c0:c0 + LANES], (chunk, LANES))
                for k in range(CONV_KERNEL):
                    off = q0 + (first + k) * SUBLANES
                    acc = acc + cw_ref[k:k + 1, c0:c0 + LANES] * ebuf[s, off:off + chunk, c0:c0 + LANES]
                cbuf[s, q0:q0 + chunk, c0:c0 + LANES] = acc
        yc = _layer_norm_rows(cbuf[s], cg_ref[...], cbeta_ref[...])
        ya_b = (yc * jax.nn.sigmoid(yc)).astype(BF16)
        ya = _dot(permt_ref[...], ya_b).astype(BF16)

        ext = pbuf[s]
        p = ext[POOL_HALO:, :]
        w_sum = ext + pltpu.roll(ext, 1, 0)
        parts = [w_sum[:, :gw]]
        shift = 2
        for _ in range(len(POOL_WINDOWS) - 1):
            w_sum = w_sum[:, gw:]
            w_sum = w_sum + pltpu.roll(w_sum, shift, 0)
            parts.append(w_sum[:, :gw])
            shift *= 2
        wsum = jnp.concatenate(parts, axis=1)[POOL_HALO:, :]
        pos = (lax.broadcasted_iota(I32, (sub, gw), 0) + (i * ts + r0 + 1)).astype(F32)
        cnt = jnp.concatenate([jnp.minimum(pos[:POOL_HALO], float(wlen)) for wlen in POOL_WINDOWS], axis=1)
        inv = jnp.concatenate([jnp.full((1, gw), 1.0 / wlen, F32) for wlen in POOL_WINDOWS], axis=1)
        mean = jnp.concatenate([wsum[:POOL_HALO] / cnt, wsum[POOL_HALO:] * inv], axis=0)
        dpool = (mean - p).astype(BF16)
        yb_parts = [_dot(dpool[:, gi * gw:(gi + 1) * gw], pw_ref[gi]) for gi in range(len(POOL_WINDOWS))]
        yb = (jnp.concatenate(yb_parts, axis=1) * ps_ref[...]).astype(BF16)

        mix = _dot(jnp.concatenate([ya, yb], axis=1), wout_ref[...]) + bout_ref[...]
        x1 = _deepnorm_rows(alpha, x, 1.0 + g1, mix, l1g_ref[...], l1b_ref[...])
        x1_ref[0, r0:r0 + sub, :] = x1
        u2 = x1 * (1.0 + sc2) + sh2
        u2_ref[0, r0:r0 + sub, :] = _pack_halves(u2)

        logits = _dot(u2.astype(BF16), wr_ref[...]).T[0:n_experts, :]
        scores = jax.nn.sigmoid(logits)
        sel = scores + rb_ref[...]
        epg = n_experts // N_EXPERT_GROUPS
        io = lax.broadcasted_iota(I32, (epg, sub), 0).astype(F32)
        neg = jnp.float32(-jnp.inf)
        best = None
        for g in range(N_EXPERT_GROUPS):
            sg = sel[g * epg:(g + 1) * epg, :]
            scg = scores[g * epg:(g + 1) * epg, :]
            m1 = jnp.max(sg, axis=0, keepdims=True)
            i1 = jnp.min(jnp.where(sg == m1, io, float(epg)), axis=0, keepdims=True)
            rest = jnp.where(io == i1, neg, sg)
            m2 = jnp.max(rest, axis=0, keepdims=True)
            i2 = jnp.min(jnp.where(jnp.logical_and(rest == m2, io != i1), io, float(epg)), axis=0, keepdims=True)
            s1 = jnp.sum(jnp.where(io == i1, scg, 0.0), axis=0, keepdims=True)
            s2 = jnp.sum(jnp.where(io == i2, scg, 0.0), axis=0, keepdims=True)
            gs = m1 + m2
            cand = (gs, i1 + float(g * epg), i2 + float(g * epg), s1, s2)
            if best is None:
                best = cand
            else:
                upd = gs > best[0]
                best = tuple(jnp.where(upd, cn, bs) for cn, bs in zip(cand, best))
        _, e1, e2, s1, s2 = best
        denom = s1 + s2
        w1 = s1 / denom
        w2 = s2 / denom

        ioe = lax.broadcasted_iota(I32, (n_experts, sub), 0).astype(F32)
        hit1 = ioe == e1
        hit2 = ioe == e2
        onehot = jnp.where(jnp.logical_or(hit1, hit2), 1.0, 0.0)
        before = _dot(onehot.astype(BF16), tri_ref[...]) + carry[:, 0:1]
        r1 = jnp.sum(jnp.where(hit1, before, 0.0), axis=0, keepdims=True)
        r2 = jnp.sum(jnp.where(hit2, before, 0.0), axis=0, keepdims=True)
        carry[...] = carry[...] + jnp.sum(onehot, axis=1, keepdims=True)

        e_ref[0:1, r0:r0 + sub] = e1.astype(I32)
        e_ref[1:2, r0:r0 + sub] = e2.astype(I32)
        rank_ref[0:1, r0:r0 + sub] = r1.astype(I32)
        rank_ref[1:2, r0:r0 + sub] = r2.astype(I32)
        rowi = lax.broadcasted_iota(I32, (LANES, sub), 0)
        wpad = jnp.where(rowi == 0, w1, jnp.where(rowi == 1, w2, 0.0))
        wm_ref[r0:r0 + sub, :] = wpad.T

    nsub = ts // sub
    for s in range(min(STAGE_LEAD, nsub)):
        stage1(s)
    for s in range(nsub):
        if s + STAGE_LEAD < nsub:
            stage1(s + STAGE_LEAD)
        stage2(s)
    cnt_ref[...] = carry[...]


def _row_perm(n):
    r = jnp.arange(n)
    t = (n // SUBLANES) * (r % SUBLANES) + r // SUBLANES
    return (t[:, None] == jnp.arange(n)[None, :]).astype(BF16)


def _mixer(x, mod, lw, wr, rbias, alpha, bsz, x_b0, mod_b0, prev=None):
    _, seq, d = x.shape
    ts = min(SEQ_TILE, seq)
    sub = min(SEQ_SUB_TILE, ts)
    nsub = ts // sub
    ns = seq // ts
    n_tok = bsz * seq
    n_experts = rbias.shape[0]
    cw = lw["conv_w"].shape[1]
    pw = lw["pool_scale"].shape[0]
    perm = _row_perm(sub)
    tri = (jnp.arange(sub)[:, None] < jnp.arange(sub)[None, :]).astype(BF16)
    row = lambda a: a.reshape(1, -1)
    full = lambda shape: pl.BlockSpec(shape, lambda b, i: (0,) * len(shape))
    if prev is None:
        ins = [(x, pl.BlockSpec((1, ts, d), lambda b, i: (b + x_b0, i, 0)))]
    else:
        yy, wmp, modp, l2g, l2b = prev
        nblk = n_tok // ts
        ins = [
            (x, pl.BlockSpec((1, ts, d), lambda b, i: (b + x_b0, i, 0))),
            (yy, pl.BlockSpec((ts, d // 2), lambda b, i: (b * ns + i, 0))),
            (yy, pl.BlockSpec((ts, d // 2), lambda b, i: (nblk + b * ns + i, 0))),
            (wmp, pl.BlockSpec((ts, LANES), lambda b, i: (b * ns + i, 0))),
            (modp, pl.BlockSpec((1, 6, d), lambda b, i: (b + mod_b0, 0, 0))),
            (row(l2g), None), (row(l2b), None),
        ]
    ins += [
        (mod, pl.BlockSpec((1, 6, d), lambda b, i: (b + mod_b0, 0, 0))),
        (lw["w_in"], None), (row(lw["b_in"]), None),
        (jnp.pad(lw["conv_w"], ((0, 1), (0, 0))), None), (row(lw["conv_b"]), None),
        (row(lw["conv_ln_g"]), None), (row(lw["conv_ln_b"]), None),
        (lw["pool_w"], None), (row(lw["pool_scale"]), None),
        (lw["w_out"], None), (row(lw["b_out"]), None),
        (row(lw["ln1_g"]), None), (row(lw["ln1_b"]), None),
        (wr, None), (rbias.reshape(-1, 1), None),
        (perm, None), (perm.T, None), (tri, None),
    ]
    args = [a for a, _ in ins]
    specs = [s if s is not None else full(a.shape) for a, s in ins]
    out_shape = (
        jax.ShapeDtypeStruct((bsz, seq, d), F32),
        jax.ShapeDtypeStruct((bsz, seq, d // 2), U32),
        jax.ShapeDtypeStruct((2, n_tok), I32),
        jax.ShapeDtypeStruct((2, n_tok), I32),
        jax.ShapeDtypeStruct((n_tok, LANES), F32),
        jax.ShapeDtypeStruct((n_experts, LANES), F32),
    )
    out_specs = (
        pl.BlockSpec((1, ts, d), lambda b, i: (b, i, 0)),
        pl.BlockSpec((1, ts, d // 2), lambda b, i: (b, i, 0)),
        pl.BlockSpec((2, ts), lambda b, i: (0, b * ns + i)),
        pl.BlockSpec((2, ts), lambda b, i: (0, b * ns + i)),
        pl.BlockSpec((ts, LANES), lambda b, i: (b * ns + i, 0)),
        pl.BlockSpec((n_experts, LANES), lambda b, i: (0, 0)),
    )
    return pl.pallas_call(
        functools.partial(_mixer_body, alpha, n_experts, sub, prev is not None),
        grid=(bsz, ns),
        in_specs=specs,
        out_specs=out_specs,
        out_shape=out_shape,
        scratch_shapes=[
            pltpu.VMEM((nsub, CONV_HALO_VREGS * SUBLANES + sub, cw), F32),
            pltpu.VMEM((nsub, sub, cw), F32),
            pltpu.VMEM((nsub, POOL_HALO + sub, pw), F32),
            pltpu.VMEM((CONV_HALO_VREGS * SUBLANES, cw), F32),
            pltpu.VMEM((POOL_HALO, pw), F32),
            pltpu.VMEM((n_experts, LANES), F32),
            pltpu.VMEM((ts, d) if prev is not None else (SUBLANES, LANES), F32),
        ],
        compiler_params=pltpu.CompilerParams(dimension_semantics=("arbitrary", "arbitrary"), vmem_limit_bytes=VMEM_LIMIT),
        name="mixer_router",
    )(*args)


def _dest_body(n_experts, pstart_ref, e_ref, rank_ref, dest_ref):
    e = e_ref[...]
    dest = rank_ref[...]
    for ex in range(n_experts):
        dest = dest + jnp.where(e == ex, pstart_ref[ex], 0)
    dest_ref[...] = dest


def _dest_rows(pstart, eidx, rank):
    n_tok = eidx.shape[1]
    tn = min(8192, n_tok)
    grid_spec = pltpu.PrefetchScalarGridSpec(
        num_scalar_prefetch=1,
        grid=(n_tok // tn,),
        in_specs=[pl.BlockSpec((2, tn), lambda i, ps: (0, i)), pl.BlockSpec((2, tn), lambda i, ps: (0, i))],
        out_specs=pl.BlockSpec((2, tn), lambda i, ps: (0, i)),
    )
    return pl.pallas_call(
        functools.partial(_dest_body, pstart.shape[0]),
        grid_spec=grid_spec,
        out_shape=jax.ShapeDtypeStruct((2, n_tok), I32),
        compiler_params=pltpu.CompilerParams(dimension_semantics=("arbitrary",)),
        name="dest_rows",
    )(pstart, eidx, rank)


def _sc_mesh():
    return plsc.VectorSubcoreMesh(core_axis_name="c", subcore_axis_name="s",
                                  num_cores=SC_CORES, num_subcores=SC_SUBCORES)


def _dispatch(u2, d0, d1, n_rows):
    n_tok, d = u2.shape
    workers = SC_CORES * SC_SUBCORES
    per_w = n_tok // workers
    ch = SC_ROWS_PER_STEP
    nch = per_w // ch

    @functools.partial(
        pl.kernel, mesh=_sc_mesh(),
        out_type=jax.ShapeDtypeStruct((n_rows, d), u2.dtype),
        scratch_types=[pltpu.VMEM((nch, ch), I32), pltpu.VMEM((nch, ch), I32), pltpu.VMEM((2, ch, d), u2.dtype),
                       pltpu.SemaphoreType.DMA((2,)), pltpu.SemaphoreType.DMA((2,))],
        name="sc_dispatch",
    )
    def run(u_hbm, d0_hbm, d1_hbm, xs_hbm, i0, i1, rows, rsem, ssem):
        wid = lax.axis_index("s") * SC_CORES + lax.axis_index("c")
        base = wid * per_w
        pltpu.sync_copy(d0_hbm.at[pl.ds(wid * nch, nch)], i0)
        pltpu.sync_copy(d1_hbm.at[pl.ds(wid * nch, nch)], i1)

        def read(j, b):
            return pltpu.make_async_copy(u_hbm.at[pl.ds(base + j * ch, ch)], rows.at[b], rsem.at[b])

        def scatter(idx, j, b):
            return pltpu.make_async_copy(rows.at[b], xs_hbm.at[idx.at[j]], ssem.at[b])

        read(0, 0).start()

        @pl.loop(0, nch, step=2)
        def _(j0):
            for b in range(2):
                j = j0 + b
                read(j, b).wait()
                scatter(i0, j, b).start()
                scatter(i1, j, b).start()

                @pl.when(j >= 1)
                def _():
                    scatter(i0, j - 1, 1 - b).wait()
                    scatter(i1, j - 1, 1 - b).wait()

                @pl.when(j + 1 < nch)
                def _():
                    read(j + 1, 1 - b).start()

        scatter(i0, nch - 1, 1).wait()
        scatter(i1, nch - 1, 1).wait()

    return run(u2, d0.reshape(n_tok // ch, ch), d1.reshape(n_tok // ch, ch))


def _gather_rows(ys, dd):
    n_out = dd.shape[0]
    d = ys.shape[1]
    workers = SC_CORES * SC_SUBCORES
    per_w = n_out // workers
    ch = SC_ROWS_PER_STEP
    nch = per_w // ch

    @functools.partial(
        pl.kernel, mesh=_sc_mesh(),
        out_type=jax.ShapeDtypeStruct((n_out, d), ys.dtype),
        scratch_types=[pltpu.VMEM((nch, ch), I32), pltpu.VMEM((2, ch, d), ys.dtype),
                       pltpu.SemaphoreType.DMA((2,)), pltpu.SemaphoreType.DMA((2,))],
        name="sc_gather",
    )
    def run(ys_hbm, dd_hbm, out_hbm, idx, rows, gsem, wsem):
        wid = lax.axis_index("s") * SC_CORES + lax.axis_index("c")
        base = wid * per_w
        pltpu.sync_copy(dd_hbm.at[pl.ds(wid * nch, nch)], idx)

        def gather(j, b):
            return pltpu.make_async_copy(ys_hbm.at[idx.at[j]], rows.at[b], gsem.at[b])

        def write(j, b):
            return pltpu.make_async_copy(rows.at[b], out_hbm.at[pl.ds(base + j * ch, ch)], wsem.at[b])

        gather(0, 0).start()

        @pl.loop(0, nch, step=2)
        def _(j0):
            for b in range(2):
                j = j0 + b
                gather(j, b).wait()
                write(j, b).start()

                @pl.when(j >= 1)
                def _():
                    write(j - 1, 1 - b).wait()

                @pl.when(j + 1 < nch)
                def _():
                    gather(j + 1, 1 - b).start()

        write(nch - 1, 1).wait()

    return run(ys, dd.reshape(n_out // ch, ch))


def _expert_body(blk_ref, nused_ref, xs_ref, wg_ref, wu_ref, wd_ref, ys_ref, wgb, wub, wdb):
    j = pl.program_id(0)
    used = j < nused_ref[0]
    new_expert = jnp.logical_or(j == 0, blk_ref[j] != blk_ref[jnp.maximum(j - 1, 0)])

    @pl.when(jnp.logical_and(used, new_expert))
    def _():
        wgb[...] = wg_ref[0, 0].astype(BF16)
        wub[...] = wu_ref[0, 0].astype(BF16)
        wdb[...] = wd_ref[0, 0].astype(BF16)

    @pl.when(used)
    def _():
        lo, hi = _unpack_halves(xs_ref[...])
        lo = lo.astype(BF16)
        hi = hi.astype(BF16)
        half = lo.shape[1]
        fw = wgb.shape[1] // EXPERT_F_SPLIT
        acc = None
        for c in range(EXPERT_F_SPLIT):
            cs = slice(c * fw, (c + 1) * fw)
            g = _dot(lo, wgb[0:half, cs]) + _dot(hi, wgb[half:, cs])
            up = _dot(lo, wub[0:half, cs]) + _dot(hi, wub[half:, cs])
            hid = (g * jax.nn.sigmoid(g) * up).astype(BF16)
            part = _dot(hid, wdb[cs, :])
            acc = part if acc is None else acc + part
        ys_ref[...] = _pack_halves(acc)

    @pl.when(j >= nused_ref[0])
    def _():
        ys_ref[...] = jnp.zeros(ys_ref.shape, U32)


def _experts(blk_e, nused, xs, wg, wu, wd, layer):
    n_rows = xs.shape[0]
    d = wg.shape[2]
    f = wg.shape[3]
    tm = EXPERT_TILE
    grid_spec = pltpu.PrefetchScalarGridSpec(
        num_scalar_prefetch=2,
        grid=(n_rows // tm,),
        in_specs=[
            pl.BlockSpec((tm, d // 2), lambda j, be, nu: (jnp.minimum(j, nu[0] - 1), 0)),
            pl.BlockSpec((1, 1, d, f), lambda j, be, nu: (layer, be[j], 0, 0)),
            pl.BlockSpec((1, 1, d, f), lambda j, be, nu: (layer, be[j], 0, 0)),
            pl.BlockSpec((1, 1, f, d), lambda j, be, nu: (layer, be[j], 0, 0)),
        ],
        out_specs=pl.BlockSpec((tm, d // 2), lambda j, be, nu: (j, 0)),
        scratch_shapes=[pltpu.VMEM((d, f), BF16), pltpu.VMEM((d, f), BF16), pltpu.VMEM((f, d), BF16)],
    )
    return pl.pallas_call(
        _expert_body,
        grid_spec=grid_spec,
        out_shape=jax.ShapeDtypeStruct((n_rows, d // 2), U32),
        compiler_params=pltpu.CompilerParams(dimension_semantics=("arbitrary",), vmem_limit_bytes=VMEM_LIMIT),
        name="expert_ffn",
    )(blk_e, nused, xs, wg, wu, wd)


def _combine_body(alpha, x1_ref, y0_ref, y1_ref, mod_ref, wm_ref, g_ref, b_ref, *rest):
    o_ref = rest[-1]
    o_ref[...] = _combine_rows(alpha, x1_ref[...], y0_ref[...], y1_ref[...], wm_ref[...],
                               mod_ref[0][5:6], g_ref[...], b_ref[...])


def _combine(x1, yy, mod, wm, ln_g, ln_b, seq, alpha, mod_b0, out_rows, out_row0, out_prev):
    n_tok, d = x1.shape
    tn = min(COMBINE_TILE, seq)
    per_seq = seq // tn
    nblk = n_tok // tn
    blk0 = out_row0 // tn
    rows = lambda: pl.BlockSpec((tn, d), lambda i: (i, 0))
    in_specs = [rows(), pl.BlockSpec((tn, d // 2), lambda i: (i, 0)),
                pl.BlockSpec((tn, d // 2), lambda i: (nblk + i, 0)),
                pl.BlockSpec((1, 6, d), lambda i: (i // per_seq + mod_b0, 0, 0)),
                pl.BlockSpec((tn, LANES), lambda i: (i, 0)),
                pl.BlockSpec((1, d), lambda i: (0, 0)),
                pl.BlockSpec((1, d), lambda i: (0, 0))]
    args = [x1, yy, yy, mod, wm, ln_g.reshape(1, d), ln_b.reshape(1, d)]
    aliases = {}
    if out_prev is not None:
        in_specs.append(pl.BlockSpec(memory_space=pl.ANY))
        args.append(out_prev)
        aliases = {len(args) - 1: 0}
    return pl.pallas_call(
        functools.partial(_combine_body, alpha),
        grid=(nblk,),
        in_specs=in_specs,
        out_specs=pl.BlockSpec((tn, d), lambda i: (blk0 + i, 0)),
        out_shape=jax.ShapeDtypeStruct((out_rows, d), F32),
        input_output_aliases=aliases,
        compiler_params=pltpu.CompilerParams(dimension_semantics=("arbitrary",), vmem_limit_bytes=VMEM_LIMIT),
        name="combine_ln",
    )(*args)


def kernel(x, c, w_ada, b_ada, w_in, b_in, conv_w, conv_b, conv_ln_g, conv_ln_b, pool_w, pool_scale, w_out, b_out, ln1_g, ln1_b, w_router, router_bias, w_gate, w_up, w_down, ln2_g, ln2_b):
    bsz, seq, d = x.shape
    depth = w_ada.shape[0]
    n_experts = w_router.shape[1]
    alpha = float((2 * depth) ** 0.25)
    tm = EXPERT_TILE
    n_chunks = BATCH_CHUNKS if bsz % BATCH_CHUNKS == 0 else 1
    bc = bsz // n_chunks
    n_tok = bc * seq
    n_rows = 2 * n_tok + n_experts * tm

    mod_all = _ada_mod(c, w_ada, b_ada).reshape(depth, bsz, 6, d)
    wr = jnp.pad(w_router.astype(BF16), ((0, 0), (0, LANES - n_experts)))

    chunks = [x] * n_chunks
    starts = [ci * bc for ci in range(n_chunks)]
    prevs = [None] * n_chunks
    out = None
    for l in range(depth):
        lw = dict(w_in=w_in[l].astype(BF16), b_in=b_in[l], conv_w=conv_w[l], conv_b=conv_b[l],
                  conv_ln_g=conv_ln_g[l], conv_ln_b=conv_ln_b[l], pool_w=pool_w[l].astype(BF16),
                  pool_scale=pool_scale[l], w_out=w_out[l].astype(BF16), b_out=b_out[l],
                  ln1_g=ln1_g[l], ln1_b=ln1_b[l])
        mod = mod_all[l]
        last = l == depth - 1
        st = []
        for ci in range(n_chunks):
            x1, u2, eidx, rank, wm, cnt = _mixer(chunks[ci], mod, lw, wr, router_bias, alpha,
                                                 bc, starts[ci], ci * bc, prevs[ci])
            counts = cnt[:, 0].astype(I32)
            tiles = (counts + tm - 1) // tm
            tile_end = jnp.cumsum(tiles)
            pstart = ((tile_end - tiles) * tm).astype(I32)
            nused = tile_end[-1:].astype(I32)
            blk_e = jnp.minimum(
                jnp.sum(tile_end[None, :] <= jnp.arange(n_rows // tm, dtype=I32)[:, None], axis=1), n_experts - 1
            ).astype(I32)
            dest = _dest_rows(pstart, eidx, rank)
            xs = _dispatch(u2.reshape(n_tok, d // 2), dest[0], dest[1], n_rows)
            st.append((x1, wm, blk_e, nused, dest, xs))
        ys = [_experts(blk_e, nused, xs, w_gate, w_up, w_down, l) for (_, _, blk_e, nused, _, xs) in st]
        yy = [_gather_rows(ys[ci], st[ci][4].reshape(2 * n_tok)) for ci in range(n_chunks)]
        for ci in range(n_chunks):
            x1, wm = st[ci][0], st[ci][1]
            if last:
                out = _combine(x1.reshape(n_tok, d), yy[ci], mod, wm, ln2_g[l], ln2_b[l], seq, alpha,
                               ci * bc, bsz * seq, ci * n_tok, out)
            else:
                chunks[ci] = x1
                starts[ci] = 0
                prevs[ci] = (yy[ci], wm, mod, ln2_g[l], ln2_b[l])
    return out.reshape(bsz, seq, d)
```

```python
import functools

import jax
import jax.numpy as jnp
from jax import lax
from jax.experimental import pallas as pl
from jax.experimental.pallas import tpu as pltpu
from jax.experimental.pallas import tpu_sc as plsc

F32 = jnp.float32
BF16 = jnp.bfloat16
I32 = jnp.int32
U32 = jnp.uint32

CONV_KERNEL = 31
POOL_WINDOWS = (2, 4, 8, 16)
N_EXPERT_GROUPS = 4
LN_EPS = 1e-5

SUBLANES = 8
LANES = 128

SEQ_TILE = 1024
SEQ_SUB_TILE = 256
STAGE_LEAD = 4
EXPERT_TILE = 1024
EXPERT_F_SPLIT = 2
COMBINE_TILE = 1024
BATCH_CHUNKS = 2
CONV_HALO_VREGS = 32
POOL_HALO = 16
VMEM_LIMIT = 56 * 1024 * 1024
SC_CORES = 2
SC_SUBCORES = 16
SC_ROWS_PER_STEP = 64


def _split_bf16(a):
    hi = a.astype(BF16)
    lo = (a - hi.astype(F32)).astype(BF16)
    return hi, lo


def _dot(a, b):
    return jnp.dot(a, b, preferred_element_type=F32)


def _pack_halves(y):
    h = y.shape[1] // 2
    lo = lax.bitcast_convert_type(y[:, :h].astype(BF16).astype(F32), U32)
    hi = lax.bitcast_convert_type(y[:, h:].astype(BF16).astype(F32), U32)
    return (lo >> 16) | hi


def _unpack_halves(p):
    lo = lax.bitcast_convert_type(p << 16, F32)
    hi = lax.bitcast_convert_type(p & jnp.uint32(0xFFFF0000), F32)
    return lo, hi


def _ada_body(c_ref, w_ref, b_ref, o_ref):
    c = c_ref[...]
    ca = c * jax.nn.sigmoid(c)
    chi, clo = _split_bf16(ca)
    whi, wlo = _split_bf16(w_ref[0])
    o_ref[0] = _dot(chi, whi) + _dot(chi, wlo) + _dot(clo, whi) + b_ref[0]


def _ada_mod(c, w_ada, b_ada):
    n_layers, d, n6 = w_ada.shape
    bsz = c.shape[0]
    tn = n6 // 6
    return pl.pallas_call(
        _ada_body,
        grid=(n_layers, n6 // tn),
        in_specs=[
            pl.BlockSpec((bsz, d), lambda l, j: (0, 0)),
            pl.BlockSpec((1, d, tn), lambda l, j: (l, 0, j)),
            pl.BlockSpec((1, 1, tn), lambda l, j: (l, 0, j)),
        ],
        out_specs=pl.BlockSpec((1, bsz, tn), lambda l, j: (l, 0, j)),
        out_shape=jax.ShapeDtypeStruct((n_layers, bsz, n6), F32),
        compiler_params=pltpu.CompilerParams(dimension_semantics=("arbitrary", "arbitrary"), vmem_limit_bytes=VMEM_LIMIT),
        name="ada_mod",
    )(c, w_ada, b_ada.reshape(n_layers, 1, n6))


def _layer_norm_rows(z, g, b, eps=LN_EPS):
    mu = jnp.mean(z, axis=-1, keepdims=True)
    zc = z - mu
    var = jnp.mean(zc * zc, axis=-1, keepdims=True)
    return zc * lax.rsqrt(var + eps) * g + b


def _deepnorm_rows(alpha, x, gate, branch, g, b):
    return _layer_norm_rows(x + (gate * (1.0 / alpha)) * branch, g, b, LN_EPS / (alpha * alpha))


def _combine_rows(alpha, x1, y0, y1, wm, g2, ln_g, ln_b):
    lo0, hi0 = _unpack_halves(y0)
    lo1, hi1 = _unpack_halves(y1)
    w0, w1 = wm[:, 0:1], wm[:, 1:2]
    f = jnp.concatenate([w0 * lo0 + w1 * lo1, w0 * hi0 + w1 * hi1], axis=1)
    return _deepnorm_rows(alpha, x1, 1.0 + g2, f, ln_g, ln_b)


def _mixer_body(alpha, n_experts, sub, fused, *refs):
    if fused:
        xp_ref, y0_ref, y1_ref, wmp_ref, modp_ref, l2g_ref, l2b_ref = refs[:7]
        refs = refs[7:]
        x_ref = None
    else:
        x_ref = refs[0]
        refs = refs[1:]
    (mod_ref, win_ref, bin_ref, cw_ref, cb_ref, cg_ref, cbeta_ref, pw_ref, ps_ref,
     wout_ref, bout_ref, l1g_ref, l1b_ref, wr_ref, rb_ref, perm_ref, permt_ref, tri_ref,
     x1_ref, u2_ref, e_ref, rank_ref, wm_ref, cnt_ref,
     ebuf, cbuf, pbuf, vprev, phalo, carry, xbuf) = refs
    ts = x1_ref.shape[1]
    cw = cg_ref.shape[1]
    pw = ps_ref.shape[1]
    gw = pw // len(POOL_WINDOWS)
    fine = sub // SUBLANES
    halo = CONV_HALO_VREGS
    b = pl.program_id(0)
    i = pl.program_id(1)

    @pl.when(i == 0)
    def _():
        vprev[...] = jnp.zeros(vprev.shape, F32)
        phalo[...] = jnp.zeros(phalo.shape, F32)

    @pl.when(jnp.logical_and(b == 0, i == 0))
    def _():
        carry[...] = jnp.zeros(carry.shape, F32)

    mod = mod_ref[0]
    sh1, sc1, g1 = mod[0:1], mod[1:2], mod[2:3]
    sh2, sc2 = mod[3:4], mod[4:5]

    def stage1(s):
        r0 = s * sub
        if fused:
            x = _combine_rows(alpha, xp_ref[0, r0:r0 + sub, :], y0_ref[r0:r0 + sub, :], y1_ref[r0:r0 + sub, :],
                              wmp_ref[r0:r0 + sub, :], modp_ref[0][5:6], l2g_ref[...], l2b_ref[...])
            xbuf[r0:r0 + sub, :] = x
        else:
            x = x_ref[0, r0:r0 + sub, :]
        u = (x * (1.0 + sc1) + sh1).astype(BF16)
        h = _dot(u, win_ref[...]) + bin_ref[...]
        a = h[:, :cw]
        gate = h[:, cw:2 * cw]
        p = h[:, 2 * cw:]

        v = (a * jax.nn.sigmoid(gate)).astype(BF16)
        vb = _dot(perm_ref[...], v)
        ebuf[s, halo * SUBLANES:, :] = vb
        sl = lax.broadcasted_iota(I32, (SUBLANES, cw), 0)
        for mm in range(halo):
            src = (fine - halo + mm) * SUBLANES
            cur = vb[src:src + SUBLANES, :]
            prev = vprev[mm * SUBLANES:(mm + 1) * SUBLANES, :]
            ebuf[s, mm * SUBLANES:(mm + 1) * SUBLANES, :] = pltpu.roll(
                jnp.where(sl == SUBLANES - 1, prev, cur), 1, 0)
        vprev[...] = vb[(fine - halo) * SUBLANES:, :]
        pbuf[s, 0:POOL_HALO, :] = phalo[...]
        pbuf[s, POOL_HALO:, :] = p
        phalo[...] = p[sub - POOL_HALO:, :]

    def stage2(s):
        r0 = s * sub
        x = xbuf[r0:r0 + sub, :] if fused else x_ref[0, r0:r0 + sub, :]
        chunk = 8 * SUBLANES
        first = halo - (CONV_KERNEL - 1)
        for q0 in range(0, sub, chunk):
            for c0 in range(0, cw, LANES):
                acc = jnp.broadcast_to(cb_ref[:, c0:c0 + LANES], (chunk, LANES))
                for k in range(CONV_KERNEL):
                    off = q0 + (first + k) * SUBLANES
                    acc = acc + cw_ref[k:k + 1, c0:c0 + LANES] * ebuf[s, off:off + chunk, c0:c0 + LANES]
                cbuf[s, q0:q0 + chunk, c0:c0 + LANES] = acc
        yc = _layer_norm_rows(cbuf[s], cg_ref[...], cbeta_ref[...])
        ya_b = (yc * jax.nn.sigmoid(yc)).astype(BF16)
        ya = _dot(permt_ref[...], ya_b).astype(BF16)

        ext = pbuf[s]
        p = ext[POOL_HALO:, :]
        w_sum = ext + pltpu.roll(ext, 1, 0)
        parts = [w_sum[:, :gw]]
        shift = 2
        for _ in range(len(POOL_WINDOWS) - 1):
            w_sum = w_sum[:, gw:]
            w_sum = w_sum + pltpu.roll(w_sum, shift, 0)
            parts.append(w_sum[:, :gw])
            shift *= 2
        wsum = jnp.concatenate(parts, axis=1)[POOL_HALO:, :]
        pos = (lax.broadcasted_iota(I32, (POOL_HALO, gw), 0) + (i * ts + r0 + 1)).astype(F32)
        cnt = jnp.concatenate([jnp.minimum(pos, float(wlen)) for wlen in POOL_WINDOWS], axis=1)
        inv = jnp.concatenate([jnp.full((1, gw), 1.0 / wlen, F32) for wlen in POOL_WINDOWS], axis=1)
        mean = jnp.concatenate([wsum[:POOL_HALO] / cnt, wsum[POOL_HALO:] * inv], axis=0)
        dpool = (mean - p).astype(BF16)
        yb_parts = [_dot(dpool[:, gi * gw:(gi + 1) * gw], pw_ref[gi]) for gi in range(len(POOL_WINDOWS))]
        yb = (jnp.concatenate(yb_parts, axis=1) * ps_ref[...]).astype(BF16)

        mix = _dot(jnp.concatenate([ya, yb], axis=1), wout_ref[...]) + bout_ref[...]
        x1 = _deepnorm_rows(alpha, x, 1.0 + g1, mix, l1g_ref[...], l1b_ref[...])
        x1_ref[0, r0:r0 + sub, :] = x1
        u2 = x1 * (1.0 + sc2) + sh2
        u2_ref[0, r0:r0 + sub, :] = _pack_halves(u2)

        logits = _dot(u2.astype(BF16), wr_ref[...]).T[0:n_experts, :]
        scores = jax.nn.sigmoid(logits)
        sel = scores + rb_ref[...]
        epg = n_experts // N_EXPERT_GROUPS
        io = lax.broadcasted_iota(I32, (epg, sub), 0).astype(F32)
        neg = jnp.float32(-jnp.inf)
        best = None
        for g in range(N_EXPERT_GROUPS):
            sg = sel[g * epg:(g + 1) * epg, :]
            scg = scores[g * epg:(g + 1) * epg, :]
            m1 = jnp.max(sg, axis=0, keepdims=True)
            i1 = jnp.min(jnp.where(sg == m1, io, float(epg)), axis=0, keepdims=True)
            rest = jnp.where(io == i1, neg, sg)
            m2 = jnp.max(rest, axis=0, keepdims=True)
            i2 = jnp.min(jnp.where(jnp.logical_and(rest == m2, io != i1), io, float(epg)), axis=0, keepdims=True)
            s1 = jnp.sum(jnp.where(io == i1, scg, 0.0), axis=0, keepdims=True)
            s2 = jnp.sum(jnp.where(io == i2, scg, 0.0), axis=0, keepdims=True)
            gs = m1 + m2
            cand = (gs, i1 + float(g * epg), i2 + float(g * epg), s1, s2)
            if best is None:
                best = cand
            else:
                upd = gs > best[0]
                best = tuple(jnp.where(upd, cn, bs) for cn, bs in zip(cand, best))
        _, e1, e2, s1, s2 = best
        denom = s1 + s2
        w1 = s1 / denom
        w2 = s2 / denom

        ioe = lax.broadcasted_iota(I32, (n_experts, sub), 0).astype(F32)
        hit1 = ioe == e1
        hit2 = ioe == e2
        onehot = jnp.where(jnp.logical_or(hit1, hit2), 1.0, 0.0)
        before = _dot(onehot.astype(BF16), tri_ref[...]) + carry[:, 0:1]
        r1 = jnp.sum(jnp.where(hit1, before, 0.0), axis=0, keepdims=True)
        r2 = jnp.sum(jnp.where(hit2, before, 0.0), axis=0, keepdims=True)
        carry[...] = carry[...] + jnp.sum(onehot, axis=1, keepdims=True)

        e_ref[0:1, r0:r0 + sub] = e1.astype(I32)
        e_ref[1:2, r0:r0 + sub] = e2.astype(I32)
        rank_ref[0:1, r0:r0 + sub] = r1.astype(I32)
        rank_ref[1:2, r0:r0 + sub] = r2.astype(I32)
        rowi = lax.broadcasted_iota(I32, (LANES, sub), 0)
        wpad = jnp.where(rowi == 0, w1, jnp.where(rowi == 1, w2, 0.0))
        wm_ref[r0:r0 + sub, :] = wpad.T

    nsub = ts // sub
    for s in range(min(STAGE_LEAD, nsub)):
        stage1(s)
    for s in range(nsub):
        if s + STAGE_LEAD < nsub:
            stage1(s + STAGE_LEAD)
        stage2(s)
    cnt_ref[...] = carry[...]


def _row_perm(n):
    r = jnp.arange(n)
    t = (n // SUBLANES) * (r % SUBLANES) + r // SUBLANES
    return (t[:, None] == jnp.arange(n)[None, :]).astype(BF16)


def _mixer(x, mod, lw, wr, rbias, alpha, bsz, x_b0, mod_b0, prev=None):
    _, seq, d = x.shape
    ts = min(SEQ_TILE, seq)
    sub = min(SEQ_SUB_TILE, ts)
    nsub = ts // sub
    ns = seq // ts
    n_tok = bsz * seq
    n_experts = rbias.shape[0]
    cw = lw["conv_w"].shape[1]
    pw = lw["pool_scale"].shape[0]
    perm = _row_perm(sub)
    tri = (jnp.arange(sub)[:, None] < jnp.arange(sub)[None, :]).astype(BF16)
    row = lambda a: a.reshape(1, -1)
    full = lambda shape: pl.BlockSpec(shape, lambda b, i: (0,) * len(shape))
    if prev is None:
        ins = [(x, pl.BlockSpec((1, ts, d), lambda b, i: (b + x_b0, i, 0)))]
    else:
        yy, wmp, modp, l2g, l2b = prev
        nblk = n_tok // ts
        ins = [
            (x, pl.BlockSpec((1, ts, d), lambda b, i: (b + x_b0, i, 0))),
            (yy, pl.BlockSpec((ts, d // 2), lambda b, i: (b * ns + i, 0))),
            (yy, pl.BlockSpec((ts, d // 2), lambda b, i: (nblk + b * ns + i, 0))),
            (wmp, pl.BlockSpec((ts, LANES), lambda b, i: (b * ns + i, 0))),
            (modp, pl.BlockSpec((1, 6, d), lambda b, i: (b + mod_b0, 0, 0))),
            (row(l2g), None), (row(l2b), None),
        ]
    ins += [
        (mod, pl.BlockSpec((1, 6, d), lambda b, i: (b + mod_b0, 0, 0))),
        (lw["w_in"], None), (row(lw["b_in"]), None),
        (jnp.pad(lw["conv_w"], ((0, 1), (0, 0))), None), (row(lw["conv_b"]), None),
        (row(lw["conv_ln_g"]), None), (row(lw["conv_ln_b"]), None),
        (lw["pool_w"], None), (row(lw["pool_scale"]), None),
        (lw["w_out"], None), (row(lw["b_out"]), None),
        (row(lw["ln1_g"]), None), (row(lw["ln1_b"]), None),
        (wr, None), (rbias.reshape(-1, 1), None),
        (perm, None), (perm.T, None), (tri, None),
    ]
    args = [a for a, _ in ins]
    specs = [s if s is not None else full(a.shape) for a, s in ins]
    out_shape = (
        jax.ShapeDtypeStruct((bsz, seq, d), F32),
        jax.ShapeDtypeStruct((bsz, seq, d // 2), U32),
        jax.ShapeDtypeStruct((2, n_tok), I32),
        jax.ShapeDtypeStruct((2, n_tok), I32),
        jax.ShapeDtypeStruct((n_tok, LANES), F32),
        jax.ShapeDtypeStruct((n_experts, LANES), F32),
    )
    out_specs = (
        pl.BlockSpec((1, ts, d), lambda b, i: (b, i, 0)),
        pl.BlockSpec((1, ts, d // 2), lambda b, i: (b, i, 0)),
        pl.BlockSpec((2, ts), lambda b, i: (0, b * ns + i)),
        pl.BlockSpec((2, ts), lambda b, i: (0, b * ns + i)),
        pl.BlockSpec((ts, LANES), lambda b, i: (b * ns + i, 0)),
        pl.BlockSpec((n_experts, LANES), lambda b, i: (0, 0)),
    )
    return pl.pallas_call(
        functools.partial(_mixer_body, alpha, n_experts, sub, prev is not None),
        grid=(bsz, ns),
        in_specs=specs,
        out_specs=out_specs,
        out_shape=out_shape,
        scratch_shapes=[
            pltpu.VMEM((nsub, CONV_HALO_VREGS * SUBLANES + sub, cw), F32),
            pltpu.VMEM((nsub, sub, cw), F32),
            pltpu.VMEM((nsub, POOL_HALO + sub, pw), F32),
            pltpu.VMEM((CONV_HALO_VREGS * SUBLANES, cw), F32),
            pltpu.VMEM((POOL_HALO, pw), F32),
            pltpu.VMEM((n_experts, LANES), F32),
            pltpu.VMEM((ts, d) if prev is not None else (SUBLANES, LANES), F32),
        ],
        compiler_params=pltpu.CompilerParams(dimension_semantics=("arbitrary", "arbitrary"), vmem_limit_bytes=VMEM_LIMIT),
        name="mixer_router",
    )(*args)


def _dest_body(n_experts, pstart_ref, e_ref, rank_ref, dest_ref):
    e = e_ref[...]
    dest = rank_ref[...]
    for ex in range(n_experts):
        dest = dest + jnp.where(e == ex, pstart_ref[ex], 0)
    dest_ref[...] = dest


def _dest_rows(pstart, eidx, rank):
    n_tok = eidx.shape[1]
    tn = min(8192, n_tok)
    grid_spec = pltpu.PrefetchScalarGridSpec(
        num_scalar_prefetch=1,
        grid=(n_tok // tn,),
        in_specs=[pl.BlockSpec((2, tn), lambda i, ps: (0, i)), pl.BlockSpec((2, tn), lambda i, ps: (0, i))],
        out_specs=pl.BlockSpec((2, tn), lambda i, ps: (0, i)),
    )
    return pl.pallas_call(
        functools.partial(_dest_body, pstart.shape[0]),
        grid_spec=grid_spec,
        out_shape=jax.ShapeDtypeStruct((2, n_tok), I32),
        compiler_params=pltpu.CompilerParams(dimension_semantics=("arbitrary",)),
        name="dest_rows",
    )(pstart, eidx, rank)


def _sc_mesh():
    return plsc.VectorSubcoreMesh(core_axis_name="c", subcore_axis_name="s",
                                  num_cores=SC_CORES, num_subcores=SC_SUBCORES)


def _dispatch(u2, d0, d1, n_rows):
    n_tok, d = u2.shape
    workers = SC_CORES * SC_SUBCORES
    per_w = n_tok // workers
    ch = SC_ROWS_PER_STEP
    nch = per_w // ch

    @functools.partial(
        pl.kernel, mesh=_sc_mesh(),
        out_type=jax.ShapeDtypeStruct((n_rows, d), u2.dtype),
        scratch_types=[pltpu.VMEM((nch, ch), I32), pltpu.VMEM((nch, ch), I32), pltpu.VMEM((2, ch, d), u2.dtype),
                       pltpu.SemaphoreType.DMA((2,)), pltpu.SemaphoreType.DMA((2,))],
        name="sc_dispatch",
    )
    def run(u_hbm, d0_hbm, d1_hbm, xs_hbm, i0, i1, rows, rsem, ssem):
        wid = lax.axis_index("s") * SC_CORES + lax.axis_index("c")
        base = wid * per_w
        pltpu.sync_copy(d0_hbm.at[pl.ds(wid * nch, nch)], i0)
        pltpu.sync_copy(d1_hbm.at[pl.ds(wid * nch, nch)], i1)

        def read(j, b):
            return pltpu.make_async_copy(u_hbm.at[pl.ds(base + j * ch, ch)], rows.at[b], rsem.at[b])

        def scatter(idx, j, b):
            return pltpu.make_async_copy(rows.at[b], xs_hbm.at[idx.at[j]], ssem.at[b])

        read(0, 0).start()

        @pl.loop(0, nch, step=2)
        def _(j0):
            for b in range(2):
                j = j0 + b
                read(j, b).wait()
                scatter(i0, j, b).start()
                scatter(i1, j, b).start()

                @pl.when(j >= 1)
                def _():
                    scatter(i0, j - 1, 1 - b).wait()
                    scatter(i1, j - 1, 1 - b).wait()

                @pl.when(j + 1 < nch)
                def _():
                    read(j + 1, 1 - b).start()

        scatter(i0, nch - 1, 1).wait()
        scatter(i1, nch - 1, 1).wait()

    return run(u2, d0.reshape(n_tok // ch, ch), d1.reshape(n_tok // ch, ch))


def _gather_rows(ys, dd):
    n_out = dd.shape[0]
    d = ys.shape[1]
    workers = SC_CORES * SC_SUBCORES
    per_w = n_out // workers
    ch = SC_ROWS_PER_STEP
    nch = per_w // ch

    @functools.partial(
        pl.kernel, mesh=_sc_mesh(),
        out_type=jax.ShapeDtypeStruct((n_out, d), ys.dtype),
        scratch_types=[pltpu.VMEM((nch, ch), I32), pltpu.VMEM((2, ch, d), ys.dtype),
                       pltpu.SemaphoreType.DMA((2,)), pltpu.SemaphoreType.DMA((2,))],
        name="sc_gather",
    )
    def run(ys_hbm, dd_hbm, out_hbm, idx, rows, gsem, wsem):
        wid = lax.axis_index("s") * SC_CORES + lax.axis_index("c")
        base = wid * per_w
        pltpu.sync_copy(dd_hbm.at[pl.ds(wid * nch, nch)], idx)

        def gather(j, b):
            return pltpu.make_async_copy(ys_hbm.at[idx.at[j]], rows.at[b], gsem.at[b])

        def write(j, b):
            return pltpu.make_async_copy(rows.at[b], out_hbm.at[pl.ds(base + j * ch, ch)], wsem.at[b])

        gather(0, 0).start()

        @pl.loop(0, nch, step=2)
        def _(j0):
            for b in range(2):
                j = j0 + b
                gather(j, b).wait()
                write(j, b).start()

                @pl.when(j >= 1)
                def _():
                    write(j - 1, 1 - b).wait()

                @pl.when(j + 1 < nch)
                def _():
                    gather(j + 1, 1 - b).start()

        write(nch - 1, 1).wait()

    return run(ys, dd.reshape(n_out // ch, ch))


def _expert_body(blk_ref, nused_ref, xs_ref, wg_ref, wu_ref, wd_ref, ys_ref, wgb, wub, wdb):
    j = pl.program_id(0)
    used = j < nused_ref[0]
    new_expert = jnp.logical_or(j == 0, blk_ref[j] != blk_ref[jnp.maximum(j - 1, 0)])

    @pl.when(jnp.logical_and(used, new_expert))
    def _():
        wgb[...] = wg_ref[0, 0].astype(BF16)
        wub[...] = wu_ref[0, 0].astype(BF16)
        wdb[...] = wd_ref[0, 0].astype(BF16)

    @pl.when(used)
    def _():
        lo, hi = _unpack_halves(xs_ref[...])
        lo = lo.astype(BF16)
        hi = hi.astype(BF16)
        half = lo.shape[1]
        fw = wgb.shape[1] // EXPERT_F_SPLIT
        acc = None
        for c in range(EXPERT_F_SPLIT):
            cs = slice(c * fw, (c + 1) * fw)
            g = _dot(lo, wgb[0:half, cs]) + _dot(hi, wgb[half:, cs])
            up = _dot(lo, wub[0:half, cs]) + _dot(hi, wub[half:, cs])
            hid = (g * jax.nn.sigmoid(g) * up).astype(BF16)
            part = _dot(hid, wdb[cs, :])
            acc = part if acc is None else acc + part
        ys_ref[...] = _pack_halves(acc)

    @pl.when(j >= nused_ref[0])
    def _():
        ys_ref[...] = jnp.zeros(ys_ref.shape, U32)


def _experts(blk_e, nused, xs, wg, wu, wd, layer):
    n_rows = xs.shape[0]
    d = wg.shape[2]
    f = wg.shape[3]
    tm = EXPERT_TILE
    grid_spec = pltpu.PrefetchScalarGridSpec(
        num_scalar_prefetch=2,
        grid=(n_rows // tm,),
        in_specs=[
            pl.BlockSpec((tm, d // 2), lambda j, be, nu: (jnp.minimum(j, nu[0] - 1), 0)),
            pl.BlockSpec((1, 1, d, f), lambda j, be, nu: (layer, be[j], 0, 0)),
            pl.BlockSpec((1, 1, d, f), lambda j, be, nu: (layer, be[j], 0, 0)),
            pl.BlockSpec((1, 1, f, d), lambda j, be, nu: (layer, be[j], 0, 0)),
        ],
        out_specs=pl.BlockSpec((tm, d // 2), lambda j, be, nu: (j, 0)),
        scratch_shapes=[pltpu.VMEM((d, f), BF16), pltpu.VMEM((d, f), BF16), pltpu.VMEM((f, d), BF16)],
    )
    return pl.pallas_call(
        _expert_body,
        grid_spec=grid_spec,
        out_shape=jax.ShapeDtypeStruct((n_rows, d // 2), U32),
        compiler_params=pltpu.CompilerParams(dimension_semantics=("arbitrary",), vmem_limit_bytes=VMEM_LIMIT),
        name="expert_ffn",
    )(blk_e, nused, xs, wg, wu, wd)


def _combine_body(alpha, x1_ref, y0_ref, y1_ref, mod_ref, wm_ref, g_ref, b_ref, *rest):
    o_ref = rest[-1]
    o_ref[...] = _combine_rows(alpha, x1_ref[...], y0_ref[...], y1_ref[...], wm_ref[...],
                               mod_ref[0][5:6], g_ref[...], b_ref[...])


def _combine(x1, yy, mod, wm, ln_g, ln_b, seq, alpha, mod_b0, out_rows, out_row0, out_prev):
    n_tok, d = x1.shape
    tn = min(COMBINE_TILE, seq)
    per_seq = seq // tn
    nblk = n_tok // tn
    blk0 = out_row0 // tn
    rows = lambda: pl.BlockSpec((tn, d), lambda i: (i, 0))
    in_specs = [rows(), pl.BlockSpec((tn, d // 2), lambda i: (i, 0)),
                pl.BlockSpec((tn, d // 2), lambda i: (nblk + i, 0)),
                pl.BlockSpec((1, 6, d), lambda i: (i // per_seq + mod_b0, 0, 0)),
                pl.BlockSpec((tn, LANES), lambda i: (i, 0)),
                pl.BlockSpec((1, d), lambda i: (0, 0)),
                pl.BlockSpec((1, d), lambda i: (0, 0))]
    args = [x1, yy, yy, mod, wm, ln_g.reshape(1, d), ln_b.reshape(1, d)]
    aliases = {}
    if out_prev is not None:
        in_specs.append(pl.BlockSpec(memory_space=pl.ANY))
        args.append(out_prev)
        aliases = {len(args) - 1: 0}
    return pl.pallas_call(
        functools.partial(_combine_body, alpha),
        grid=(nblk,),
        in_specs=in_specs,
        out_specs=pl.BlockSpec((tn, d), lambda i: (blk0 + i, 0)),
        out_shape=jax.ShapeDtypeStruct((out_rows, d), F32),
        input_output_aliases=aliases,
        compiler_params=pltpu.CompilerParams(dimension_semantics=("arbitrary",), vmem_limit_bytes=VMEM_LIMIT),
        name="combine_ln",
    )(*args)


def kernel(x, c, w_ada, b_ada, w_in, b_in, conv_w, conv_b, conv_ln_g, conv_ln_b, pool_w, pool_scale, w_out, b_out, ln1_g, ln1_b, w_router, router_bias, w_gate, w_up, w_down, ln2_g, ln2_b):
    bsz, seq, d = x.shape
    depth = w_ada.shape[0]
    n_experts = w_router.shape[1]
    alpha = float((2 * depth) ** 0.25)
    tm = EXPERT_TILE
    n_chunks = BATCH_CHUNKS if bsz % BATCH_CHUNKS == 0 else 1
    bc = bsz // n_chunks
    n_tok = bc * seq
    n_rows = 2 * n_tok + n_experts * tm

    mod_all = _ada_mod(c, w_ada, b_ada).reshape(depth, bsz, 6, d)
    wr = jnp.pad(w_router.astype(BF16), ((0, 0), (0, LANES - n_experts)))

    chunks = [x] * n_chunks
    starts = [ci * bc for ci in range(n_chunks)]
    prevs = [None] * n_chunks
    out = None
    for l in range(depth):
        lw = dict(w_in=w_in[l].astype(BF16), b_in=b_in[l], conv_w=conv_w[l], conv_b=conv_b[l],
                  conv_ln_g=conv_ln_g[l], conv_ln_b=conv_ln_b[l], pool_w=pool_w[l].astype(BF16),
                  pool_scale=pool_scale[l], w_out=w_out[l].astype(BF16), b_out=b_out[l],
                  ln1_g=ln1_g[l], ln1_b=ln1_b[l])
        mod = mod_all[l]
        last = l == depth - 1
        st = []
        for ci in range(n_chunks):
            x1, u2, eidx, rank, wm, cnt = _mixer(chunks[ci], mod, lw, wr, router_bias, alpha,
                                                 bc, starts[ci], ci * bc, prevs[ci])
            counts = cnt[:, 0].astype(I32)
            tiles = (counts + tm - 1) // tm
            tile_end = jnp.cumsum(tiles)
            pstart = ((tile_end - tiles) * tm).astype(I32)
            nused = tile_end[-1:].astype(I32)
            blk_e = jnp.minimum(
                jnp.sum(tile_end[None, :] <= jnp.arange(n_rows // tm, dtype=I32)[:, None], axis=1), n_experts - 1
            ).astype(I32)
            dest = _dest_rows(pstart, eidx, rank)
            xs = _dispatch(u2.reshape(n_tok, d // 2), dest[0], dest[1], n_rows)
            st.append((x1, wm, blk_e, nused, dest, xs))
        ys = [_experts(blk_e, nused, xs, w_gate, w_up, w_down, l) for (_, _, blk_e, nused, _, xs) in st]
        yy = [_gather_rows(ys[ci], st[ci][4].reshape(2 * n_tok)) for ci in range(n_chunks)]
        for ci in range(n_chunks):
            x1, wm = st[ci][0], st[ci][1]
            if last:
                out = _combine(x1.reshape(n_tok, d), yy[ci], mod, wm, ln2_g[l], ln2_b[l], seq, alpha,
                               ci * bc, bsz * seq, ci * n_tok, out)
            else:
                chunks[ci] = x1
                starts[ci] = 0
                prevs[ci] = (yy[ci], wm, mod, ln2_g[l], ln2_b[l])
    return out.reshape(bsz, seq, d)
```

```python
import functools

import jax
import jax.numpy as jnp
from jax import lax
from jax.experimental import pallas as pl
from jax.experimental.pallas import tpu as pltpu
from jax.experimental.pallas import tpu_sc as plsc

F32 = jnp.float32
BF16 = jnp.bfloat16
I32 = jnp.int32
U32 = jnp.uint32

CONV_KERNEL = 31
POOL_WINDOWS = (2, 4, 8, 16)
N_EXPERT_GROUPS = 4
LN_EPS = 1e-5

SUBLANES = 8
LANES = 128

SEQ_TILE = 1024
SEQ_SUB_TILE = 256
STAGE_LEAD = 4
EXPERT_TILE = 1024
EXPERT_F_SPLIT = 2
COMBINE_TILE = 1024
BATCH_CHUNKS = 2
CONV_HALO_VREGS = 32
POOL_HALO = 16
VMEM_LIMIT = 56 * 1024 * 1024
SC_CORES = 2
SC_SUBCORES = 16
SC_ROWS_PER_STEP = 64


def _split_bf16(a):
    hi = a.astype(BF16)
    lo = (a - hi.astype(F32)).astype(BF16)
    return hi, lo


def _dot(a, b):
    return jnp.dot(a, b, preferred_element_type=F32)


def _pack_halves(y):
    h = y.shape[1] // 2
    lo = lax.bitcast_convert_type(y[:, :h].astype(BF16).astype(F32), U32)
    hi = lax.bitcast_convert_type(y[:, h:].astype(BF16).astype(F32), U32)
    return (lo >> 16) | hi


def _unpack_halves(p):
    lo = lax.bitcast_convert_type(p << 16, F32)
    hi = lax.bitcast_convert_type(p & jnp.uint32(0xFFFF0000), F32)
    return lo, hi


def _ada_body(c_ref, w_ref, b_ref, o_ref):
    c = c_ref[...]
    ca = c * jax.nn.sigmoid(c)
    chi, clo = _split_bf16(ca)
    whi, wlo = _split_bf16(w_ref[0])
    o_ref[0] = _dot(chi, whi) + _dot(chi, wlo) + _dot(clo, whi) + b_ref[0]


def _ada_mod(c, w_ada, b_ada):
    n_layers, d, n6 = w_ada.shape
    bsz = c.shape[0]
    tn = n6 // 6
    return pl.pallas_call(
        _ada_body,
        grid=(n_layers, n6 // tn),
        in_specs=[
            pl.BlockSpec((bsz, d), lambda l, j: (0, 0)),
            pl.BlockSpec((1, d, tn), lambda l, j: (l, 0, j)),
            pl.BlockSpec((1, 1, tn), lambda l, j: (l, 0, j)),
        ],
        out_specs=pl.BlockSpec((1, bsz, tn), lambda l, j: (l, 0, j)),
        out_shape=jax.ShapeDtypeStruct((n_layers, bsz, n6), F32),
        compiler_params=pltpu.CompilerParams(dimension_semantics=("arbitrary", "arbitrary"), vmem_limit_bytes=VMEM_LIMIT),
        name="ada_mod",
    )(c, w_ada, b_ada.reshape(n_layers, 1, n6))


def _layer_norm_rows(z, g, b, eps=LN_EPS):
    mu = jnp.mean(z, axis=-1, keepdims=True)
    zc = z - mu
    var = jnp.mean(zc * zc, axis=-1, keepdims=True)
    return zc * lax.rsqrt(var + eps) * g + b


def _deepnorm_rows(alpha, x, gate, branch, g, b):
    return _layer_norm_rows(x + (gate * (1.0 / alpha)) * branch, g, b, LN_EPS / (alpha * alpha))


def _combine_rows(alpha, x1, y0, y1, wm, g2, ln_g, ln_b):
    lo0, hi0 = _unpack_halves(y0)
    lo1, hi1 = _unpack_halves(y1)
    w0, w1 = wm[:, 0:1], wm[:, 1:2]
    f = jnp.concatenate([w0 * lo0 + w1 * lo1, w0 * hi0 + w1 * hi1], axis=1)
    return _deepnorm_rows(alpha, x1, 1.0 + g2, f, ln_g, ln_b)


def _mixer_body(alpha, n_experts, sub, fused, *refs):
    if fused:
        xp_ref, y0_ref, y1_ref, wmp_ref, modp_ref, l2g_ref, l2b_ref = refs[:7]
        refs = refs[7:]
        x_ref = None
    else:
        x_ref = refs[0]
        refs = refs[1:]
    (mod_ref, win_ref, bin_ref, cw_ref, cb_ref, cg_ref, cbeta_ref, pw_ref, ps_ref,
     wout_ref, bout_ref, l1g_ref, l1b_ref, wr_ref, rb_ref, perm_ref, permt_ref, tri_ref,
     x1_ref, u2_ref, e_ref, rank_ref, wm_ref, cnt_ref,
     ebuf, cbuf, pbuf, vprev, phalo, carry, xbuf) = refs
    ts = x1_ref.shape[1]
    cw = cg_ref.shape[1]
    pw = ps_ref.shape[1]
    gw = pw // len(POOL_WINDOWS)
    fine = sub // SUBLANES
    halo = CONV_HALO_VREGS
    b = pl.program_id(0)
    i = pl.program_id(1)

    @pl.when(i == 0)
    def _():
        vprev[...] = jnp.zeros(vprev.shape, F32)
        phalo[...] = jnp.zeros(phalo.shape, F32)

    @pl.when(jnp.logical_and(b == 0, i == 0))
    def _():
        carry[...] = jnp.zeros(carry.shape, F32)

    mod = mod_ref[0]
    sh1, sc1, g1 = mod[0:1], mod[1:2], mod[2:3]
    sh2, sc2 = mod[3:4], mod[4:5]

    def stage1(s):
        r0 = s * sub
        if fused:
            x = _combine_rows(alpha, xp_ref[0, r0:r0 + sub, :], y0_ref[r0:r0 + sub, :], y1_ref[r0:r0 + sub, :],
                              wmp_ref[r0:r0 + sub, :], modp_ref[0][5:6], l2g_ref[...], l2b_ref[...])
            xbuf[r0:r0 + sub, :] = x
        else:
            x = x_ref[0, r0:r0 + sub, :]
        u = (x * (1.0 + sc1) + sh1).astype(BF16)
        h = _dot(u, win_ref[...]) + bin_ref[...]
        a = h[:, :cw]
        gate = h[:, cw:2 * cw]
        p = h[:, 2 * cw:]

        v = (a * jax.nn.sigmoid(gate)).astype(BF16)
        vb = _dot(perm_ref[...], v)
        ebuf[s, halo * SUBLANES:, :] = vb
        sl = lax.broadcasted_iota(I32, (SUBLANES, cw), 0)
        for mm in range(halo):
            src = (fine - halo + mm) * SUBLANES
            cur = vb[src:src + SUBLANES, :]
            prev = vprev[mm * SUBLANES:(mm + 1) * SUBLANES, :]
            ebuf[s, mm * SUBLANES:(mm + 1) * SUBLANES, :] = pltpu.roll(
                jnp.where(sl == SUBLANES - 1, prev, cur), 1, 0)
        vprev[...] = vb[(fine - halo) * SUBLANES:, :]
        pbuf[s, 0:POOL_HALO, :] = phalo[...]
        pbuf[s, POOL_HALO:, :] = p
        phalo[...] = p[sub - POOL_HALO:, :]

    def stage2(s):
        r0 = s * sub
        x = xbuf[r0:r0 + sub, :] if fused else x_ref[0, r0:r0 + sub, :]
        chunk = 8 * SUBLANES
        first = halo - (CONV_KERNEL - 1)
        for q0 in range(0, sub, chunk):
            for c0 in range(0, cw, LANES):
                acc = jnp.broadcast_to(cb_ref[:, c0:c0 + LANES], (chunk, LANES))
                for k in range(CONV_KERNEL):
                    off = q0 + (first + k) * SUBLANES
                    acc = acc + cw_ref[k:k + 1, c0:c0 + LANES] * ebuf[s, off:off + chunk, c0:c0 + LANES]
                cbuf[s, q0:q0 + chunk, c0:c0 + LANES] = acc
        yc = _layer_norm_rows(cbuf[s], cg_ref[...], cbeta_ref[...])
        ya_b = (yc * jax.nn.sigmoid(yc)).astype(BF16)
        ya = _dot(permt_ref[...], ya_b).astype(BF16)

        ext = pbuf[s]
        p = ext[POOL_HALO:, :]
        w_sum = ext + pltpu.roll(ext, 1, 0)
        parts = [w_sum[:, :gw]]
        shift = 2
        for _ in range(len(POOL_WINDOWS) - 1):
            w_sum = w_sum[:, gw:]
            w_sum = w_sum + pltpu.roll(w_sum, shift, 0)
            parts.append(w_sum[:, :gw])
            shift *= 2
        wsum = jnp.concatenate(parts, axis=1)[POOL_HALO:, :]
        pos = (lax.broadcasted_iota(I32, (POOL_HALO, gw), 0) + (i * ts + r0 + 1)).astype(F32)
        cnt = jnp.concatenate([jnp.minimum(pos, float(wlen)) for wlen in POOL_WINDOWS], axis=1)
        inv = jnp.concatenate([jnp.full((1, gw), 1.0 / wlen, F32) for wlen in POOL_WINDOWS], axis=1)
        mean = jnp.concatenate([wsum[:POOL_HALO] / cnt, wsum[POOL_HALO:] * inv], axis=0)
        dpool = (mean - p).astype(BF16)
        yb_parts = [_dot(dpool[:, gi * gw:(gi + 1) * gw], pw_ref[gi]) for gi in range(len(POOL_WINDOWS))]
        yb = (jnp.concatenate(yb_parts, axis=1) * ps_ref[...]).astype(BF16)

        mix = _dot(jnp.concatenate([ya, yb], axis=1), wout_ref[...]) + bout_ref[...]
        x1 = _deepnorm_rows(alpha, x, 1.0 + g1, mix, l1g_ref[...], l1b_ref[...])
        x1_ref[0, r0:r0 + sub, :] = x1
        u2 = x1 * (1.0 + sc2) + sh2
        u2_ref[0, r0:r0 + sub, :] = _pack_halves(u2)

        logits = _dot(u2.astype(BF16), wr_ref[...]).T[0:n_experts, :]
        scores = jax.nn.sigmoid(logits)
        sel = scores + rb_ref[...]
        epg = n_experts // N_EXPERT_GROUPS
        io = lax.broadcasted_iota(I32, (epg, sub), 0).astype(F32)
        neg = jnp.float32(-jnp.inf)
        best = None
        for g in range(N_EXPERT_GROUPS):
            sg = sel[g * epg:(g + 1) * epg, :]
            scg = scores[g * epg:(g + 1) * epg, :]
            m1 = jnp.max(sg, axis=0, keepdims=True)
            i1 = jnp.min(jnp.where(sg == m1, io, float(epg)), axis=0, keepdims=True)
            rest = jnp.where(io == i1, neg, sg)
            m2 = jnp.max(rest, axis=0, keepdims=True)
            i2 = jnp.min(jnp.where(jnp.logical_and(rest == m2, io != i1), io, float(epg)), axis=0, keepdims=True)
            s1 = jnp.sum(jnp.where(io == i1, scg, 0.0), axis=0, keepdims=True)
            s2 = jnp.sum(jnp.where(io == i2, scg, 0.0), axis=0, keepdims=True)
            gs = m1 + m2
            cand = (gs, i1 + float(g * epg), i2 + float(g * epg), s1, s2)
            if best is None:
                best = cand
            else:
                upd = gs > best[0]
                best = tuple(jnp.where(upd, cn, bs) for cn, bs in zip(cand, best))
        _, e1, e2, s1, s2 = best
        denom = s1 + s2
        w1 = s1 / denom
        w2 = s2 / denom

        ioe = lax.broadcasted_iota(I32, (n_experts, sub), 0).astype(F32)
        hit1 = ioe == e1
        hit2 = ioe == e2
        onehot = jnp.where(jnp.logical_or(hit1, hit2), 1.0, 0.0)
        before = _dot(onehot.astype(BF16), tri_ref[...]) + carry[:, 0:1]
        r1 = jnp.sum(jnp.where(hit1, before, 0.0), axis=0, keepdims=True)
        r2 = jnp.sum(jnp.where(hit2, before, 0.0), axis=0, keepdims=True)
        carry[...] = carry[...] + jnp.sum(onehot, axis=1, keepdims=True)

        e_ref[0:1, r0:r0 + sub] = e1.astype(I32)
        e_ref[1:2, r0:r0 + sub] = e2.astype(I32)
        rank_ref[0:1, r0:r0 + sub] = r1.astype(I32)
        rank_ref[1:2, r0:r0 + sub] = r2.astype(I32)
        rowi = lax.broadcasted_iota(I32, (LANES, sub), 0)
        wpad = jnp.where(rowi == 0, w1, jnp.where(rowi == 1, w2, 0.0))
        wm_ref[r0:r0 + sub, :] = wpad.T

    nsub = ts // sub
    for s in range(min(STAGE_LEAD, nsub)):
        stage1(s)
    for s in range(nsub):
        if s + STAGE_LEAD < nsub:
            stage1(s + STAGE_LEAD)
        stage2(s)
    cnt_ref[...] = carry[...]


def _row_perm(n):
    r = jnp.arange(n)
    t = (n // SUBLANES) * (r % SUBLANES) + r // SUBLANES
    return (t[:, None] == jnp.arange(n)[None, :]).astype(BF16)


def _mixer(x, mod, lw, wr, rbias, alpha, bsz, x_b0, mod_b0, prev=None):
    _, seq, d = x.shape
    ts = min(SEQ_TILE, seq)
    sub = min(SEQ_SUB_TILE, ts)
    nsub = ts // sub
    ns = seq // ts
    n_tok = bsz * seq
    n_experts = rbias.shape[0]
    cw = lw["conv_w"].shape[1]
    pw = lw["pool_scale"].shape[0]
    perm = _row_perm(sub)
    tri = (jnp.arange(sub)[:, None] < jnp.arange(sub)[None, :]).astype(BF16)
    row = lambda a: a.reshape(1, -1)
    full = lambda shape: pl.BlockSpec(shape, lambda b, i: (0,) * len(shape))
    if prev is None:
        ins = [(x, pl.BlockSpec((1, ts, d), lambda b, i: (b + x_b0, i, 0)))]
    else:
        yy, wmp, modp, l2g, l2b = prev
        nblk = n_tok // ts
        ins = [
            (x, pl.BlockSpec((1, ts, d), lambda b, i: (b + x_b0, i, 0))),
            (yy, pl.BlockSpec((ts, d // 2), lambda b, i: (b * ns + i, 0))),
            (yy, pl.BlockSpec((ts, d // 2), lambda b, i: (nblk + b * ns + i, 0))),
            (wmp, pl.BlockSpec((ts, LANES), lambda b, i: (b * ns + i, 0))),
            (modp, pl.BlockSpec((1, 6, d), lambda b, i: (b + mod_b0, 0, 0))),
            (row(l2g), None), (row(l2b), None),
        ]
    ins += [
        (mod, pl.BlockSpec((1, 6, d), lambda b, i: (b + mod_b0, 0, 0))),
        (lw["w_in"], None), (row(lw["b_in"]), None),
        (jnp.pad(lw["conv_w"], ((0, 1), (0, 0))), None), (row(lw["conv_b"]), None),
        (row(lw["conv_ln_g"]), None), (row(lw["conv_ln_b"]), None),
        (lw["pool_w"], None), (row(lw["pool_scale"]), None),
        (lw["w_out"], None), (row(lw["b_out"]), None),
        (row(lw["ln1_g"]), None), (row(lw["ln1_b"]), None),
        (wr, None), (rbias.reshape(-1, 1), None),
        (perm, None), (perm.T, None), (tri, None),
    ]
    args = [a for a, _ in ins]
    specs = [s if s is not None else full(a.shape) for a, s in ins]
    out_shape = (
        jax.ShapeDtypeStruct((bsz, seq, d), F32),
        jax.ShapeDtypeStruct((bsz, seq, d // 2), U32),
        jax.ShapeDtypeStruct((2, n_tok), I32),
        jax.ShapeDtypeStruct((2, n_tok), I32),
        jax.ShapeDtypeStruct((n_tok, LANES), F32),
        jax.ShapeDtypeStruct((n_experts, LANES), F32),
    )
    out_specs = (
        pl.BlockSpec((1, ts, d), lambda b, i: (b, i, 0)),
        pl.BlockSpec((1, ts, d // 2), lambda b, i: (b, i, 0)),
        pl.BlockSpec((2, ts), lambda b, i: (0, b * ns + i)),
        pl.BlockSpec((2, ts), lambda b, i: (0, b * ns + i)),
        pl.BlockSpec((ts, LANES), lambda b, i: (b * ns + i, 0)),
        pl.BlockSpec((n_experts, LANES), lambda b, i: (0, 0)),
    )
    return pl.pallas_call(
        functools.partial(_mixer_body, alpha, n_experts, sub, prev is not None),
        grid=(bsz, ns),
        in_specs=specs,
        out_specs=out_specs,
        out_shape=out_shape,
        scratch_shapes=[
            pltpu.VMEM((nsub, CONV_HALO_VREGS * SUBLANES + sub, cw), F32),
            pltpu.VMEM((nsub, sub, cw), F32),
            pltpu.VMEM((nsub, POOL_HALO + sub, pw), F32),
            pltpu.VMEM((CONV_HALO_VREGS * SUBLANES, cw), F32),
            pltpu.VMEM((POOL_HALO, pw), F32),
            pltpu.VMEM((n_experts, LANES), F32),
            pltpu.VMEM((ts, d) if prev is not None else (SUBLANES, LANES), F32),
        ],
        compiler_params=pltpu.CompilerParams(dimension_semantics=("arbitrary", "arbitrary"), vmem_limit_bytes=VMEM_LIMIT),
        name="mixer_router",
    )(*args)


def _dest_body(n_experts, pstart_ref, e_ref, rank_ref, dest_ref):
    e = e_ref[...]
    dest = rank_ref[...]
    for ex in range(n_experts):
        dest = dest + jnp.where(e == ex, pstart_ref[ex], 0)
    dest_ref[...] = dest


def _dest_rows(pstart, eidx, rank):
    n_tok = eidx.shape[1]
    tn = min(8192, n_tok)
    grid_spec = pltpu.PrefetchScalarGridSpec(
        num_scalar_prefetch=1,
        grid=(n_tok // tn,),
        in_specs=[pl.BlockSpec((2, tn), lambda i, ps: (0, i)), pl.BlockSpec((2, tn), lambda i, ps: (0, i))],
        out_specs=pl.BlockSpec((2, tn), lambda i, ps: (0, i)),
    )
    return pl.pallas_call(
        functools.partial(_dest_body, pstart.shape[0]),
        grid_spec=grid_spec,
        out_shape=jax.ShapeDtypeStruct((2, n_tok), I32),
        compiler_params=pltpu.CompilerParams(dimension_semantics=("arbitrary",)),
        name="dest_rows",
    )(pstart, eidx, rank)


def _sc_mesh():
    return plsc.VectorSubcoreMesh(core_axis_name="c", subcore_axis_name="s",
                                  num_cores=SC_CORES, num_subcores=SC_SUBCORES)


def _dispatch(u2, d0, d1, n_rows):
    n_tok, d = u2.shape
    workers = SC_CORES * SC_SUBCORES
    per_w = n_tok // workers
    ch = SC_ROWS_PER_STEP
    nch = per_w // ch

    @functools.partial(
        pl.kernel, mesh=_sc_mesh(),
        out_type=jax.ShapeDtypeStruct((n_rows, d), u2.dtype),
        scratch_types=[pltpu.VMEM((nch, ch), I32), pltpu.VMEM((nch, ch), I32), pltpu.VMEM((2, ch, d), u2.dtype),
                       pltpu.SemaphoreType.DMA((2,)), pltpu.SemaphoreType.DMA((2,))],
        name="sc_dispatch",
    )
    def run(u_hbm, d0_hbm, d1_hbm, xs_hbm, i0, i1, rows, rsem, ssem):
        wid = lax.axis_index("s") * SC_CORES + lax.axis_index("c")
        base = wid * per_w
        pltpu.sync_copy(d0_hbm.at[pl.ds(wid * nch, nch)], i0)
        pltpu.sync_copy(d1_hbm.at[pl.ds(wid * nch, nch)], i1)

        def read(j, b):
            return pltpu.make_async_copy(u_hbm.at[pl.ds(base + j * ch, ch)], rows.at[b], rsem.at[b])

        def scatter(idx, j, b):
            return pltpu.make_async_copy(rows.at[b], xs_hbm.at[idx.at[j]], ssem.at[b])

        read(0, 0).start()

        @pl.loop(0, nch, step=2)
        def _(j0):
            for b in range(2):
                j = j0 + b
                read(j, b).wait()
                scatter(i0, j, b).start()
                scatter(i1, j, b).start()

                @pl.when(j >= 1)
                def _():
                    scatter(i0, j - 1, 1 - b).wait()
                    scatter(i1, j - 1, 1 - b).wait()

                @pl.when(j + 1 < nch)
                def _():
                    read(j + 1, 1 - b).start()

        scatter(i0, nch - 1, 1).wait()
        scatter(i1, nch - 1, 1).wait()

    return run(u2, d0.reshape(n_tok // ch, ch), d1.reshape(n_tok // ch, ch))


def _gather_rows(ys, dd):
    n_out = dd.shape[0]
    d = ys.shape[1]
    workers = SC_CORES * SC_SUBCORES
    per_w = n_out // workers
    ch = SC_ROWS_PER_STEP
    nch = per_w // ch

    @functools.partial(
        pl.kernel, mesh=_sc_mesh(),
        out_type=jax.ShapeDtypeStruct((n_out, d), ys.dtype),
        scratch_types=[pltpu.VMEM((nch, ch), I32), pltpu.VMEM((2, ch, d), ys.dtype),
                       pltpu.SemaphoreType.DMA((2,)), pltpu.SemaphoreType.DMA((2,))],
        name="sc_gather",
    )
    def run(ys_hbm, dd_hbm, out_hbm, idx, rows, gsem, wsem):
        wid = lax.axis_index("s") * SC_CORES + lax.axis_index("c")
        base = wid * per_w
        pltpu.sync_copy(dd_hbm.at[pl.ds(wid * nch, nch)], idx)

        def gather(j, b):
            return pltpu.make_async_copy(ys_hbm.at[idx.at[j]], rows.at[b], gsem.at[b])

        def write(j, b):
            return pltpu.make_async_copy(rows.at[b], out_hbm.at[pl.ds(base + j * ch, ch)], wsem.at[b])

        gather(0, 0).start()

        @pl.loop(0, nch, step=2)
        def _(j0):
            for b in range(2):
                j = j0 + b
                gather(j, b).wait()
                write(j, b).start()

                @pl.when(j >= 1)
                def _():
                    write(j - 1, 1 - b).wait()

                @pl.when(j + 1 < nch)
                def _():
                    gather(j + 1, 1 - b).start()

        write(nch - 1, 1).wait()

    return run(ys, dd.reshape(n_out // ch, ch))


def _expert_body(layer, blk_ref, slot_ref, next_ref, nused_ref, xs_ref, wg_hbm, wu_hbm, wd_hbm, ys_ref,
                 wg32, wu32, wd32, wgb, wub, wdb, sem):
    j = pl.program_id(0)
    used = j < nused_ref[0]
    e = blk_ref[j]
    slot = slot_ref[j]
    nxt = next_ref[j]
    new_expert = jnp.logical_or(j == 0, e != blk_ref[jnp.maximum(j - 1, 0)])

    def fetch(ex, sl):
        return (pltpu.make_async_copy(wg_hbm.at[layer, ex], wg32.at[sl], sem.at[sl, 0]),
                pltpu.make_async_copy(wu_hbm.at[layer, ex], wu32.at[sl], sem.at[sl, 1]),
                pltpu.make_async_copy(wd_hbm.at[layer, ex], wd32.at[sl], sem.at[sl, 2]))

    @pl.when(j == 0)
    def _():
        for cp in fetch(e, slot):
            cp.start()

    @pl.when(jnp.logical_and(used, new_expert))
    def _():
        for cp in fetch(e, slot):
            cp.wait()

        @pl.when(nxt >= 0)
        def _():
            for cp in fetch(nxt, 1 - slot):
                cp.start()

        wgb[...] = wg32[slot].astype(BF16)
        wub[...] = wu32[slot].astype(BF16)
        wdb[...] = wd32[slot].astype(BF16)

    @pl.when(used)
    def _():
        lo, hi = _unpack_halves(xs_ref[...])
        lo = lo.astype(BF16)
        hi = hi.astype(BF16)
        half = lo.shape[1]
        fw = wgb.shape[1] // EXPERT_F_SPLIT
        acc = None
        for c in range(EXPERT_F_SPLIT):
            cs = slice(c * fw, (c + 1) * fw)
            g = _dot(lo, wgb[0:half, cs]) + _dot(hi, wgb[half:, cs])
            up = _dot(lo, wub[0:half, cs]) + _dot(hi, wub[half:, cs])
            hid = (g * jax.nn.sigmoid(g) * up).astype(BF16)
            part = _dot(hid, wdb[cs, :])
            acc = part if acc is None else acc + part
        ys_ref[...] = _pack_halves(acc)

    @pl.when(j >= nused_ref[0])
    def _():
        ys_ref[...] = jnp.zeros(ys_ref.shape, U32)


def _experts(blk_e, blk_slot, blk_next, nused, xs, wg, wu, wd, layer):
    n_rows = xs.shape[0]
    d = wg.shape[2]
    f = wg.shape[3]
    tm = EXPERT_TILE
    grid_spec = pltpu.PrefetchScalarGridSpec(
        num_scalar_prefetch=4,
        grid=(n_rows // tm,),
        in_specs=[
            pl.BlockSpec((tm, d // 2), lambda j, be, sl, nx, nu: (jnp.minimum(j, nu[0] - 1), 0)),
            pl.BlockSpec(memory_space=pl.ANY),
            pl.BlockSpec(memory_space=pl.ANY),
            pl.BlockSpec(memory_space=pl.ANY),
        ],
        out_specs=pl.BlockSpec((tm, d // 2), lambda j, be, sl, nx, nu: (j, 0)),
        scratch_shapes=[pltpu.VMEM((2, d, f), F32), pltpu.VMEM((2, d, f), F32), pltpu.VMEM((2, f, d), F32),
                        pltpu.VMEM((d, f), BF16), pltpu.VMEM((d, f), BF16), pltpu.VMEM((f, d), BF16),
                        pltpu.SemaphoreType.DMA((2, 3))],
    )
    return pl.pallas_call(
        functools.partial(_expert_body, layer),
        grid_spec=grid_spec,
        out_shape=jax.ShapeDtypeStruct((n_rows, d // 2), U32),
        compiler_params=pltpu.CompilerParams(dimension_semantics=("arbitrary",), vmem_limit_bytes=VMEM_LIMIT),
        name="expert_ffn",
    )(blk_e, blk_slot, blk_next, nused, xs, wg, wu, wd)


def _combine_body(alpha, x1_ref, y0_ref, y1_ref, mod_ref, wm_ref, g_ref, b_ref, *rest):
    o_ref = rest[-1]
    o_ref[...] = _combine_rows(alpha, x1_ref[...], y0_ref[...], y1_ref[...], wm_ref[...],
                               mod_ref[0][5:6], g_ref[...], b_ref[...])


def _combine(x1, yy, mod, wm, ln_g, ln_b, seq, alpha, mod_b0, out_rows, out_row0, out_prev):
    n_tok, d = x1.shape
    tn = min(COMBINE_TILE, seq)
    per_seq = seq // tn
    nblk = n_tok // tn
    blk0 = out_row0 // tn
    rows = lambda: pl.BlockSpec((tn, d), lambda i: (i, 0))
    in_specs = [rows(), pl.BlockSpec((tn, d // 2), lambda i: (i, 0)),
                pl.BlockSpec((tn, d // 2), lambda i: (nblk + i, 0)),
                pl.BlockSpec((1, 6, d), lambda i: (i // per_seq + mod_b0, 0, 0)),
                pl.BlockSpec((tn, LANES), lambda i: (i, 0)),
                pl.BlockSpec((1, d), lambda i: (0, 0)),
                pl.BlockSpec((1, d), lambda i: (0, 0))]
    args = [x1, yy, yy, mod, wm, ln_g.reshape(1, d), ln_b.reshape(1, d)]
    aliases = {}
    if out_prev is not None:
        in_specs.append(pl.BlockSpec(memory_space=pl.ANY))
        args.append(out_prev)
        aliases = {len(args) - 1: 0}
    return pl.pallas_call(
        functools.partial(_combine_body, alpha),
        grid=(nblk,),
        in_specs=in_specs,
        out_specs=pl.BlockSpec((tn, d), lambda i: (blk0 + i, 0)),
        out_shape=jax.ShapeDtypeStruct((out_rows, d), F32),
        input_output_aliases=aliases,
        compiler_params=pltpu.CompilerParams(dimension_semantics=("arbitrary",), vmem_limit_bytes=VMEM_LIMIT),
        name="combine_ln",
    )(*args)


def kernel(x, c, w_ada, b_ada, w_in, b_in, conv_w, conv_b, conv_ln_g, conv_ln_b, pool_w, pool_scale, w_out, b_out, ln1_g, ln1_b, w_router, router_bias, w_gate, w_up, w_down, ln2_g, ln2_b):
    bsz, seq, d = x.shape
    depth = w_ada.shape[0]
    n_experts = w_router.shape[1]
    alpha = float((2 * depth) ** 0.25)
    tm = EXPERT_TILE
    n_chunks = BATCH_CHUNKS if bsz % BATCH_CHUNKS == 0 else 1
    bc = bsz // n_chunks
    n_tok = bc * seq
    n_rows = 2 * n_tok + n_experts * tm

    mod_all = _ada_mod(c, w_ada, b_ada).reshape(depth, bsz, 6, d)
    wr = jnp.pad(w_router.astype(BF16), ((0, 0), (0, LANES - n_experts)))

    chunks = [x] * n_chunks
    starts = [ci * bc for ci in range(n_chunks)]
    prevs = [None] * n_chunks
    out = None
    for l in range(depth):
        lw = dict(w_in=w_in[l].astype(BF16), b_in=b_in[l], conv_w=conv_w[l], conv_b=conv_b[l],
                  conv_ln_g=conv_ln_g[l], conv_ln_b=conv_ln_b[l], pool_w=pool_w[l].astype(BF16),
                  pool_scale=pool_scale[l], w_out=w_out[l].astype(BF16), b_out=b_out[l],
                  ln1_g=ln1_g[l], ln1_b=ln1_b[l])
        mod = mod_all[l]
        last = l == depth - 1
        st = []
        for ci in range(n_chunks):
            x1, u2, eidx, rank, wm, cnt = _mixer(chunks[ci], mod, lw, wr, router_bias, alpha,
                                                 bc, starts[ci], ci * bc, prevs[ci])
            counts = cnt[:, 0].astype(I32)
            tiles = (counts + tm - 1) // tm
            tile_end = jnp.cumsum(tiles)
            pstart = ((tile_end - tiles) * tm).astype(I32)
            nused = tile_end[-1:].astype(I32)
            blk_e = jnp.minimum(
                jnp.sum(tile_end[None, :] <= jnp.arange(n_rows // tm, dtype=I32)[:, None], axis=1), n_experts - 1
            ).astype(I32)
            in_use = tiles > 0
            slot_e = (jnp.cumsum(in_use.astype(I32)) - 1) % 2
            ids = jnp.where(in_use, jnp.arange(n_experts, dtype=I32), n_experts)
            after = jnp.concatenate([lax.cummin(ids, reverse=True)[1:], jnp.full((1,), n_experts, I32)])
            next_e = jnp.where(after < n_experts, after, -1).astype(I32)
            dest = _dest_rows(pstart, eidx, rank)
            xs = _dispatch(u2.reshape(n_tok, d // 2), dest[0], dest[1], n_rows)
            st.append((x1, wm, (blk_e, slot_e[blk_e].astype(I32), next_e[blk_e]), nused, dest, xs))
        ys = [_experts(*plan, nused, xs, w_gate, w_up, w_down, l) for (_, _, plan, nused, _, xs) in st]
        yy = [_gather_rows(ys[ci], st[ci][4].reshape(2 * n_tok)) for ci in range(n_chunks)]
        for ci in range(n_chunks):
            x1, wm = st[ci][0], st[ci][1]
            if last:
                out = _combine(x1.reshape(n_tok, d), yy[ci], mod, wm, ln2_g[l], ln2_b[l], seq, alpha,
                               ci * bc, bsz * seq, ci * n_tok, out)
            else:
                chunks[ci] = x1
                starts[ci] = 0
                prevs[ci] = (yy[ci], wm, mod, ln2_g[l], ln2_b[l])
    return out.reshape(bsz, seq, d)
```

```python
import functools

import jax
import jax.numpy as jnp
from jax import lax
from jax.experimental import pallas as pl
from jax.experimental.pallas import tpu as pltpu
from jax.experimental.pallas import tpu_sc as plsc

F32 = jnp.float32
BF16 = jnp.bfloat16
I32 = jnp.int32
U32 = jnp.uint32

CONV_KERNEL = 31
POOL_WINDOWS = (2, 4, 8, 16)
N_EXPERT_GROUPS = 4
LN_EPS = 1e-5

SUBLANES = 8
LANES = 128

SEQ_TILE = 1024
SEQ_SUB_TILE = 256
STAGE_LEAD = 4
EXPERT_TILE = 1024
EXPERT_F_SPLIT = 2
COMBINE_TILE = 1024
BATCH_CHUNKS = 2
CONV_HALO_VREGS = 32
POOL_HALO = 16
VMEM_LIMIT = 56 * 1024 * 1024
SC_CORES = 2
SC_SUBCORES = 16
SC_ROWS_PER_STEP = 64


def _split_bf16(a):
    hi = a.astype(BF16)
    lo = (a - hi.astype(F32)).astype(BF16)
    return hi, lo


def _dot(a, b):
    return jnp.dot(a, b, preferred_element_type=F32)


def _pack_halves(y):
    h = y.shape[1] // 2
    lo = lax.bitcast_convert_type(y[:, :h].astype(BF16).astype(F32), U32)
    hi = lax.bitcast_convert_type(y[:, h:].astype(BF16).astype(F32), U32)
    return (lo >> 16) | hi


def _unpack_halves(p):
    lo = lax.bitcast_convert_type(p << 16, F32)
    hi = lax.bitcast_convert_type(p & jnp.uint32(0xFFFF0000), F32)
    return lo, hi


def _ada_body(c_ref, w_ref, b_ref, o_ref):
    c = c_ref[...]
    ca = c * jax.nn.sigmoid(c)
    chi, clo = _split_bf16(ca)
    whi, wlo = _split_bf16(w_ref[0])
    o_ref[0] = _dot(chi, whi) + _dot(chi, wlo) + _dot(clo, whi) + b_ref[0]


def _ada_mod(c, w_ada, b_ada):
    n_layers, d, n6 = w_ada.shape
    bsz = c.shape[0]
    tn = n6 // 6
    return pl.pallas_call(
        _ada_body,
        grid=(n_layers, n6 // tn),
        in_specs=[
            pl.BlockSpec((bsz, d), lambda l, j: (0, 0)),
            pl.BlockSpec((1, d, tn), lambda l, j: (l, 0, j)),
            pl.BlockSpec((1, 1, tn), lambda l, j: (l, 0, j)),
        ],
        out_specs=pl.BlockSpec((1, bsz, tn), lambda l, j: (l, 0, j)),
        out_shape=jax.ShapeDtypeStruct((n_layers, bsz, n6), F32),
        compiler_params=pltpu.CompilerParams(dimension_semantics=("arbitrary", "arbitrary"), vmem_limit_bytes=VMEM_LIMIT),
        name="ada_mod",
    )(c, w_ada, b_ada.reshape(n_layers, 1, n6))


def _layer_norm_rows(z, g, b, eps=LN_EPS):
    mu = jnp.mean(z, axis=-1, keepdims=True)
    zc = z - mu
    var = jnp.mean(zc * zc, axis=-1, keepdims=True)
    return zc * lax.rsqrt(var + eps) * g + b


def _deepnorm_rows(alpha, x, gate, branch, g, b):
    return _layer_norm_rows(x + (gate * (1.0 / alpha)) * branch, g, b, LN_EPS / (alpha * alpha))


def _combine_rows(alpha, x1, y0, y1, wm, g2, ln_g, ln_b):
    lo0, hi0 = _unpack_halves(y0)
    lo1, hi1 = _unpack_halves(y1)
    w0, w1 = wm[:, 0:1], wm[:, 1:2]
    f = jnp.concatenate([w0 * lo0 + w1 * lo1, w0 * hi0 + w1 * hi1], axis=1)
    return _deepnorm_rows(alpha, x1, 1.0 + g2, f, ln_g, ln_b)


def _mixer_body(alpha, n_experts, sub, fused, *refs):
    if fused:
        xp_ref, y0_ref, y1_ref, wmp_ref, modp_ref, l2g_ref, l2b_ref = refs[:7]
        refs = refs[7:]
        x_ref = None
    else:
        x_ref = refs[0]
        refs = refs[1:]
    (mod_ref, win_ref, bin_ref, cw_ref, cb_ref, cg_ref, cbeta_ref, pw_ref, ps_ref,
     wout_ref, bout_ref, l1g_ref, l1b_ref, wr_ref, rb_ref, perm_ref, permt_ref, tri_ref,
     x1_ref, u2_ref, e_ref, rank_ref, wm_ref, cnt_ref,
     ebuf, cbuf, pbuf, vprev, phalo, carry, xbuf) = refs
    ts = x1_ref.shape[1]
    cw = cg_ref.shape[1]
    pw = ps_ref.shape[1]
    gw = pw // len(POOL_WINDOWS)
    fine = sub // SUBLANES
    halo = CONV_HALO_VREGS
    b = pl.program_id(0)
    i = pl.program_id(1)

    @pl.when(i == 0)
    def _():
        vprev[...] = jnp.zeros(vprev.shape, F32)
        phalo[...] = jnp.zeros(phalo.shape, F32)

    @pl.when(jnp.logical_and(b == 0, i == 0))
    def _():
        carry[...] = jnp.zeros(carry.shape, F32)

    mod = mod_ref[0]
    sh1, sc1, g1 = mod[0:1], mod[1:2], mod[2:3]
    sh2, sc2 = mod[3:4], mod[4:5]

    def stage1(s):
        r0 = s * sub
        if fused:
            x = _combine_rows(alpha, xp_ref[0, r0:r0 + sub, :], y0_ref[r0:r0 + sub, :], y1_ref[r0:r0 + sub, :],
                              wmp_ref[r0:r0 + sub, :], modp_ref[0][5:6], l2g_ref[...], l2b_ref[...])
            xbuf[r0:r0 + sub, :] = x
        else:
            x = x_ref[0, r0:r0 + sub, :]
        u = (x * (1.0 + sc1) + sh1).astype(BF16)
        h = _dot(u, win_ref[...]) + bin_ref[...]
        a = h[:, :cw]
        gate = h[:, cw:2 * cw]
        p = h[:, 2 * cw:]

        v = (a * jax.nn.sigmoid(gate)).astype(BF16)
        vb = _dot(perm_ref[...], v)
        ebuf[s, halo * SUBLANES:, :] = vb
        sl = lax.broadcasted_iota(I32, (SUBLANES, cw), 0)
        for mm in range(halo):
            src = (fine - halo + mm) * SUBLANES
            cur = vb[src:src + SUBLANES, :]
            prev = vprev[mm * SUBLANES:(mm + 1) * SUBLANES, :]
            ebuf[s, mm * SUBLANES:(mm + 1) * SUBLANES, :] = pltpu.roll(
                jnp.where(sl == SUBLANES - 1, prev, cur), 1, 0)
        vprev[...] = vb[(fine - halo) * SUBLANES:, :]
        pbuf[s, 0:POOL_HALO, :] = phalo[...]
        pbuf[s, POOL_HALO:, :] = p
        phalo[...] = p[sub - POOL_HALO:, :]

    def stage2(s):
        r0 = s * sub
        x = xbuf[r0:r0 + sub, :] if fused else x_ref[0, r0:r0 + sub, :]
        chunk = 8 * SUBLANES
        first = halo - (CONV_KERNEL - 1)
        for q0 in range(0, sub, chunk):
            for c0 in range(0, cw, LANES):
                acc = jnp.broadcast_to(cb_ref[:, c0:c0 + LANES], (chunk, LANES))
                for k in range(CONV_KERNEL):
                    off = q0 + (first + k) * SUBLANES
                    acc = acc + cw_ref[k:k + 1, c0:c0 + LANES] * ebuf[s, off:off + chunk, c0:c0 + LANES]
                cbuf[s, q0:q0 + chunk, c0:c0 + LANES] = acc
        yc = _layer_norm_rows(cbuf[s], cg_ref[...], cbeta_ref[...])
        ya_b = (yc * jax.nn.sigmoid(yc)).astype(BF16)
        ya = _dot(permt_ref[...], ya_b).astype(BF16)

        ext = pbuf[s]
        p = ext[POOL_HALO:, :]
        w_sum = ext + pltpu.roll(ext, 1, 0)
        parts = [w_sum[:, :gw]]
        shift = 2
        for _ in range(len(POOL_WINDOWS) - 1):
            w_sum = w_sum[:, gw:]
            w_sum = w_sum + pltpu.roll(w_sum, shift, 0)
            parts.append(w_sum[:, :gw])
            shift *= 2
        wsum = jnp.concatenate(parts, axis=1)[POOL_HALO:, :]
        pos = (lax.broadcasted_iota(I32, (POOL_HALO, gw), 0) + (i * ts + r0 + 1)).astype(F32)
        cnt = jnp.concatenate([jnp.minimum(pos, float(wlen)) for wlen in POOL_WINDOWS], axis=1)
        inv = jnp.concatenate([jnp.full((1, gw), 1.0 / wlen, F32) for wlen in POOL_WINDOWS], axis=1)
        mean = jnp.concatenate([wsum[:POOL_HALO] / cnt, wsum[POOL_HALO:] * inv], axis=0)
        dpool = (mean - p).astype(BF16)
        yb_parts = [_dot(dpool[:, gi * gw:(gi + 1) * gw], pw_ref[gi]) for gi in range(len(POOL_WINDOWS))]
        yb = (jnp.concatenate(yb_parts, axis=1) * ps_ref[...]).astype(BF16)

        mix = _dot(jnp.concatenate([ya, yb], axis=1), wout_ref[...]) + bout_ref[...]
        x1 = _deepnorm_rows(alpha, x, 1.0 + g1, mix, l1g_ref[...], l1b_ref[...])
        x1_ref[0, r0:r0 + sub, :] = x1
        u2 = x1 * (1.0 + sc2) + sh2
        u2_ref[0, r0:r0 + sub, :] = _pack_halves(u2)

        logits = _dot(u2.astype(BF16), wr_ref[...]).T[0:n_experts, :]
        scores = jax.nn.sigmoid(logits)
        sel = scores + rb_ref[...]
        epg = n_experts // N_EXPERT_GROUPS
        io = lax.broadcasted_iota(I32, (epg, sub), 0).astype(F32)
        neg = jnp.float32(-jnp.inf)
        best = None
        for g in range(N_EXPERT_GROUPS):
            sg = sel[g * epg:(g + 1) * epg, :]
            scg = scores[g * epg:(g + 1) * epg, :]
            m1 = jnp.max(sg, axis=0, keepdims=True)
            i1 = jnp.min(jnp.where(sg == m1, io, float(epg)), axis=0, keepdims=True)
            rest = jnp.where(io == i1, neg, sg)
            m2 = jnp.max(rest, axis=0, keepdims=True)
            i2 = jnp.min(jnp.where(jnp.logical_and(rest == m2, io != i1), io, float(epg)), axis=0, keepdims=True)
            s1 = jnp.sum(jnp.where(io == i1, scg, 0.0), axis=0, keepdims=True)
            s2 = jnp.sum(jnp.where(io == i2, scg, 0.0), axis=0, keepdims=True)
            gs = m1 + m2
            cand = (gs, i1 + float(g * epg), i2 + float(g * epg), s1, s2)
            if best is None:
                best = cand
            else:
                upd = gs > best[0]
                best = tuple(jnp.where(upd, cn, bs) for cn, bs in zip(cand, best))
        _, e1, e2, s1, s2 = best
        denom = s1 + s2
        w1 = s1 / denom
        w2 = s2 / denom

        ioe = lax.broadcasted_iota(I32, (n_experts, sub), 0).astype(F32)
        hit1 = ioe == e1
        hit2 = ioe == e2
        onehot = jnp.where(jnp.logical_or(hit1, hit2), 1.0, 0.0)
        before = _dot(onehot.astype(BF16), tri_ref[...]) + carry[:, 0:1]
        r1 = jnp.sum(jnp.where(hit1, before, 0.0), axis=0, keepdims=True)
        r2 = jnp.sum(jnp.where(hit2, before, 0.0), axis=0, keepdims=True)
        carry[...] = carry[...] + jnp.sum(onehot, axis=1, keepdims=True)

        e_ref[0:1, r0:r0 + sub] = e1.astype(I32)
        e_ref[1:2, r0:r0 + sub] = e2.astype(I32)
        rank_ref[0:1, r0:r0 + sub] = r1.astype(I32)
        rank_ref[1:2, r0:r0 + sub] = r2.astype(I32)
        rowi = lax.broadcasted_iota(I32, (LANES, sub), 0)
        wpad = jnp.where(rowi == 0, w1, jnp.where(rowi == 1, w2, 0.0))
        wm_ref[r0:r0 + sub, :] = wpad.T

    nsub = ts // sub
    for s in range(min(STAGE_LEAD, nsub)):
        stage1(s)
    for s in range(nsub):
        if s + STAGE_LEAD < nsub:
            stage1(s + STAGE_LEAD)
        stage2(s)
    cnt_ref[...] = carry[...]


def _row_perm(n):
    r = jnp.arange(n)
    t = (n // SUBLANES) * (r % SUBLANES) + r // SUBLANES
    return (t[:, None] == jnp.arange(n)[None, :]).astype(BF16)


def _mixer(x, mod, lw, wr, rbias, alpha, bsz, x_b0, mod_b0, prev=None):
    _, seq, d = x.shape
    ts = min(SEQ_TILE, seq)
    sub = min(SEQ_SUB_TILE, ts)
    nsub = ts // sub
    ns = seq // ts
    n_tok = bsz * seq
    n_experts = rbias.shape[0]
    cw = lw["conv_w"].shape[1]
    pw = lw["pool_scale"].shape[0]
    perm = _row_perm(sub)
    tri = (jnp.arange(sub)[:, None] < jnp.arange(sub)[None, :]).astype(BF16)
    row = lambda a: a.reshape(1, -1)
    full = lambda shape: pl.BlockSpec(shape, lambda b, i: (0,) * len(shape))
    if prev is None:
        ins = [(x, pl.BlockSpec((1, ts, d), lambda b, i: (b + x_b0, i, 0)))]
    else:
        yy, wmp, modp, l2g, l2b = prev
        nblk = n_tok // ts
        ins = [
            (x, pl.BlockSpec((1, ts, d), lambda b, i: (b + x_b0, i, 0))),
            (yy, pl.BlockSpec((ts, d // 2), lambda b, i: (b * ns + i, 0))),
            (yy, pl.BlockSpec((ts, d // 2), lambda b, i: (nblk + b * ns + i, 0))),
            (wmp, pl.BlockSpec((ts, LANES), lambda b, i: (b * ns + i, 0))),
            (modp, pl.BlockSpec((1, 6, d), lambda b, i: (b + mod_b0, 0, 0))),
            (row(l2g), None), (row(l2b), None),
        ]
    ins += [
        (mod, pl.BlockSpec((1, 6, d), lambda b, i: (b + mod_b0, 0, 0))),
        (lw["w_in"], None), (row(lw["b_in"]), None),
        (jnp.pad(lw["conv_w"], ((0, 1), (0, 0))), None), (row(lw["conv_b"]), None),
        (row(lw["conv_ln_g"]), None), (row(lw["conv_ln_b"]), None),
        (lw["pool_w"], None), (row(lw["pool_scale"]), None),
        (lw["w_out"], None), (row(lw["b_out"]), None),
        (row(lw["ln1_g"]), None), (row(lw["ln1_b"]), None),
        (wr, None), (rbias.reshape(-1, 1), None),
        (perm, None), (perm.T, None), (tri, None),
    ]
    args = [a for a, _ in ins]
    specs = [s if s is not None else full(a.shape) for a, s in ins]
    out_shape = (
        jax.ShapeDtypeStruct((bsz, seq, d), F32),
        jax.ShapeDtypeStruct((bsz, seq, d // 2), U32),
        jax.ShapeDtypeStruct((2, n_tok), I32),
        jax.ShapeDtypeStruct((2, n_tok), I32),
        jax.ShapeDtypeStruct((n_tok, LANES), F32),
        jax.ShapeDtypeStruct((n_experts, LANES), F32),
    )
    out_specs = (
        pl.BlockSpec((1, ts, d), lambda b, i: (b, i, 0)),
        pl.BlockSpec((1, ts, d // 2), lambda b, i: (b, i, 0)),
        pl.BlockSpec((2, ts), lambda b, i: (0, b * ns + i)),
        pl.BlockSpec((2, ts), lambda b, i: (0, b * ns + i)),
        pl.BlockSpec((ts, LANES), lambda b, i: (b * ns + i, 0)),
        pl.BlockSpec((n_experts, LANES), lambda b, i: (0, 0)),
    )
    return pl.pallas_call(
        functools.partial(_mixer_body, alpha, n_experts, sub, prev is not None),
        grid=(bsz, ns),
        in_specs=specs,
        out_specs=out_specs,
        out_shape=out_shape,
        scratch_shapes=[
            pltpu.VMEM((nsub, CONV_HALO_VREGS * SUBLANES + sub, cw), F32),
            pltpu.VMEM((nsub, sub, cw), F32),
            pltpu.VMEM((nsub, POOL_HALO + sub, pw), F32),
            pltpu.VMEM((CONV_HALO_VREGS * SUBLANES, cw), F32),
            pltpu.VMEM((POOL_HALO, pw), F32),
            pltpu.VMEM((n_experts, LANES), F32),
            pltpu.VMEM((ts, d) if prev is not None else (SUBLANES, LANES), F32),
        ],
        compiler_params=pltpu.CompilerParams(dimension_semantics=("arbitrary", "arbitrary"), vmem_limit_bytes=VMEM_LIMIT),
        name="mixer_router",
    )(*args)


def _dest_body(n_experts, pstart_ref, e_ref, rank_ref, dest_ref):
    e = e_ref[...]
    dest = rank_ref[...]
    for ex in range(n_experts):
        dest = dest + jnp.where(e == ex, pstart_ref[ex], 0)
    dest_ref[...] = dest


def _dest_rows(pstart, eidx, rank):
    n_tok = eidx.shape[1]
    tn = min(8192, n_tok)
    grid_spec = pltpu.PrefetchScalarGridSpec(
        num_scalar_prefetch=1,
        grid=(n_tok // tn,),
        in_specs=[pl.BlockSpec((2, tn), lambda i, ps: (0, i)), pl.BlockSpec((2, tn), lambda i, ps: (0, i))],
        out_specs=pl.BlockSpec((2, tn), lambda i, ps: (0, i)),
    )
    return pl.pallas_call(
        functools.partial(_dest_body, pstart.shape[0]),
        grid_spec=grid_spec,
        out_shape=jax.ShapeDtypeStruct((2, n_tok), I32),
        compiler_params=pltpu.CompilerParams(dimension_semantics=("arbitrary",)),
        name="dest_rows",
    )(pstart, eidx, rank)


def _sc_mesh():
    return plsc.VectorSubcoreMesh(core_axis_name="c", subcore_axis_name="s",
                                  num_cores=SC_CORES, num_subcores=SC_SUBCORES)


def _dispatch(u2, d0, d1, n_rows):
    n_tok, d = u2.shape
    workers = SC_CORES * SC_SUBCORES
    per_w = n_tok // workers
    ch = SC_ROWS_PER_STEP
    nch = per_w // ch

    @functools.partial(
        pl.kernel, mesh=_sc_mesh(),
        out_type=jax.ShapeDtypeStruct((n_rows, d), u2.dtype),
        scratch_types=[pltpu.VMEM((nch, ch), I32), pltpu.VMEM((nch, ch), I32), pltpu.VMEM((2, ch, d), u2.dtype),
                       pltpu.SemaphoreType.DMA((2,)), pltpu.SemaphoreType.DMA((2,))],
        name="sc_dispatch",
    )
    def run(u_hbm, d0_hbm, d1_hbm, xs_hbm, i0, i1, rows, rsem, ssem):
        wid = lax.axis_index("s") * SC_CORES + lax.axis_index("c")
        base = wid * per_w
        pltpu.sync_copy(d0_hbm.at[pl.ds(wid * nch, nch)], i0)
        pltpu.sync_copy(d1_hbm.at[pl.ds(wid * nch, nch)], i1)

        def read(j, b):
            return pltpu.make_async_copy(u_hbm.at[pl.ds(base + j * ch, ch)], rows.at[b], rsem.at[b])

        def scatter(idx, j, b):
            return pltpu.make_async_copy(rows.at[b], xs_hbm.at[idx.at[j]], ssem.at[b])

        read(0, 0).start()

        @pl.loop(0, nch, step=2)
        def _(j0):
            for b in range(2):
                j = j0 + b
                read(j, b).wait()
                scatter(i0, j, b).start()
                scatter(i1, j, b).start()

                @pl.when(j >= 1)
                def _():
                    scatter(i0, j - 1, 1 - b).wait()
                    scatter(i1, j - 1, 1 - b).wait()

                @pl.when(j + 1 < nch)
                def _():
                    read(j + 1, 1 - b).start()

        scatter(i0, nch - 1, 1).wait()
        scatter(i1, nch - 1, 1).wait()

    return run(u2, d0.reshape(n_tok // ch, ch), d1.reshape(n_tok // ch, ch))


def _gather_rows(ys, dd):
    n_out = dd.shape[0]
    d = ys.shape[1]
    workers = SC_CORES * SC_SUBCORES
    per_w = n_out // workers
    ch = SC_ROWS_PER_STEP
    nch = per_w // ch

    @functools.partial(
        pl.kernel, mesh=_sc_mesh(),
        out_type=jax.ShapeDtypeStruct((n_out, d), ys.dtype),
        scratch_types=[pltpu.VMEM((nch, ch), I32), pltpu.VMEM((2, ch, d), ys.dtype),
                       pltpu.SemaphoreType.DMA((2,)), pltpu.SemaphoreType.DMA((2,))],
        name="sc_gather",
    )
    def run(ys_hbm, dd_hbm, out_hbm, idx, rows, gsem, wsem):
        wid = lax.axis_index("s") * SC_CORES + lax.axis_index("c")
        base = wid * per_w
        pltpu.sync_copy(dd_hbm.at[pl.ds(wid * nch, nch)], idx)

        def gather(j, b):
            return pltpu.make_async_copy(ys_hbm.at[idx.at[j]], rows.at[b], gsem.at[b])

        def write(j, b):
            return pltpu.make_async_copy(rows.at[b], out_hbm.at[pl.ds(base + j * ch, ch)], wsem.at[b])

        gather(0, 0).start()

        @pl.loop(0, nch, step=2)
        def _(j0):
            for b in range(2):
                j = j0 + b
                gather(j, b).wait()
                write(j, b).start()

                @pl.when(j >= 1)
                def _():
                    write(j - 1, 1 - b).wait()

                @pl.when(j + 1 < nch)
                def _():
                    gather(j + 1, 1 - b).start()

        write(nch - 1, 1).wait()

    return run(ys, dd.reshape(n_out // ch, ch))


def _expert_body(layer, blk_ref, slot_ref, next_ref, nused_ref, xs_hbm, wg_hbm, wu_hbm, wd_hbm, ys_ref,
                 wg32, wu32, wd32, wgb, wub, wdb, sem, xbuf, xsem):
    j = pl.program_id(0)
    used = j < nused_ref[0]
    e = blk_ref[j]
    slot = slot_ref[j]
    nxt = next_ref[j]
    new_expert = jnp.logical_or(j == 0, e != blk_ref[jnp.maximum(j - 1, 0)])

    def fetch(ex, sl):
        return (pltpu.make_async_copy(wg_hbm.at[layer, ex], wg32.at[sl], sem.at[sl, 0]),
                pltpu.make_async_copy(wu_hbm.at[layer, ex], wu32.at[sl], sem.at[sl, 1]),
                pltpu.make_async_copy(wd_hbm.at[layer, ex], wd32.at[sl], sem.at[sl, 2]))

    @pl.when(j == 0)
    def _():
        for cp in fetch(e, slot):
            cp.start()

    @pl.when(jnp.logical_and(used, new_expert))
    def _():
        for cp in fetch(e, slot):
            cp.wait()

        @pl.when(nxt >= 0)
        def _():
            for cp in fetch(nxt, 1 - slot):
                cp.start()

        wgb[...] = wg32[slot].astype(BF16)
        wub[...] = wu32[slot].astype(BF16)
        wdb[...] = wd32[slot].astype(BF16)

    tm = xbuf.shape[1]
    nused = nused_ref[0]

    def rows(t):
        return pltpu.make_async_copy(xs_hbm.at[pl.ds(t * tm, tm)], xbuf.at[lax.rem(t, 3)], xsem.at[lax.rem(t, 3)])

    @pl.when(j == 0)
    def _():
        rows(0).start()

        @pl.when(nused > 1)
        def _():
            rows(1).start()

    @pl.when(used)
    def _():
        rows(j).wait()

        @pl.when(j + 2 < nused)
        def _():
            rows(j + 2).start()

        lo, hi = _unpack_halves(xbuf[lax.rem(j, 3)])
        lo = lo.astype(BF16)
        hi = hi.astype(BF16)
        half = lo.shape[1]
        fw = wgb.shape[1] // EXPERT_F_SPLIT
        acc = None
        for c in range(EXPERT_F_SPLIT):
            cs = slice(c * fw, (c + 1) * fw)
            g = _dot(lo, wgb[0:half, cs]) + _dot(hi, wgb[half:, cs])
            up = _dot(lo, wub[0:half, cs]) + _dot(hi, wub[half:, cs])
            hid = (g * jax.nn.sigmoid(g) * up).astype(BF16)
            part = _dot(hid, wdb[cs, :])
            acc = part if acc is None else acc + part
        ys_ref[...] = _pack_halves(acc)

    @pl.when(j >= nused_ref[0])
    def _():
        ys_ref[...] = jnp.zeros(ys_ref.shape, U32)


def _experts(blk_e, blk_slot, blk_next, nused, xs, wg, wu, wd, layer):
    n_rows = xs.shape[0]
    d = wg.shape[2]
    f = wg.shape[3]
    tm = EXPERT_TILE
    grid_spec = pltpu.PrefetchScalarGridSpec(
        num_scalar_prefetch=4,
        grid=(n_rows // tm,),
        in_specs=[
            pl.BlockSpec(memory_space=pl.ANY),
            pl.BlockSpec(memory_space=pl.ANY),
            pl.BlockSpec(memory_space=pl.ANY),
            pl.BlockSpec(memory_space=pl.ANY),
        ],
        out_specs=pl.BlockSpec((tm, d // 2), lambda j, be, sl, nx, nu: (j, 0)),
        scratch_shapes=[pltpu.VMEM((2, d, f), F32), pltpu.VMEM((2, d, f), F32), pltpu.VMEM((2, f, d), F32),
                        pltpu.VMEM((d, f), BF16), pltpu.VMEM((d, f), BF16), pltpu.VMEM((f, d), BF16),
                        pltpu.SemaphoreType.DMA((2, 3)),
                        pltpu.VMEM((3, tm, d // 2), U32), pltpu.SemaphoreType.DMA((3,))],
    )
    return pl.pallas_call(
        functools.partial(_expert_body, layer),
        grid_spec=grid_spec,
        out_shape=jax.ShapeDtypeStruct((n_rows, d // 2), U32),
        compiler_params=pltpu.CompilerParams(dimension_semantics=("arbitrary",), vmem_limit_bytes=VMEM_LIMIT),
        name="expert_ffn",
    )(blk_e, blk_slot, blk_next, nused, xs, wg, wu, wd)


def _combine_body(alpha, x1_ref, y0_ref, y1_ref, mod_ref, wm_ref, g_ref, b_ref, *rest):
    o_ref = rest[-1]
    o_ref[...] = _combine_rows(alpha, x1_ref[...], y0_ref[...], y1_ref[...], wm_ref[...],
                               mod_ref[0][5:6], g_ref[...], b_ref[...])


def _combine(x1, yy, mod, wm, ln_g, ln_b, seq, alpha, mod_b0, out_rows, out_row0, out_prev):
    n_tok, d = x1.shape
    tn = min(COMBINE_TILE, seq)
    per_seq = seq // tn
    nblk = n_tok // tn
    blk0 = out_row0 // tn
    rows = lambda: pl.BlockSpec((tn, d), lambda i: (i, 0))
    in_specs = [rows(), pl.BlockSpec((tn, d // 2), lambda i: (i, 0)),
                pl.BlockSpec((tn, d // 2), lambda i: (nblk + i, 0)),
                pl.BlockSpec((1, 6, d), lambda i: (i // per_seq + mod_b0, 0, 0)),
                pl.BlockSpec((tn, LANES), lambda i: (i, 0)),
                pl.BlockSpec((1, d), lambda i: (0, 0)),
                pl.BlockSpec((1, d), lambda i: (0, 0))]
    args = [x1, yy, yy, mod, wm, ln_g.reshape(1, d), ln_b.reshape(1, d)]
    aliases = {}
    if out_prev is not None:
        in_specs.append(pl.BlockSpec(memory_space=pl.ANY))
        args.append(out_prev)
        aliases = {len(args) - 1: 0}
    return pl.pallas_call(
        functools.partial(_combine_body, alpha),
        grid=(nblk,),
        in_specs=in_specs,
        out_specs=pl.BlockSpec((tn, d), lambda i: (blk0 + i, 0)),
        out_shape=jax.ShapeDtypeStruct((out_rows, d), F32),
        input_output_aliases=aliases,
        compiler_params=pltpu.CompilerParams(dimension_semantics=("arbitrary",), vmem_limit_bytes=VMEM_LIMIT),
        name="combine_ln",
    )(*args)


def kernel(x, c, w_ada, b_ada, w_in, b_in, conv_w, conv_b, conv_ln_g, conv_ln_b, pool_w, pool_scale, w_out, b_out, ln1_g, ln1_b, w_router, router_bias, w_gate, w_up, w_down, ln2_g, ln2_b):
    bsz, seq, d = x.shape
    depth = w_ada.shape[0]
    n_experts = w_router.shape[1]
    alpha = float((2 * depth) ** 0.25)
    tm = EXPERT_TILE
    n_chunks = BATCH_CHUNKS if bsz % BATCH_CHUNKS == 0 else 1
    bc = bsz // n_chunks
    n_tok = bc * seq
    n_rows = 2 * n_tok + n_experts * tm

    mod_all = _ada_mod(c, w_ada, b_ada).reshape(depth, bsz, 6, d)
    wr = jnp.pad(w_router.astype(BF16), ((0, 0), (0, LANES - n_experts)))

    chunks = [x] * n_chunks
    starts = [ci * bc for ci in range(n_chunks)]
    prevs = [None] * n_chunks
    out = None
    for l in range(depth):
        lw = dict(w_in=w_in[l].astype(BF16), b_in=b_in[l], conv_w=conv_w[l], conv_b=conv_b[l],
                  conv_ln_g=conv_ln_g[l], conv_ln_b=conv_ln_b[l], pool_w=pool_w[l].astype(BF16),
                  pool_scale=pool_scale[l], w_out=w_out[l].astype(BF16), b_out=b_out[l],
                  ln1_g=ln1_g[l], ln1_b=ln1_b[l])
        mod = mod_all[l]
        last = l == depth - 1
        st = []
        for ci in range(n_chunks):
            x1, u2, eidx, rank, wm, cnt = _mixer(chunks[ci], mod, lw, wr, router_bias, alpha,
                                                 bc, starts[ci], ci * bc, prevs[ci])
            counts = cnt[:, 0].astype(I32)
            tiles = (counts + tm - 1) // tm
            tile_end = jnp.cumsum(tiles)
            pstart = ((tile_end - tiles) * tm).astype(I32)
            nused = tile_end[-1:].astype(I32)
            blk_e = jnp.minimum(
                jnp.sum(tile_end[None, :] <= jnp.arange(n_rows // tm, dtype=I32)[:, None], axis=1), n_experts - 1
            ).astype(I32)
            in_use = tiles > 0
            slot_e = (jnp.cumsum(in_use.astype(I32)) - 1) % 2
            ids = jnp.where(in_use, jnp.arange(n_experts, dtype=I32), n_experts)
            after = jnp.concatenate([lax.cummin(ids, reverse=True)[1:], jnp.full((1,), n_experts, I32)])
            next_e = jnp.where(after < n_experts, after, -1).astype(I32)
            dest = _dest_rows(pstart, eidx, rank)
            xs = _dispatch(u2.reshape(n_tok, d // 2), dest[0], dest[1], n_rows)
            st.append((x1, wm, (blk_e, slot_e[blk_e].astype(I32), next_e[blk_e]), nused, dest, xs))
        ys = [_experts(*plan, nused, xs, w_gate, w_up, w_down, l) for (_, _, plan, nused, _, xs) in st]
        yy = [_gather_rows(ys[ci], st[ci][4].reshape(2 * n_tok)) for ci in range(n_chunks)]
        for ci in range(n_chunks):
            x1, wm = st[ci][0], st[ci][1]
            if last:
                out = _combine(x1.reshape(n_tok, d), yy[ci], mod, wm, ln2_g[l], ln2_b[l], seq, alpha,
                               ci * bc, bsz * seq, ci * n_tok, out)
            else:
                chunks[ci] = x1
                starts[ci] = 0
                prevs[ci] = (yy[ci], wm, mod, ln2_g[l], ln2_b[l])
    return out.reshape(bsz, seq, d)
```

```python
import functools

import jax
import jax.numpy as jnp
from jax import lax
from jax.experimental import pallas as pl
from jax.experimental.pallas import tpu as pltpu
from jax.experimental.pallas import tpu_sc as plsc

F32 = jnp.float32
BF16 = jnp.bfloat16
I32 = jnp.int32
U32 = jnp.uint32

CONV_KERNEL = 31
POOL_WINDOWS = (2, 4, 8, 16)
N_EXPERT_GROUPS = 4
LN_EPS = 1e-5

SUBLANES = 8
LANES = 128

SEQ_TILE = 1024
SEQ_SUB_TILE = 256
STAGE_LEAD = 4
EXPERT_TILE = 512
EXPERT_F_SPLIT = 2
COMBINE_TILE = 1024
BATCH_CHUNKS = 2
CONV_HALO_VREGS = 32
POOL_HALO = 16
VMEM_LIMIT = 56 * 1024 * 1024
SC_CORES = 2
SC_SUBCORES = 16
SC_ROWS_PER_STEP = 64


def _split_bf16(a):
    hi = a.astype(BF16)
    lo = (a - hi.astype(F32)).astype(BF16)
    return hi, lo


def _dot(a, b):
    return jnp.dot(a, b, preferred_element_type=F32)


def _pack_halves(y):
    h = y.shape[1] // 2
    lo = lax.bitcast_convert_type(y[:, :h].astype(BF16).astype(F32), U32)
    hi = lax.bitcast_convert_type(y[:, h:].astype(BF16).astype(F32), U32)
    return (lo >> 16) | hi


def _unpack_halves(p):
    lo = lax.bitcast_convert_type(p << 16, F32)
    hi = lax.bitcast_convert_type(p & jnp.uint32(0xFFFF0000), F32)
    return lo, hi


def _ada_body(c_ref, w_ref, b_ref, o_ref):
    c = c_ref[...]
    ca = c * jax.nn.sigmoid(c)
    chi, clo = _split_bf16(ca)
    whi, wlo = _split_bf16(w_ref[0])
    o_ref[0] = _dot(chi, whi) + _dot(chi, wlo) + _dot(clo, whi) + b_ref[0]


def _ada_mod(c, w_ada, b_ada):
    n_layers, d, n6 = w_ada.shape
    bsz = c.shape[0]
    tn = n6 // 6
    return pl.pallas_call(
        _ada_body,
        grid=(n_layers, n6 // tn),
        in_specs=[
            pl.BlockSpec((bsz, d), lambda l, j: (0, 0)),
            pl.BlockSpec((1, d, tn), lambda l, j: (l, 0, j)),
            pl.BlockSpec((1, 1, tn), lambda l, j: (l, 0, j)),
        ],
        out_specs=pl.BlockSpec((1, bsz, tn), lambda l, j: (l, 0, j)),
        out_shape=jax.ShapeDtypeStruct((n_layers, bsz, n6), F32),
        compiler_params=pltpu.CompilerParams(dimension_semantics=("arbitrary", "arbitrary"), vmem_limit_bytes=VMEM_LIMIT),
        name="ada_mod",
    )(c, w_ada, b_ada.reshape(n_layers, 1, n6))


def _layer_norm_rows(z, g, b, eps=LN_EPS):
    mu = jnp.mean(z, axis=-1, keepdims=True)
    zc = z - mu
    var = jnp.mean(zc * zc, axis=-1, keepdims=True)
    return zc * lax.rsqrt(var + eps) * g + b


def _deepnorm_rows(alpha, x, gate, branch, g, b):
    return _layer_norm_rows(x + (gate * (1.0 / alpha)) * branch, g, b, LN_EPS / (alpha * alpha))


def _combine_rows(alpha, x1, y0, y1, wm, g2, ln_g, ln_b):
    lo0, hi0 = _unpack_halves(y0)
    lo1, hi1 = _unpack_halves(y1)
    w0, w1 = wm[:, 0:1], wm[:, 1:2]
    f = jnp.concatenate([w0 * lo0 + w1 * lo1, w0 * hi0 + w1 * hi1], axis=1)
    return _deepnorm_rows(alpha, x1, 1.0 + g2, f, ln_g, ln_b)


def _mixer_body(alpha, n_experts, sub, fused, *refs):
    if fused:
        xp_ref, y0_ref, y1_ref, wmp_ref, modp_ref, l2g_ref, l2b_ref = refs[:7]
        refs = refs[7:]
        x_ref = None
    else:
        x_ref = refs[0]
        refs = refs[1:]
    (mod_ref, win_ref, bin_ref, cw_ref, cb_ref, cg_ref, cbeta_ref, pw_ref, ps_ref,
     wout_ref, bout_ref, l1g_ref, l1b_ref, wr_ref, rb_ref, perm_ref, permt_ref, tri_ref,
     x1_ref, u2_ref, e_ref, rank_ref, wm_ref, cnt_ref,
     ebuf, cbuf, pbuf, vprev, phalo, carry, xbuf) = refs
    ts = x1_ref.shape[1]
    cw = cg_ref.shape[1]
    pw = ps_ref.shape[1]
    gw = pw // len(POOL_WINDOWS)
    fine = sub // SUBLANES
    halo = CONV_HALO_VREGS
    b = pl.program_id(0)
    i = pl.program_id(1)

    @pl.when(i == 0)
    def _():
        vprev[...] = jnp.zeros(vprev.shape, F32)
        phalo[...] = jnp.zeros(phalo.shape, F32)

    @pl.when(jnp.logical_and(b == 0, i == 0))
    def _():
        carry[...] = jnp.zeros(carry.shape, F32)

    mod = mod_ref[0]
    sh1, sc1, g1 = mod[0:1], mod[1:2], mod[2:3]
    sh2, sc2 = mod[3:4], mod[4:5]

    def stage1(s):
        r0 = s * sub
        if fused:
            x = _combine_rows(alpha, xp_ref[0, r0:r0 + sub, :], y0_ref[r0:r0 + sub, :], y1_ref[r0:r0 + sub, :],
                              wmp_ref[r0:r0 + sub, :], modp_ref[0][5:6], l2g_ref[...], l2b_ref[...])
            xbuf[r0:r0 + sub, :] = x
        else:
            x = x_ref[0, r0:r0 + sub, :]
        u = (x * (1.0 + sc1) + sh1).astype(BF16)
        h = _dot(u, win_ref[...]) + bin_ref[...]
        a = h[:, :cw]
        gate = h[:, cw:2 * cw]
        p = h[:, 2 * cw:]

        v = (a * jax.nn.sigmoid(gate)).astype(BF16)
        vb = _dot(perm_ref[...], v)
        ebuf[s, halo * SUBLANES:, :] = vb
        sl = lax.broadcasted_iota(I32, (SUBLANES, cw), 0)
        for mm in range(halo):
            src = (fine - halo + mm) * SUBLANES
            cur = vb[src:src + SUBLANES, :]
            prev = vprev[mm * SUBLANES:(mm + 1) * SUBLANES, :]
            ebuf[s, mm * SUBLANES:(mm + 1) * SUBLANES, :] = pltpu.roll(
                jnp.where(sl == SUBLANES - 1, prev, cur), 1, 0)
        vprev[...] = vb[(fine - halo) * SUBLANES:, :]
        pbuf[s, 0:POOL_HALO, :] = phalo[...]
        pbuf[s, POOL_HALO:, :] = p
        phalo[...] = p[sub - POOL_HALO:, :]

    def stage2(s):
        r0 = s * sub
        x = xbuf[r0:r0 + sub, :] if fused else x_ref[0, r0:r0 + sub, :]
        chunk = 8 * SUBLANES
        first = halo - (CONV_KERNEL - 1)
        for q0 in range(0, sub, chunk):
            for c0 in range(0, cw, LANES):
                acc = jnp.broadcast_to(cb_ref[:, c0:c0 + LANES], (chunk, LANES))
                for k in range(CONV_KERNEL):
                    off = q0 + (first + k) * SUBLANES
                    acc = acc + cw_ref[k:k + 1, c0:c0 + LANES] * ebuf[s, off:off + chunk, c0:c0 + LANES]
                cbuf[s, q0:q0 + chunk, c0:c0 + LANES] = acc
        yc = _layer_norm_rows(cbuf[s], cg_ref[...], cbeta_ref[...])
        ya_b = (yc * jax.nn.sigmoid(yc)).astype(BF16)
        ya = _dot(permt_ref[...], ya_b).astype(BF16)

        ext = pbuf[s]
        p = ext[POOL_HALO:, :]
        w_sum = ext + pltpu.roll(ext, 1, 0)
        parts = [w_sum[:, :gw]]
        shift = 2
        for _ in range(len(POOL_WINDOWS) - 1):
            w_sum = w_sum[:, gw:]
            w_sum = w_sum + pltpu.roll(w_sum, shift, 0)
            parts.append(w_sum[:, :gw])
            shift *= 2
        wsum = jnp.concatenate(parts, axis=1)[POOL_HALO:, :]
        pos = (lax.broadcasted_iota(I32, (POOL_HALO, gw), 0) + (i * ts + r0 + 1)).astype(F32)
        cnt = jnp.concatenate([jnp.minimum(pos, float(wlen)) for wlen in POOL_WINDOWS], axis=1)
        inv = jnp.concatenate([jnp.full((1, gw), 1.0 / wlen, F32) for wlen in POOL_WINDOWS], axis=1)
        mean = jnp.concatenate([wsum[:POOL_HALO] / cnt, wsum[POOL_HALO:] * inv], axis=0)
        dpool = (mean - p).astype(BF16)
        yb_parts = [_dot(dpool[:, gi * gw:(gi + 1) * gw], pw_ref[gi]) for gi in range(len(POOL_WINDOWS))]
        yb = (jnp.concatenate(yb_parts, axis=1) * ps_ref[...]).astype(BF16)

        mix = _dot(jnp.concatenate([ya, yb], axis=1), wout_ref[...]) + bout_ref[...]
        x1 = _deepnorm_rows(alpha, x, 1.0 + g1, mix, l1g_ref[...], l1b_ref[...])
        x1_ref[0, r0:r0 + sub, :] = x1
        u2 = x1 * (1.0 + sc2) + sh2
        u2_ref[0, r0:r0 + sub, :] = _pack_halves(u2)

        logits = _dot(u2.astype(BF16), wr_ref[...]).T[0:n_experts, :]
        scores = jax.nn.sigmoid(logits)
        sel = scores + rb_ref[...]
        epg = n_experts // N_EXPERT_GROUPS
        io = lax.broadcasted_iota(I32, (epg, sub), 0).astype(F32)
        neg = jnp.float32(-jnp.inf)
        best = None
        for g in range(N_EXPERT_GROUPS):
            sg = sel[g * epg:(g + 1) * epg, :]
            scg = scores[g * epg:(g + 1) * epg, :]
            m1 = jnp.max(sg, axis=0, keepdims=True)
            i1 = jnp.min(jnp.where(sg == m1, io, float(epg)), axis=0, keepdims=True)
            rest = jnp.where(io == i1, neg, sg)
            m2 = jnp.max(rest, axis=0, keepdims=True)
            i2 = jnp.min(jnp.where(jnp.logical_and(rest == m2, io != i1), io, float(epg)), axis=0, keepdims=True)
            s1 = jnp.sum(jnp.where(io == i1, scg, 0.0), axis=0, keepdims=True)
            s2 = jnp.sum(jnp.where(io == i2, scg, 0.0), axis=0, keepdims=True)
            gs = m1 + m2
            cand = (gs, i1 + float(g * epg), i2 + float(g * epg), s1, s2)
            if best is None:
                best = cand
            else:
                upd = gs > best[0]
                best = tuple(jnp.where(upd, cn, bs) for cn, bs in zip(cand, best))
        _, e1, e2, s1, s2 = best
        denom = s1 + s2
        w1 = s1 / denom
        w2 = s2 / denom

        ioe = lax.broadcasted_iota(I32, (n_experts, sub), 0).astype(F32)
        hit1 = ioe == e1
        hit2 = ioe == e2
        onehot = jnp.where(jnp.logical_or(hit1, hit2), 1.0, 0.0)
        before = _dot(onehot.astype(BF16), tri_ref[...]) + carry[:, 0:1]
        r1 = jnp.sum(jnp.where(hit1, before, 0.0), axis=0, keepdims=True)
        r2 = jnp.sum(jnp.where(hit2, before, 0.0), axis=0, keepdims=True)
        carry[...] = carry[...] + jnp.sum(onehot, axis=1, keepdims=True)

        e_ref[0:1, r0:r0 + sub] = e1.astype(I32)
        e_ref[1:2, r0:r0 + sub] = e2.astype(I32)
        rank_ref[0:1, r0:r0 + sub] = r1.astype(I32)
        rank_ref[1:2, r0:r0 + sub] = r2.astype(I32)
        rowi = lax.broadcasted_iota(I32, (LANES, sub), 0)
        wpad = jnp.where(rowi == 0, w1, jnp.where(rowi == 1, w2, 0.0))
        wm_ref[r0:r0 + sub, :] = wpad.T

    nsub = ts // sub
    for s in range(min(STAGE_LEAD, nsub)):
        stage1(s)
    for s in range(nsub):
        if s + STAGE_LEAD < nsub:
            stage1(s + STAGE_LEAD)
        stage2(s)
    cnt_ref[...] = carry[...]


def _row_perm(n):
    r = jnp.arange(n)
    t = (n // SUBLANES) * (r % SUBLANES) + r // SUBLANES
    return (t[:, None] == jnp.arange(n)[None, :]).astype(BF16)


def _mixer(x, mod, lw, wr, rbias, alpha, bsz, x_b0, mod_b0, prev=None):
    _, seq, d = x.shape
    ts = min(SEQ_TILE, seq)
    sub = min(SEQ_SUB_TILE, ts)
    nsub = ts // sub
    ns = seq // ts
    n_tok = bsz * seq
    n_experts = rbias.shape[0]
    cw = lw["conv_w"].shape[1]
    pw = lw["pool_scale"].shape[0]
    perm = _row_perm(sub)
    tri = (jnp.arange(sub)[:, None] < jnp.arange(sub)[None, :]).astype(BF16)
    row = lambda a: a.reshape(1, -1)
    full = lambda shape: pl.BlockSpec(shape, lambda b, i: (0,) * len(shape))
    if prev is None:
        ins = [(x, pl.BlockSpec((1, ts, d), lambda b, i: (b + x_b0, i, 0)))]
    else:
        yy, wmp, modp, l2g, l2b = prev
        nblk = n_tok // ts
        ins = [
            (x, pl.BlockSpec((1, ts, d), lambda b, i: (b + x_b0, i, 0))),
            (yy, pl.BlockSpec((ts, d // 2), lambda b, i: (b * ns + i, 0))),
            (yy, pl.BlockSpec((ts, d // 2), lambda b, i: (nblk + b * ns + i, 0))),
            (wmp, pl.BlockSpec((ts, LANES), lambda b, i: (b * ns + i, 0))),
            (modp, pl.BlockSpec((1, 6, d), lambda b, i: (b + mod_b0, 0, 0))),
            (row(l2g), None), (row(l2b), None),
        ]
    ins += [
        (mod, pl.BlockSpec((1, 6, d), lambda b, i: (b + mod_b0, 0, 0))),
        (lw["w_in"], None), (row(lw["b_in"]), None),
        (jnp.pad(lw["conv_w"], ((0, 1), (0, 0))), None), (row(lw["conv_b"]), None),
        (row(lw["conv_ln_g"]), None), (row(lw["conv_ln_b"]), None),
        (lw["pool_w"], None), (row(lw["pool_scale"]), None),
        (lw["w_out"], None), (row(lw["b_out"]), None),
        (row(lw["ln1_g"]), None), (row(lw["ln1_b"]), None),
        (wr, None), (rbias.reshape(-1, 1), None),
        (perm, None), (perm.T, None), (tri, None),
    ]
    args = [a for a, _ in ins]
    specs = [s if s is not None else full(a.shape) for a, s in ins]
    out_shape = (
        jax.ShapeDtypeStruct((bsz, seq, d), F32),
        jax.ShapeDtypeStruct((bsz, seq, d // 2), U32),
        jax.ShapeDtypeStruct((2, n_tok), I32),
        jax.ShapeDtypeStruct((2, n_tok), I32),
        jax.ShapeDtypeStruct((n_tok, LANES), F32),
        jax.ShapeDtypeStruct((n_experts, LANES), F32),
    )
    out_specs = (
        pl.BlockSpec((1, ts, d), lambda b, i: (b, i, 0)),
        pl.BlockSpec((1, ts, d // 2), lambda b, i: (b, i, 0)),
        pl.BlockSpec((2, ts), lambda b, i: (0, b * ns + i)),
        pl.BlockSpec((2, ts), lambda b, i: (0, b * ns + i)),
        pl.BlockSpec((ts, LANES), lambda b, i: (b * ns + i, 0)),
        pl.BlockSpec((n_experts, LANES), lambda b, i: (0, 0)),
    )
    return pl.pallas_call(
        functools.partial(_mixer_body, alpha, n_experts, sub, prev is not None),
        grid=(bsz, ns),
        in_specs=specs,
        out_specs=out_specs,
        out_shape=out_shape,
        scratch_shapes=[
            pltpu.VMEM((nsub, CONV_HALO_VREGS * SUBLANES + sub, cw), F32),
            pltpu.VMEM((nsub, sub, cw), F32),
            pltpu.VMEM((nsub, POOL_HALO + sub, pw), F32),
            pltpu.VMEM((CONV_HALO_VREGS * SUBLANES, cw), F32),
            pltpu.VMEM((POOL_HALO, pw), F32),
            pltpu.VMEM((n_experts, LANES), F32),
            pltpu.VMEM((ts, d) if prev is not None else (SUBLANES, LANES), F32),
        ],
        compiler_params=pltpu.CompilerParams(dimension_semantics=("arbitrary", "arbitrary"), vmem_limit_bytes=VMEM_LIMIT),
        name="mixer_router",
    )(*args)


def _dest_body(n_experts, pstart_ref, e_ref, rank_ref, dest_ref):
    e = e_ref[...]
    dest = rank_ref[...]
    for ex in range(n_experts):
        dest = dest + jnp.where(e == ex, pstart_ref[ex], 0)
    dest_ref[...] = dest


def _dest_rows(pstart, eidx, rank):
    n_tok = eidx.shape[1]
    tn = min(8192, n_tok)
    grid_spec = pltpu.PrefetchScalarGridSpec(
        num_scalar_prefetch=1,
        grid=(n_tok // tn,),
        in_specs=[pl.BlockSpec((2, tn), lambda i, ps: (0, i)), pl.BlockSpec((2, tn), lambda i, ps: (0, i))],
        out_specs=pl.BlockSpec((2, tn), lambda i, ps: (0, i)),
    )
    return pl.pallas_call(
        functools.partial(_dest_body, pstart.shape[0]),
        grid_spec=grid_spec,
        out_shape=jax.ShapeDtypeStruct((2, n_tok), I32),
        compiler_params=pltpu.CompilerParams(dimension_semantics=("arbitrary",)),
        name="dest_rows",
    )(pstart, eidx, rank)


def _sc_mesh():
    return plsc.VectorSubcoreMesh(core_axis_name="c", subcore_axis_name="s",
                                  num_cores=SC_CORES, num_subcores=SC_SUBCORES)


def _dispatch(u2, d0, d1, n_rows):
    n_tok, d = u2.shape
    workers = SC_CORES * SC_SUBCORES
    per_w = n_tok // workers
    ch = SC_ROWS_PER_STEP
    nch = per_w // ch

    @functools.partial(
        pl.kernel, mesh=_sc_mesh(),
        out_type=jax.ShapeDtypeStruct((n_rows, d), u2.dtype),
        scratch_types=[pltpu.VMEM((nch, ch), I32), pltpu.VMEM((nch, ch), I32), pltpu.VMEM((2, ch, d), u2.dtype),
                       pltpu.SemaphoreType.DMA((2,)), pltpu.SemaphoreType.DMA((2,))],
        name="sc_dispatch",
    )
    def run(u_hbm, d0_hbm, d1_hbm, xs_hbm, i0, i1, rows, rsem, ssem):
        wid = lax.axis_index("s") * SC_CORES + lax.axis_index("c")
        base = wid * per_w
        pltpu.sync_copy(d0_hbm.at[pl.ds(wid * nch, nch)], i0)
        pltpu.sync_copy(d1_hbm.at[pl.ds(wid * nch, nch)], i1)

        def read(j, b):
            return pltpu.make_async_copy(u_hbm.at[pl.ds(base + j * ch, ch)], rows.at[b], rsem.at[b])

        def scatter(idx, j, b):
            return pltpu.make_async_copy(rows.at[b], xs_hbm.at[idx.at[j]], ssem.at[b])

        read(0, 0).start()

        @pl.loop(0, nch, step=2)
        def _(j0):
            for b in range(2):
                j = j0 + b
                read(j, b).wait()
                scatter(i0, j, b).start()
                scatter(i1, j, b).start()

                @pl.when(j >= 1)
                def _():
                    scatter(i0, j - 1, 1 - b).wait()
                    scatter(i1, j - 1, 1 - b).wait()

                @pl.when(j + 1 < nch)
                def _():
                    read(j + 1, 1 - b).start()

        scatter(i0, nch - 1, 1).wait()
        scatter(i1, nch - 1, 1).wait()

    return run(u2, d0.reshape(n_tok // ch, ch), d1.reshape(n_tok // ch, ch))


def _gather_rows(ys, dd):
    n_out = dd.shape[0]
    d = ys.shape[1]
    workers = SC_CORES * SC_SUBCORES
    per_w = n_out // workers
    ch = SC_ROWS_PER_STEP
    nch = per_w // ch

    @functools.partial(
        pl.kernel, mesh=_sc_mesh(),
        out_type=jax.ShapeDtypeStruct((n_out, d), ys.dtype),
        scratch_types=[pltpu.VMEM((nch, ch), I32), pltpu.VMEM((2, ch, d), ys.dtype),
                       pltpu.SemaphoreType.DMA((2,)), pltpu.SemaphoreType.DMA((2,))],
        name="sc_gather",
    )
    def run(ys_hbm, dd_hbm, out_hbm, idx, rows, gsem, wsem):
        wid = lax.axis_index("s") * SC_CORES + lax.axis_index("c")
        base = wid * per_w
        pltpu.sync_copy(dd_hbm.at[pl.ds(wid * nch, nch)], idx)

        def gather(j, b):
            return pltpu.make_async_copy(ys_hbm.at[idx.at[j]], rows.at[b], gsem.at[b])

        def write(j, b):
            return pltpu.make_async_copy(rows.at[b], out_hbm.at[pl.ds(base + j * ch, ch)], wsem.at[b])

        gather(0, 0).start()

        @pl.loop(0, nch, step=2)
        def _(j0):
            for b in range(2):
                j = j0 + b
                gather(j, b).wait()
                write(j, b).start()

                @pl.when(j >= 1)
                def _():
                    write(j - 1, 1 - b).wait()

                @pl.when(j + 1 < nch)
                def _():
                    gather(j + 1, 1 - b).start()

        write(nch - 1, 1).wait()

    return run(ys, dd.reshape(n_out // ch, ch))


def _expert_body(layer, blk_ref, slot_ref, next_ref, nused_ref, xs_ref, wg_hbm, wu_hbm, wd_hbm, ys_ref,
                 wg32, wu32, wd32, wgb, wub, wdb, sem):
    j = pl.program_id(0)
    used = j < nused_ref[0]
    e = blk_ref[j]
    slot = slot_ref[j]
    nxt = next_ref[j]
    new_expert = jnp.logical_or(j == 0, e != blk_ref[jnp.maximum(j - 1, 0)])

    def fetch(ex, sl):
        return (pltpu.make_async_copy(wg_hbm.at[layer, ex], wg32.at[sl], sem.at[sl, 0]),
                pltpu.make_async_copy(wu_hbm.at[layer, ex], wu32.at[sl], sem.at[sl, 1]),
                pltpu.make_async_copy(wd_hbm.at[layer, ex], wd32.at[sl], sem.at[sl, 2]))

    @pl.when(j == 0)
    def _():
        for cp in fetch(e, slot):
            cp.start()

    @pl.when(jnp.logical_and(used, new_expert))
    def _():
        for cp in fetch(e, slot):
            cp.wait()

        @pl.when(nxt >= 0)
        def _():
            for cp in fetch(nxt, 1 - slot):
                cp.start()

        wgb[...] = wg32[slot].astype(BF16)
        wub[...] = wu32[slot].astype(BF16)
        wdb[...] = wd32[slot].astype(BF16)

    @pl.when(used)
    def _():
        lo, hi = _unpack_halves(xs_ref[...])
        lo = lo.astype(BF16)
        hi = hi.astype(BF16)
        half = lo.shape[1]
        fw = wgb.shape[1] // EXPERT_F_SPLIT
        acc = None
        for c in range(EXPERT_F_SPLIT):
            cs = slice(c * fw, (c + 1) * fw)
            g = _dot(lo, wgb[0:half, cs]) + _dot(hi, wgb[half:, cs])
            up = _dot(lo, wub[0:half, cs]) + _dot(hi, wub[half:, cs])
            hid = (g * jax.nn.sigmoid(g) * up).astype(BF16)
            part = _dot(hid, wdb[cs, :])
            acc = part if acc is None else acc + part
        ys_ref[...] = _pack_halves(acc)

    @pl.when(j >= nused_ref[0])
    def _():
        ys_ref[...] = jnp.zeros(ys_ref.shape, U32)


def _experts(blk_e, blk_slot, blk_next, nused, xs, wg, wu, wd, layer):
    n_rows = xs.shape[0]
    d = wg.shape[2]
    f = wg.shape[3]
    tm = EXPERT_TILE
    grid_spec = pltpu.PrefetchScalarGridSpec(
        num_scalar_prefetch=4,
        grid=(n_rows // tm,),
        in_specs=[
            pl.BlockSpec((tm, d // 2), lambda j, be, sl, nx, nu: (jnp.minimum(j, nu[0] - 1), 0)),
            pl.BlockSpec(memory_space=pl.ANY),
            pl.BlockSpec(memory_space=pl.ANY),
            pl.BlockSpec(memory_space=pl.ANY),
        ],
        out_specs=pl.BlockSpec((tm, d // 2), lambda j, be, sl, nx, nu: (j, 0)),
        scratch_shapes=[pltpu.VMEM((2, d, f), F32), pltpu.VMEM((2, d, f), F32), pltpu.VMEM((2, f, d), F32),
                        pltpu.VMEM((d, f), BF16), pltpu.VMEM((d, f), BF16), pltpu.VMEM((f, d), BF16),
                        pltpu.SemaphoreType.DMA((2, 3))],
    )
    return pl.pallas_call(
        functools.partial(_expert_body, layer),
        grid_spec=grid_spec,
        out_shape=jax.ShapeDtypeStruct((n_rows, d // 2), U32),
        compiler_params=pltpu.CompilerParams(dimension_semantics=("arbitrary",), vmem_limit_bytes=VMEM_LIMIT),
        name="expert_ffn",
    )(blk_e, blk_slot, blk_next, nused, xs, wg, wu, wd)


def _combine_body(alpha, x1_ref, y0_ref, y1_ref, mod_ref, wm_ref, g_ref, b_ref, *rest):
    o_ref = rest[-1]
    o_ref[...] = _combine_rows(alpha, x1_ref[...], y0_ref[...], y1_ref[...], wm_ref[...],
                               mod_ref[0][5:6], g_ref[...], b_ref[...])


def _combine(x1, yy, mod, wm, ln_g, ln_b, seq, alpha, mod_b0, out_rows, out_row0, out_prev):
    n_tok, d = x1.shape
    tn = min(COMBINE_TILE, seq)
    per_seq = seq // tn
    nblk = n_tok // tn
    blk0 = out_row0 // tn
    rows = lambda: pl.BlockSpec((tn, d), lambda i: (i, 0))
    in_specs = [rows(), pl.BlockSpec((tn, d // 2), lambda i: (i, 0)),
                pl.BlockSpec((tn, d // 2), lambda i: (nblk + i, 0)),
                pl.BlockSpec((1, 6, d), lambda i: (i // per_seq + mod_b0, 0, 0)),
                pl.BlockSpec((tn, LANES), lambda i: (i, 0)),
                pl.BlockSpec((1, d), lambda i: (0, 0)),
                pl.BlockSpec((1, d), lambda i: (0, 0))]
    args = [x1, yy, yy, mod, wm, ln_g.reshape(1, d), ln_b.reshape(1, d)]
    aliases = {}
    if out_prev is not None:
        in_specs.append(pl.BlockSpec(memory_space=pl.ANY))
        args.append(out_prev)
        aliases = {len(args) - 1: 0}
    return pl.pallas_call(
        functools.partial(_combine_body, alpha),
        grid=(nblk,),
        in_specs=in_specs,
        out_specs=pl.BlockSpec((tn, d), lambda i: (blk0 + i, 0)),
        out_shape=jax.ShapeDtypeStruct((out_rows, d), F32),
        input_output_aliases=aliases,
        compiler_params=pltpu.CompilerParams(dimension_semantics=("arbitrary",), vmem_limit_bytes=VMEM_LIMIT),
        name="combine_ln",
    )(*args)


def kernel(x, c, w_ada, b_ada, w_in, b_in, conv_w, conv_b, conv_ln_g, conv_ln_b, pool_w, pool_scale, w_out, b_out, ln1_g, ln1_b, w_router, router_bias, w_gate, w_up, w_down, ln2_g, ln2_b):
    bsz, seq, d = x.shape
    depth = w_ada.shape[0]
    n_experts = w_router.shape[1]
    alpha = float((2 * depth) ** 0.25)
    tm = EXPERT_TILE
    n_chunks = BATCH_CHUNKS if bsz % BATCH_CHUNKS == 0 else 1
    bc = bsz // n_chunks
    n_tok = bc * seq
    n_rows = 2 * n_tok + n_experts * tm

    mod_all = _ada_mod(c, w_ada, b_ada).reshape(depth, bsz, 6, d)
    wr = jnp.pad(w_router.astype(BF16), ((0, 0), (0, LANES - n_experts)))

    chunks = [x] * n_chunks
    starts = [ci * bc for ci in range(n_chunks)]
    prevs = [None] * n_chunks
    out = None
    for l in range(depth):
        lw = dict(w_in=w_in[l].astype(BF16), b_in=b_in[l], conv_w=conv_w[l], conv_b=conv_b[l],
                  conv_ln_g=conv_ln_g[l], conv_ln_b=conv_ln_b[l], pool_w=pool_w[l].astype(BF16),
                  pool_scale=pool_scale[l], w_out=w_out[l].astype(BF16), b_out=b_out[l],
                  ln1_g=ln1_g[l], ln1_b=ln1_b[l])
        mod = mod_all[l]
        last = l == depth - 1
        st = []
        for ci in range(n_chunks):
            x1, u2, eidx, rank, wm, cnt = _mixer(chunks[ci], mod, lw, wr, router_bias, alpha,
                                                 bc, starts[ci], ci * bc, prevs[ci])
            counts = cnt[:, 0].astype(I32)
            tiles = (counts + tm - 1) // tm
            tile_end = jnp.cumsum(tiles)
            pstart = ((tile_end - tiles) * tm).astype(I32)
            nused = tile_end[-1:].astype(I32)
            blk_e = jnp.minimum(
                jnp.sum(tile_end[None, :] <= jnp.arange(n_rows // tm, dtype=I32)[:, None], axis=1), n_experts - 1
            ).astype(I32)
            in_use = tiles > 0
            slot_e = (jnp.cumsum(in_use.astype(I32)) - 1) % 2
            ids = jnp.where(in_use, jnp.arange(n_experts, dtype=I32), n_experts)
            after = jnp.concatenate([lax.cummin(ids, reverse=True)[1:], jnp.full((1,), n_experts, I32)])
            next_e = jnp.where(after < n_experts, after, -1).astype(I32)
            dest = _dest_rows(pstart, eidx, rank)
            xs = _dispatch(u2.reshape(n_tok, d // 2), dest[0], dest[1], n_rows)
            st.append((x1, wm, (blk_e, slot_e[blk_e].astype(I32), next_e[blk_e]), nused, dest, xs))
        ys = [_experts(*plan, nused, xs, w_gate, w_up, w_down, l) for (_, _, plan, nused, _, xs) in st]
        yy = [_gather_rows(ys[ci], st[ci][4].reshape(2 * n_tok)) for ci in range(n_chunks)]
        for ci in range(n_chunks):
            x1, wm = st[ci][0], st[ci][1]
            if last:
                out = _combine(x1.reshape(n_tok, d), yy[ci], mod, wm, ln2_g[l], ln2_b[l], seq, alpha,
                               ci * bc, bsz * seq, ci * n_tok, out)
            else:
                chunks[ci] = x1
                starts[ci] = 0
                prevs[ci] = (yy[ci], wm, mod, ln2_g[l], ln2_b[l])
    return out.reshape(bsz, seq, d)
```

```python
import functools

import jax
import jax.numpy as jnp
from jax import lax
from jax.experimental import pallas as pl
from jax.experimental.pallas import tpu as pltpu
from jax.experimental.pallas import tpu_sc as plsc

F32 = jnp.float32
BF16 = jnp.bfloat16
I32 = jnp.int32
U32 = jnp.uint32

CONV_KERNEL = 31
POOL_WINDOWS = (2, 4, 8, 16)
N_EXPERT_GROUPS = 4
LN_EPS = 1e-5

SUBLANES = 8
LANES = 128

SEQ_TILE = 1024
SEQ_SUB_TILE = 256
STAGE_LEAD = 4
EXPERT_TILE = 512
EXPERT_F_SPLIT = 2
COMBINE_TILE = 1024
BATCH_CHUNKS = 2
CONV_HALO_VREGS = 32
POOL_HALO = 16
VMEM_LIMIT = 56 * 1024 * 1024
SC_CORES = 2
SC_SUBCORES = 16
SC_ROWS_PER_STEP = 64


def _split_bf16(a):
    hi = a.astype(BF16)
    lo = (a - hi.astype(F32)).astype(BF16)
    return hi, lo


def _dot(a, b):
    return jnp.dot(a, b, preferred_element_type=F32)


def _pack_halves(y):
    h = y.shape[1] // 2
    lo = lax.bitcast_convert_type(y[:, :h].astype(BF16).astype(F32), U32)
    hi = lax.bitcast_convert_type(y[:, h:].astype(BF16).astype(F32), U32)
    return (lo >> 16) | hi


def _unpack_halves(p):
    lo = lax.bitcast_convert_type(p << 16, F32)
    hi = lax.bitcast_convert_type(p & jnp.uint32(0xFFFF0000), F32)
    return lo, hi


def _ada_body(c_ref, w_ref, b_ref, o_ref):
    c = c_ref[...]
    ca = c * jax.nn.sigmoid(c)
    chi, clo = _split_bf16(ca)
    whi, wlo = _split_bf16(w_ref[0])
    o_ref[0] = _dot(chi, whi) + _dot(chi, wlo) + _dot(clo, whi) + b_ref[0]


def _ada_mod(c, w_ada, b_ada):
    n_layers, d, n6 = w_ada.shape
    bsz = c.shape[0]
    tn = n6 // 6
    return pl.pallas_call(
        _ada_body,
        grid=(n_layers, n6 // tn),
        in_specs=[
            pl.BlockSpec((bsz, d), lambda l, j: (0, 0)),
            pl.BlockSpec((1, d, tn), lambda l, j: (l, 0, j)),
            pl.BlockSpec((1, 1, tn), lambda l, j: (l, 0, j)),
        ],
        out_specs=pl.BlockSpec((1, bsz, tn), lambda l, j: (l, 0, j)),
        out_shape=jax.ShapeDtypeStruct((n_layers, bsz, n6), F32),
        compiler_params=pltpu.CompilerParams(dimension_semantics=("arbitrary", "arbitrary"), vmem_limit_bytes=VMEM_LIMIT),
        name="ada_mod",
    )(c, w_ada, b_ada.reshape(n_layers, 1, n6))


def _layer_norm_rows(z, g, b, eps=LN_EPS):
    mu = jnp.mean(z, axis=-1, keepdims=True)
    zc = z - mu
    var = jnp.mean(zc * zc, axis=-1, keepdims=True)
    return zc * lax.rsqrt(var + eps) * g + b


def _deepnorm_rows(alpha, x, gate, branch, g, b):
    return _layer_norm_rows(x + (gate * (1.0 / alpha)) * branch, g, b, LN_EPS / (alpha * alpha))


def _combine_rows(alpha, x1, y0, y1, wm, g2, ln_g, ln_b):
    lo0, hi0 = _unpack_halves(y0)
    lo1, hi1 = _unpack_halves(y1)
    w0, w1 = wm[:, 0:1], wm[:, 1:2]
    f = jnp.concatenate([w0 * lo0 + w1 * lo1, w0 * hi0 + w1 * hi1], axis=1)
    return _deepnorm_rows(alpha, x1, 1.0 + g2, f, ln_g, ln_b)


def _mixer_body(alpha, n_experts, sub, fused, *refs):
    if fused:
        xp_ref, y0_ref, y1_ref, wmp_ref, modp_ref, l2g_ref, l2b_ref = refs[:7]
        refs = refs[7:]
        x_ref = None
    else:
        x_ref = refs[0]
        refs = refs[1:]
    (mod_ref, win_ref, bin_ref, cw_ref, cb_ref, cg_ref, cbeta_ref, pw_ref, ps_ref,
     wout_ref, bout_ref, l1g_ref, l1b_ref, wr_ref, rb_ref, perm_ref, permt_ref, tri_ref,
     x1_ref, u2_ref, e_ref, rank_ref, wm_ref, cnt_ref,
     ebuf, cbuf, pbuf, vprev, phalo, carry, xbuf) = refs
    ts = x1_ref.shape[1]
    cw = cg_ref.shape[1]
    pw = ps_ref.shape[1]
    gw = pw // len(POOL_WINDOWS)
    fine = sub // SUBLANES
    halo = CONV_HALO_VREGS
    b = pl.program_id(0)
    i = pl.program_id(1)

    @pl.when(i == 0)
    def _():
        vprev[...] = jnp.zeros(vprev.shape, F32)
        phalo[...] = jnp.zeros(phalo.shape, F32)

    @pl.when(jnp.logical_and(b == 0, i == 0))
    def _():
        carry[...] = jnp.zeros(carry.shape, F32)

    mod = mod_ref[0]
    sh1, sc1, g1 = mod[0:1], mod[1:2], mod[2:3]
    sh2, sc2 = mod[3:4], mod[4:5]

    def stage1(s):
        r0 = s * sub
        if fused:
            x = _combine_rows(alpha, xp_ref[0, r0:r0 + sub, :], y0_ref[r0:r0 + sub, :], y1_ref[r0:r0 + sub, :],
                              wmp_ref[r0:r0 + sub, :], modp_ref[0][5:6], l2g_ref[...], l2b_ref[...])
            xbuf[r0:r0 + sub, :] = x
        else:
            x = x_ref[0, r0:r0 + sub, :]
        u = (x * (1.0 + sc1) + sh1).astype(BF16)
        h = _dot(u, win_ref[...]) + bin_ref[...]
        a = h[:, :cw]
        gate = h[:, cw:2 * cw]
        p = h[:, 2 * cw:]

        v = (a * jax.nn.sigmoid(gate)).astype(BF16)
        vb = _dot(perm_ref[...], v)
        ebuf[s, halo * SUBLANES:, :] = vb
        sl = lax.broadcasted_iota(I32, (SUBLANES, cw), 0)
        for mm in range(halo):
            src = (fine - halo + mm) * SUBLANES
            cur = vb[src:src + SUBLANES, :]
            prev = vprev[mm * SUBLANES:(mm + 1) * SUBLANES, :]
            ebuf[s, mm * SUBLANES:(mm + 1) * SUBLANES, :] = pltpu.roll(
                jnp.where(sl == SUBLANES - 1, prev, cur), 1, 0)
        vprev[...] = vb[(fine - halo) * SUBLANES:, :]
        pbuf[s, 0:POOL_HALO, :] = phalo[...]
        pbuf[s, POOL_HALO:, :] = p
        phalo[...] = p[sub - POOL_HALO:, :]

    def stage2(s):
        r0 = s * sub
        x = xbuf[r0:r0 + sub, :] if fused else x_ref[0, r0:r0 + sub, :]
        chunk = 8 * SUBLANES
        first = halo - (CONV_KERNEL - 1)
        for q0 in range(0, sub, chunk):
            for c0 in range(0, cw, LANES):
                acc = jnp.broadcast_to(cb_ref[:, c0:c0 + LANES], (chunk, LANES))
                for k in range(CONV_KERNEL):
                    off = q0 + (first + k) * SUBLANES
                    acc = acc + cw_ref[k:k + 1, c0:c0 + LANES] * ebuf[s, off:off + chunk, c0:c0 + LANES]
                cbuf[s, q0:q0 + chunk, c0:c0 + LANES] = acc
        yc = _layer_norm_rows(cbuf[s], cg_ref[...], cbeta_ref[...])
        ya_b = (yc * jax.nn.sigmoid(yc)).astype(BF16)
        ya = _dot(permt_ref[...], ya_b).astype(BF16)

        ext = pbuf[s]
        p = ext[POOL_HALO:, :]
        w_sum = ext + pltpu.roll(ext, 1, 0)
        parts = [w_sum[:, :gw]]
        shift = 2
        for _ in range(len(POOL_WINDOWS) - 1):
            w_sum = w_sum[:, gw:]
            w_sum = w_sum + pltpu.roll(w_sum, shift, 0)
            parts.append(w_sum[:, :gw])
            shift *= 2
        wsum = jnp.concatenate(parts, axis=1)[POOL_HALO:, :]
        pos = (lax.broadcasted_iota(I32, (POOL_HALO, gw), 0) + (i * ts + r0 + 1)).astype(F32)
        cnt = jnp.concatenate([jnp.minimum(pos, float(wlen)) for wlen in POOL_WINDOWS], axis=1)
        inv = jnp.concatenate([jnp.full((1, gw), 1.0 / wlen, F32) for wlen in POOL_WINDOWS], axis=1)
        mean = jnp.concatenate([wsum[:POOL_HALO] / cnt, wsum[POOL_HALO:] * inv], axis=0)
        dpool = (mean - p).astype(BF16)
        yb_parts = [_dot(dpool[:, gi * gw:(gi + 1) * gw], pw_ref[gi]) for gi in range(len(POOL_WINDOWS))]
        yb = (jnp.concatenate(yb_parts, axis=1) * ps_ref[...]).astype(BF16)

        mix = _dot(jnp.concatenate([ya, yb], axis=1), wout_ref[...]) + bout_ref[...]
        x1 = _deepnorm_rows(alpha, x, 1.0 + g1, mix, l1g_ref[...], l1b_ref[...])
        x1_ref[0, r0:r0 + sub, :] = x1
        u2 = x1 * (1.0 + sc2) + sh2
        u2_ref[0, r0:r0 + sub, :] = _pack_halves(u2)

        logits = _dot(u2.astype(BF16), wr_ref[...]).T[0:n_experts, :]
        scores = jax.nn.sigmoid(logits)
        sel = scores + rb_ref[...]
        epg = n_experts // N_EXPERT_GROUPS
        io = lax.broadcasted_iota(I32, (epg, sub), 0).astype(F32)
        neg = jnp.float32(-jnp.inf)
        best = None
        for g in range(N_EXPERT_GROUPS):
            sg = sel[g * epg:(g + 1) * epg, :]
            scg = scores[g * epg:(g + 1) * epg, :]
            m1 = jnp.max(sg, axis=0, keepdims=True)
            i1 = jnp.min(jnp.where(sg == m1, io, float(epg)), axis=0, keepdims=True)
            rest = jnp.where(io == i1, neg, sg)
            m2 = jnp.max(rest, axis=0, keepdims=True)
            i2 = jnp.min(jnp.where(jnp.logical_and(rest == m2, io != i1), io, float(epg)), axis=0, keepdims=True)
            s1 = jnp.sum(jnp.where(io == i1, scg, 0.0), axis=0, keepdims=True)
            s2 = jnp.sum(jnp.where(io == i2, scg, 0.0), axis=0, keepdims=True)
            gs = m1 + m2
            cand = (gs, i1 + float(g * epg), i2 + float(g * epg), s1, s2)
            if best is None:
                best = cand
            else:
                upd = gs > best[0]
                best = tuple(jnp.where(upd, cn, bs) for cn, bs in zip(cand, best))
        _, e1, e2, s1, s2 = best
        denom = s1 + s2
        w1 = s1 / denom
        w2 = s2 / denom

        ioe = lax.broadcasted_iota(I32, (n_experts, sub), 0).astype(F32)
        hit1 = ioe == e1
        hit2 = ioe == e2
        onehot = jnp.where(jnp.logical_or(hit1, hit2), 1.0, 0.0)
        before = _dot(onehot.astype(BF16), tri_ref[...]) + carry[:, 0:1]
        r1 = jnp.sum(jnp.where(hit1, before, 0.0), axis=0, keepdims=True)
        r2 = jnp.sum(jnp.where(hit2, before, 0.0), axis=0, keepdims=True)
        carry[...] = carry[...] + jnp.sum(onehot, axis=1, keepdims=True)

        e_ref[0:1, r0:r0 + sub] = e1.astype(I32)
        e_ref[1:2, r0:r0 + sub] = e2.astype(I32)
        rank_ref[0:1, r0:r0 + sub] = r1.astype(I32)
        rank_ref[1:2, r0:r0 + sub] = r2.astype(I32)
        rowi = lax.broadcasted_iota(I32, (LANES, sub), 0)
        wpad = jnp.where(rowi == 0, w1, jnp.where(rowi == 1, w2, 0.0))
        wm_ref[r0:r0 + sub, :] = wpad.T

    nsub = ts // sub
    for s in range(min(STAGE_LEAD, nsub)):
        stage1(s)
    for s in range(nsub):
        if s + STAGE_LEAD < nsub:
            stage1(s + STAGE_LEAD)
        stage2(s)
    cnt_ref[...] = carry[...]


def _row_perm(n):
    r = jnp.arange(n)
    t = (n // SUBLANES) * (r % SUBLANES) + r // SUBLANES
    return (t[:, None] == jnp.arange(n)[None, :]).astype(BF16)


def _mixer(x, mod, lw, wr, rbias, alpha, bsz, x_b0, mod_b0, prev=None):
    _, seq, d = x.shape
    ts = min(SEQ_TILE, seq)
    sub = min(SEQ_SUB_TILE, ts)
    nsub = ts // sub
    ns = seq // ts
    n_tok = bsz * seq
    n_experts = rbias.shape[0]
    cw = lw["conv_w"].shape[1]
    pw = lw["pool_scale"].shape[0]
    perm = _row_perm(sub)
    tri = (jnp.arange(sub)[:, None] < jnp.arange(sub)[None, :]).astype(BF16)
    row = lambda a: a.reshape(1, -1)
    full = lambda shape: pl.BlockSpec(shape, lambda b, i: (0,) * len(shape))
    if prev is None:
        ins = [(x, pl.BlockSpec((1, ts, d), lambda b, i: (b + x_b0, i, 0)))]
    else:
        yy, wmp, modp, l2g, l2b = prev
        nblk = n_tok // ts
        ins = [
            (x, pl.BlockSpec((1, ts, d), lambda b, i: (b + x_b0, i, 0))),
            (yy, pl.BlockSpec((ts, d // 2), lambda b, i: (b * ns + i, 0))),
            (yy, pl.BlockSpec((ts, d // 2), lambda b, i: (nblk + b * ns + i, 0))),
            (wmp, pl.BlockSpec((ts, LANES), lambda b, i: (b * ns + i, 0))),
            (modp, pl.BlockSpec((1, 6, d), lambda b, i: (b + mod_b0, 0, 0))),
            (row(l2g), None), (row(l2b), None),
        ]
    ins += [
        (mod, pl.BlockSpec((1, 6, d), lambda b, i: (b + mod_b0, 0, 0))),
        (lw["w_in"], None), (row(lw["b_in"]), None),
        (jnp.pad(lw["conv_w"], ((0, 1), (0, 0))), None), (row(lw["conv_b"]), None),
        (row(lw["conv_ln_g"]), None), (row(lw["conv_ln_b"]), None),
        (lw["pool_w"], None), (row(lw["pool_scale"]), None),
        (lw["w_out"], None), (row(lw["b_out"]), None),
        (row(lw["ln1_g"]), None), (row(lw["ln1_b"]), None),
        (wr, None), (rbias.reshape(-1, 1), None),
        (perm, None), (perm.T, None), (tri, None),
    ]
    args = [a for a, _ in ins]
    specs = [s if s is not None else full(a.shape) for a, s in ins]
    out_shape = (
        jax.ShapeDtypeStruct((bsz, seq, d), F32),
        jax.ShapeDtypeStruct((bsz, seq, d // 2), U32),
        jax.ShapeDtypeStruct((2, n_tok), I32),
        jax.ShapeDtypeStruct((2, n_tok), I32),
        jax.ShapeDtypeStruct((n_tok, LANES), F32),
        jax.ShapeDtypeStruct((n_experts, LANES), F32),
    )
    out_specs = (
        pl.BlockSpec((1, ts, d), lambda b, i: (b, i, 0)),
        pl.BlockSpec((1, ts, d // 2), lambda b, i: (b, i, 0)),
        pl.BlockSpec((2, ts), lambda b, i: (0, b * ns + i)),
        pl.BlockSpec((2, ts), lambda b, i: (0, b * ns + i)),
        pl.BlockSpec((ts, LANES), lambda b, i: (b * ns + i, 0)),
        pl.BlockSpec((n_experts, LANES), lambda b, i: (0, 0)),
    )
    return pl.pallas_call(
        functools.partial(_mixer_body, alpha, n_experts, sub, prev is not None),
        grid=(bsz, ns),
        in_specs=specs,
        out_specs=out_specs,
        out_shape=out_shape,
        scratch_shapes=[
            pltpu.VMEM((nsub, CONV_HALO_VREGS * SUBLANES + sub, cw), F32),
            pltpu.VMEM((nsub, sub, cw), F32),
            pltpu.VMEM((nsub, POOL_HALO + sub, pw), F32),
            pltpu.VMEM((CONV_HALO_VREGS * SUBLANES, cw), F32),
            pltpu.VMEM((POOL_HALO, pw), F32),
            pltpu.VMEM((n_experts, LANES), F32),
            pltpu.VMEM((ts, d) if prev is not None else (SUBLANES, LANES), F32),
        ],
        compiler_params=pltpu.CompilerParams(dimension_semantics=("arbitrary", "arbitrary"), vmem_limit_bytes=VMEM_LIMIT),
        name="mixer_router",
    )(*args)


def _dest_body(n_experts, pstart_ref, e_ref, rank_ref, dest_ref):
    e = e_ref[...]
    dest = rank_ref[...]
    for ex in range(n_experts):
        dest = dest + jnp.where(e == ex, pstart_ref[ex], 0)
    dest_ref[...] = dest


def _dest_rows(pstart, eidx, rank):
    n_tok = eidx.shape[1]
    tn = min(8192, n_tok)
    grid_spec = pltpu.PrefetchScalarGridSpec(
        num_scalar_prefetch=1,
        grid=(n_tok // tn,),
        in_specs=[pl.BlockSpec((2, tn), lambda i, ps: (0, i)), pl.BlockSpec((2, tn), lambda i, ps: (0, i))],
        out_specs=pl.BlockSpec((2, tn), lambda i, ps: (0, i)),
    )
    return pl.pallas_call(
        functools.partial(_dest_body, pstart.shape[0]),
        grid_spec=grid_spec,
        out_shape=jax.ShapeDtypeStruct((2, n_tok), I32),
        compiler_params=pltpu.CompilerParams(dimension_semantics=("arbitrary",)),
        name="dest_rows",
    )(pstart, eidx, rank)


def _sc_mesh():
    return plsc.VectorSubcoreMesh(core_axis_name="c", subcore_axis_name="s",
                                  num_cores=SC_CORES, num_subcores=SC_SUBCORES)


def _dispatch(u2, d0, d1, n_rows):
    n_tok, d = u2.shape
    workers = SC_CORES * SC_SUBCORES
    per_w = n_tok // workers
    ch = SC_ROWS_PER_STEP
    nch = per_w // ch

    @functools.partial(
        pl.kernel, mesh=_sc_mesh(),
        out_type=jax.ShapeDtypeStruct((n_rows, d), u2.dtype),
        scratch_types=[pltpu.VMEM((nch, ch), I32), pltpu.VMEM((nch, ch), I32), pltpu.VMEM((2, ch, d), u2.dtype),
                       pltpu.SemaphoreType.DMA((2,)), pltpu.SemaphoreType.DMA((2,))],
        name="sc_dispatch",
    )
    def run(u_hbm, d0_hbm, d1_hbm, xs_hbm, i0, i1, rows, rsem, ssem):
        wid = lax.axis_index("s") * SC_CORES + lax.axis_index("c")
        base = wid * per_w
        pltpu.sync_copy(d0_hbm.at[pl.ds(wid * nch, nch)], i0)
        pltpu.sync_copy(d1_hbm.at[pl.ds(wid * nch, nch)], i1)

        def read(j, b):
            return pltpu.make_async_copy(u_hbm.at[pl.ds(base + j * ch, ch)], rows.at[b], rsem.at[b])

        def scatter(idx, j, b):
            return pltpu.make_async_copy(rows.at[b], xs_hbm.at[idx.at[j]], ssem.at[b])

        read(0, 0).start()

        @pl.loop(0, nch, step=2)
        def _(j0):
            for b in range(2):
                j = j0 + b
                read(j, b).wait()
                scatter(i0, j, b).start()
                scatter(i1, j, b).start()

                @pl.when(j >= 1)
                def _():
                    scatter(i0, j - 1, 1 - b).wait()
                    scatter(i1, j - 1, 1 - b).wait()

                @pl.when(j + 1 < nch)
                def _():
                    read(j + 1, 1 - b).start()

        scatter(i0, nch - 1, 1).wait()
        scatter(i1, nch - 1, 1).wait()

    return run(u2, d0.reshape(n_tok // ch, ch), d1.reshape(n_tok // ch, ch))


def _gather_rows(ys, dd):
    n_out = dd.shape[0]
    d = ys.shape[1]
    workers = SC_CORES * SC_SUBCORES
    per_w = n_out // workers
    ch = SC_ROWS_PER_STEP
    nch = per_w // ch

    @functools.partial(
        pl.kernel, mesh=_sc_mesh(),
        out_type=jax.ShapeDtypeStruct((n_out, d), ys.dtype),
        scratch_types=[pltpu.VMEM((nch, ch), I32), pltpu.VMEM((2, ch, d), ys.dtype),
                       pltpu.SemaphoreType.DMA((2,)), pltpu.SemaphoreType.DMA((2,))],
        name="sc_gather",
    )
    def run(ys_hbm, dd_hbm, out_hbm, idx, rows, gsem, wsem):
        wid = lax.axis_index("s") * SC_CORES + lax.axis_index("c")
        base = wid * per_w
        pltpu.sync_copy(dd_hbm.at[pl.ds(wid * nch, nch)], idx)

        def gather(j, b):
            return pltpu.make_async_copy(ys_hbm.at[idx.at[j]], rows.at[b], gsem.at[b])

        def write(j, b):
            return pltpu.make_async_copy(rows.at[b], out_hbm.at[pl.ds(base + j * ch, ch)], wsem.at[b])

        gather(0, 0).start()

        @pl.loop(0, nch, step=2)
        def _(j0):
            for b in range(2):
                j = j0 + b
                gather(j, b).wait()
                write(j, b).start()

                @pl.when(j >= 1)
                def _():
                    write(j - 1, 1 - b).wait()

                @pl.when(j + 1 < nch)
                def _():
                    gather(j + 1, 1 - b).start()

        write(nch - 1, 1).wait()

    return run(ys, dd.reshape(n_out // ch, ch))


def _expert_body(layer, blk_ref, slot_ref, next_ref, nused_ref, xs_ref, wg_hbm, wu_hbm, wd_hbm, ys_ref,
                 wg32, wu32, wd32, wgb, wub, wdb, sem):
    j = pl.program_id(0)
    used = j < nused_ref[0]
    e = blk_ref[j]
    slot = slot_ref[j]
    nxt = next_ref[j]
    new_expert = jnp.logical_or(j == 0, e != blk_ref[jnp.maximum(j - 1, 0)])

    def fetch(ex, sl):
        return (pltpu.make_async_copy(wg_hbm.at[layer, ex], wg32.at[sl], sem.at[sl, 0]),
                pltpu.make_async_copy(wu_hbm.at[layer, ex], wu32.at[sl], sem.at[sl, 1]),
                pltpu.make_async_copy(wd_hbm.at[layer, ex], wd32.at[sl], sem.at[sl, 2]))

    @pl.when(j == 0)
    def _():
        for cp in fetch(e, slot):
            cp.start()

    @pl.when(jnp.logical_and(used, new_expert))
    def _():
        for cp in fetch(e, slot):
            cp.wait()

        @pl.when(nxt >= 0)
        def _():
            for cp in fetch(nxt, 1 - slot):
                cp.start()

        wgb[...] = wg32[slot].astype(BF16)
        wub[...] = wu32[slot].astype(BF16)
        wdb[...] = wd32[slot].astype(BF16)

    @pl.when(used)
    def _():
        lo, hi = _unpack_halves(xs_ref[...])
        lo = lo.astype(BF16)
        hi = hi.astype(BF16)
        half = lo.shape[1]
        fw = wgb.shape[1] // EXPERT_F_SPLIT
        acc = None
        for c in range(EXPERT_F_SPLIT):
            cs = slice(c * fw, (c + 1) * fw)
            g = _dot(lo, wgb[0:half, cs]) + _dot(hi, wgb[half:, cs])
            up = _dot(lo, wub[0:half, cs]) + _dot(hi, wub[half:, cs])
            hid = (g * jax.nn.sigmoid(g) * up).astype(BF16)
            part = _dot(hid, wdb[cs, :])
            acc = part if acc is None else acc + part
        ys_ref[...] = _pack_halves(acc)

    @pl.when(j >= nused_ref[0])
    def _():
        ys_ref[...] = jnp.zeros(ys_ref.shape, U32)


def _experts(blk_e, blk_slot, blk_next, nused, xs, wg, wu, wd, layer):
    n_rows = xs.shape[0]
    d = wg.shape[2]
    f = wg.shape[3]
    tm = EXPERT_TILE
    grid_spec = pltpu.PrefetchScalarGridSpec(
        num_scalar_prefetch=4,
        grid=(n_rows // tm,),
        in_specs=[
            pl.BlockSpec((tm, d // 2), lambda j, be, sl, nx, nu: (jnp.minimum(j, nu[0] - 1), 0)),
            pl.BlockSpec(memory_space=pl.ANY),
            pl.BlockSpec(memory_space=pl.ANY),
            pl.BlockSpec(memory_space=pl.ANY),
        ],
        out_specs=pl.BlockSpec((tm, d // 2), lambda j, be, sl, nx, nu: (j, 0)),
        scratch_shapes=[pltpu.VMEM((2, d, f), F32), pltpu.VMEM((2, d, f), F32), pltpu.VMEM((2, f, d), F32),
                        pltpu.VMEM((d, f), BF16), pltpu.VMEM((d, f), BF16), pltpu.VMEM((f, d), BF16),
                        pltpu.SemaphoreType.DMA((2, 3))],
    )
    return pl.pallas_call(
        functools.partial(_expert_body, layer),
        grid_spec=grid_spec,
        out_shape=jax.ShapeDtypeStruct((n_rows, d // 2), U32),
        compiler_params=pltpu.CompilerParams(dimension_semantics=("arbitrary",), vmem_limit_bytes=VMEM_LIMIT),
        name="expert_ffn",
    )(blk_e, blk_slot, blk_next, nused, xs, wg, wu, wd)


def _combine_body(alpha, x1_ref, y0_ref, y1_ref, mod_ref, wm_ref, g_ref, b_ref, *rest):
    o_ref = rest[-1]
    o_ref[...] = _combine_rows(alpha, x1_ref[...], y0_ref[...], y1_ref[...], wm_ref[...],
                               mod_ref[0][5:6], g_ref[...], b_ref[...])


def _combine(x1, yy, mod, wm, ln_g, ln_b, seq, alpha, mod_b0, out_rows, out_row0, out_prev):
    n_tok, d = x1.shape
    tn = min(COMBINE_TILE, seq)
    per_seq = seq // tn
    nblk = n_tok // tn
    blk0 = out_row0 // tn
    rows = lambda: pl.BlockSpec((tn, d), lambda i: (i, 0))
    in_specs = [rows(), pl.BlockSpec((tn, d // 2), lambda i: (i, 0)),
                pl.BlockSpec((tn, d // 2), lambda i: (nblk + i, 0)),
                pl.BlockSpec((1, 6, d), lambda i: (i // per_seq + mod_b0, 0, 0)),
                pl.BlockSpec((tn, LANES), lambda i: (i, 0)),
                pl.BlockSpec((1, d), lambda i: (0, 0)),
                pl.BlockSpec((1, d), lambda i: (0, 0))]
    args = [x1, yy, yy, mod, wm, ln_g.reshape(1, d), ln_b.reshape(1, d)]
    aliases = {}
    if out_prev is not None:
        in_specs.append(pl.BlockSpec(memory_space=pl.ANY))
        args.append(out_prev)
        aliases = {len(args) - 1: 0}
    return pl.pallas_call(
        functools.partial(_combine_body, alpha),
        grid=(nblk,),
        in_specs=in_specs,
        out_specs=pl.BlockSpec((tn, d), lambda i: (blk0 + i, 0)),
        out_shape=jax.ShapeDtypeStruct((out_rows, d), F32),
        input_output_aliases=aliases,
        compiler_params=pltpu.CompilerParams(dimension_semantics=("arbitrary",), vmem_limit_bytes=VMEM_LIMIT),
        name="combine_ln",
    )(*args)


def kernel(x, c, w_ada, b_ada, w_in, b_in, conv_w, conv_b, conv_ln_g, conv_ln_b, pool_w, pool_scale, w_out, b_out, ln1_g, ln1_b, w_router, router_bias, w_gate, w_up, w_down, ln2_g, ln2_b):
    bsz, seq, d = x.shape
    depth = w_ada.shape[0]
    n_experts = w_router.shape[1]
    alpha = float((2 * depth) ** 0.25)
    tm = EXPERT_TILE
    n_chunks = BATCH_CHUNKS if bsz % BATCH_CHUNKS == 0 else 1
    bc = bsz // n_chunks
    n_tok = bc * seq
    n_rows = 2 * n_tok + n_experts * tm

    mod_all = _ada_mod(c, w_ada, b_ada).reshape(depth, bsz, 6, d)
    wr = jnp.pad(w_router.astype(BF16), ((0, 0), (0, LANES - n_experts)))

    chunks = [x] * n_chunks
    starts = [ci * bc for ci in range(n_chunks)]
    prevs = [None] * n_chunks
    out = None
    for l in range(depth):
        lw = dict(w_in=w_in[l].astype(BF16), b_in=b_in[l], conv_w=conv_w[l], conv_b=conv_b[l],
                  conv_ln_g=conv_ln_g[l], conv_ln_b=conv_ln_b[l], pool_w=pool_w[l].astype(BF16),
                  pool_scale=pool_scale[l], w_out=w_out[l].astype(BF16), b_out=b_out[l],
                  ln1_g=ln1_g[l], ln1_b=ln1_b[l])
        mod = mod_all[l]
        last = l == depth - 1
        st = []
        for ci in range(n_chunks):
            x1, u2, eidx, rank, wm, cnt = _mixer(chunks[ci], mod, lw, wr, router_bias, alpha,
                                                 bc, starts[ci], ci * bc, prevs[ci])
            counts = cnt[:, 0].astype(I32)
            tiles = (counts + tm - 1) // tm
            tile_end = jnp.cumsum(tiles)
            pstart = ((tile_end - tiles) * tm).astype(I32)
            nused = tile_end[-1:].astype(I32)
            blk_e = jnp.minimum(
                jnp.sum(tile_end[None, :] <= jnp.arange(n_rows // tm, dtype=I32)[:, None], axis=1), n_experts - 1
            ).astype(I32)
            in_use = tiles > 0
            slot_e = (jnp.cumsum(in_use.astype(I32)) - 1) % 2
            ids = jnp.where(in_use, jnp.arange(n_experts, dtype=I32), n_experts)
            after = jnp.concatenate([lax.cummin(ids, reverse=True)[1:], jnp.full((1,), n_experts, I32)])
            next_e = jnp.where(after < n_experts, after, -1).astype(I32)
            dest = _dest_rows(pstart, eidx, rank)
            xs = _dispatch(u2.reshape(n_tok, d // 2), dest[0], dest[1], n_rows)
            of_tile = blk_e[:, None] == jnp.arange(n_experts, dtype=I32)[None, :]
            blk_slot = jnp.sum(jnp.where(of_tile, slot_e[None, :], 0), axis=1).astype(I32)
            blk_next = jnp.sum(jnp.where(of_tile, next_e[None, :], 0), axis=1).astype(I32)
            st.append((x1, wm, (blk_e, blk_slot, blk_next), nused, dest, xs))
        ys = [_experts(*plan, nused, xs, w_gate, w_up, w_down, l) for (_, _, plan, nused, _, xs) in st]
        yy = [_gather_rows(ys[ci], st[ci][4].reshape(2 * n_tok)) for ci in range(n_chunks)]
        for ci in range(n_chunks):
            x1, wm = st[ci][0], st[ci][1]
            if last:
                out = _combine(x1.reshape(n_tok, d), yy[ci], mod, wm, ln2_g[l], ln2_b[l], seq, alpha,
                               ci * bc, bsz * seq, ci * n_tok, out)
            else:
                chunks[ci] = x1
                starts[ci] = 0
                prevs[ci] = (yy[ci], wm, mod, ln2_g[l], ln2_b[l])
    return out.reshape(bsz, seq, d)
```
